```python
import math
import jax, jax.numpy as jnp
from jax import lax
import numpy as np

D_MODEL = 1024
BATCH = 16
SEQ = 256
DEPTH = 4
DEC_BATCH = 8
DEC_SEQ = 1024
PAST_LEN = 256

GRID_W = 64
HEAD_DIM = 64
GROUP_W = D_MODEL // 4
RET_HEADS = GROUP_W // HEAD_DIM
RET_CHUNK = 128
LRU_WIDTH = GROUP_W
LRU_BLOCKS = 4
LRU_BLOCK_W = LRU_WIDTH // LRU_BLOCKS
LRU_CONV_W = 4
LRU_C = 8.0
SWA_HEADS = GROUP_W // HEAD_DIM
SWA_KV_HEADS = 2
SWA_WINDOW = 128
ATTN_BLOCK = 128
DIFF_HEADS = GROUP_W // HEAD_DIM
DIFF_QK_DIM = HEAD_DIM // 2
D_FF = 4 * D_MODEL
ROPE_BASE = 10000.0
EPS = 1e-6
N_MOD = 6
IN_SIZES = (GROUP_W, GROUP_W, GROUP_W, GROUP_W,
            LRU_WIDTH, LRU_WIDTH,
            SWA_HEADS * HEAD_DIM, SWA_KV_HEADS * HEAD_DIM, SWA_KV_HEADS * HEAD_DIM,
            DIFF_HEADS * HEAD_DIM, DIFF_HEADS * HEAD_DIM, DIFF_HEADS * HEAD_DIM)
D_IN = sum(IN_SIZES)

kernel_name = "hybrid_diffusion_parallel_heads_step"

F32 = jnp.float32


def rmsnorm(x, g):
    xf = x.astype(F32)
    y = xf * lax.rsqrt(jnp.mean(xf * xf, -1, keepdims=True) + EPS)
    return (y * g.astype(F32)).astype(x.dtype)


def head_groupnorm(o, g):
    B, L, H, d = o.shape
    of = o.astype(F32)
    mu = jnp.mean(of, -1, keepdims=True)
    var = jnp.mean(jnp.square(of - mu), -1, keepdims=True)
    y = (of - mu) * lax.rsqrt(var + EPS)
    return y.reshape(B, L, H * d) * g.astype(F32)


def head_rmsnorm(o, g):
    of = o.astype(F32)
    y = of * lax.rsqrt(jnp.mean(of * of, -1, keepdims=True) + EPS) * g.astype(F32)
    B, L, H, d = o.shape
    return y.reshape(B, L, H * d)


def modulation(cvec, w, b):
    m = (jax.nn.silu(cvec) @ w + b)[..., None, :]
    return jnp.split(m, N_MOD, axis=-1)


def axial_rope(length, dim):
    rows = length // GRID_W
    row = jnp.repeat(jnp.arange(rows, dtype=F32), GRID_W)
    col = (jnp.arange(length) % GRID_W).astype(F32)
    n = dim // 4
    inv = ROPE_BASE ** (-jnp.arange(n, dtype=F32) / n)
    ang = jnp.concatenate([row[:, None] * inv, col[:, None] * inv], -1)
    return jnp.cos(ang), jnp.sin(ang)


def apply_rope(x, cos, sin):
    x1, x2 = jnp.split(x.astype(F32), 2, -1)
    c = cos[None, :, None, :]
    s = sin[None, :, None, :]
    return jnp.concatenate([x1 * c - x2 * s, x1 * s + x2 * c], -1).astype(x.dtype)


def retention_scan(q, k, v, log_gamma, s0):
    B, L, H, dk = q.shape
    dv = v.shape[-1]
    nc = L // RET_CHUNK
    lg = log_gamma.astype(F32)
    idx = jnp.arange(RET_CHUNK, dtype=F32)
    dist = idx[:, None] - idx[None, :]
    decay_in = jnp.exp(jnp.where((dist >= 0)[None], dist[None] * lg[:, None, None], -jnp.inf))
    xi = jnp.exp((idx[:, None] + 1.0) * lg[None, :])
    zeta = jnp.exp((RET_CHUNK - 1.0 - idx)[:, None] * lg[None, :])
    g_chunk = jnp.exp(RET_CHUNK * lg)

    def to_chunks(t):
        return t.astype(F32).reshape(B, nc, RET_CHUNK, H, t.shape[-1]).swapaxes(0, 1)

    def step(S, inp):
        qc, kc, vc = inp
        scores = jnp.einsum('bnhd,bmhd->bhnm', qc, kc) * decay_in[None]
        o = (jnp.einsum('bhnm,bmhe->bnhe', scores, vc)
             + jnp.einsum('bnhd,bhde->bnhe', qc, S) * xi[None, :, :, None])
        S = S * g_chunk[None, :, None, None] + jnp.einsum('bmhd,bmhe->bhde', kc * zeta[None, :, :, None], vc)
        return S, o

    S, o = lax.scan(step, s0.astype(F32), (to_chunks(q), to_chunks(k), to_chunks(v)))
    return o.swapaxes(0, 1).reshape(B, L, H, dv), S


def retention_mixer(q, k, v, g, decay_logit, gn_g, s0):
    B, L, _ = q.shape
    q = q.reshape(B, L, RET_HEADS, HEAD_DIM)
    k = k.reshape(B, L, RET_HEADS, HEAD_DIM) * (HEAD_DIM ** -0.5)
    v = v.reshape(B, L, RET_HEADS, HEAD_DIM)
    log_gamma = jax.nn.log_sigmoid(decay_logit.astype(F32))
    o_f, s_f = retention_scan(q, k, v, log_gamma[0], s0[:, 0])
    o_b, s_b = retention_scan(q[:, ::-1], k[:, ::-1], v[:, ::-1], log_gamma[1], s0[:, 1])
    y = head_groupnorm(o_f + o_b[:, ::-1], gn_g) * jax.nn.silu(g.astype(F32))
    return y, jnp.stack([s_f, s_b], 1)


def depthwise_conv_centred(x, w, b):
    y = lax.conv_general_dilated(x, w[:, None, :].astype(x.dtype), window_strides=(1,),
                                 padding=[(LRU_CONV_W // 2, LRU_CONV_W - 1 - LRU_CONV_W // 2)],
                                 dimension_numbers=('NWC', 'WIO', 'NWC'), feature_group_count=x.shape[-1])
    return y + b


def linear_scan(a, b, h0):
    def combine(left, right):
        a1, b1 = left
        a2, b2 = right
        return a1 * a2, a2 * b1 + b2
    a_cum, b_cum = lax.associative_scan(combine, (a, b), axis=1)
    return a_cum * h0[:, None, :] + b_cum


def rglru_direction(xc, w_a, b_a, w_x, b_x, lam, h0):
    B, L, C = xc.shape
    xb = xc.reshape(B, L, LRU_BLOCKS, LRU_BLOCK_W)
    r = jax.nn.sigmoid((jnp.einsum('blnc,ncd->blnd', xb, w_a).reshape(B, L, C) + b_a).astype(F32))
    i = jax.nn.sigmoid((jnp.einsum('blnc,ncd->blnd', xb, w_x).reshape(B, L, C) + b_x).astype(F32))
    log_a = -LRU_C * r * jax.nn.softplus(-lam.astype(F32))
    a = jnp.exp(log_a)
    u = jnp.sqrt(-jnp.expm1(2.0 * log_a)) * (i * xc.astype(F32))
    h = linear_scan(a, u, h0.astype(F32))
    return h, h[:, -1]


def rglru_mixer(x_in, gate_in, conv_w, conv_b, w_a, b_a, w_x, b_x, lam, h0):
    xc = depthwise_conv_centred(x_in, conv_w, conv_b)
    h_f, hT_f = rglru_direction(xc, w_a[0], b_a[0], w_x[0], b_x[0], lam[0], h0[:, 0])
    h_b, hT_b = rglru_direction(xc[:, ::-1], w_a[1], b_a[1], w_x[1], b_x[1], lam[1], h0[:, 1])
    y = (h_f + h_b[:, ::-1]) * jax.nn.gelu(gate_in.astype(F32))
    return y, jnp.stack([hT_f, hT_b], 1)


def sink_attention_dense(q, k, v, sink):
    B, Lq, Hq, d = q.shape
    Hkv = k.shape[2]
    G = Hq // Hkv
    nb = Lq // ATTN_BLOCK
    qb = q.reshape(B, nb, ATTN_BLOCK, Hkv, G, d).swapaxes(0, 1)
    sink_l = sink.astype(F32).reshape(Hkv, G)

    def block(qi):
        s = jnp.einsum('bqkgd,bskd->bkgqs', qi, k).astype(F32) * (d ** -0.5)
        s = jnp.concatenate([s, jnp.broadcast_to(sink_l[None, :, :, None, None], s.shape[:-1] + (1,))], -1)
        p = jax.nn.softmax(s, -1)[..., :-1]
        return jnp.einsum('bkgqs,bskd->bqkgd', p.astype(v.dtype), v)

    o = lax.map(block, qb)
    return o.swapaxes(0, 1).reshape(B, Lq, Hq * d)


def window_sink_attention(q, k, v, k_ctx, v_ctx, sink):
    B, L, Hq, d = q.shape
    Hkv = k.shape[2]
    G = Hq // Hkv
    BLK = ATTN_BLOCK
    nb = L // BLK
    qb = q.reshape(B, nb, BLK, Hkv, G, d)

    def band(t):
        tp = jnp.pad(t, ((0, 0), (BLK, BLK), (0, 0), (0, 0))).reshape(B, nb + 2, BLK, Hkv, d)
        return jnp.concatenate([tp[:, :-2], tp[:, 1:-1], tp[:, 2:]], axis=2)

    kw, vw = band(k), band(v)
    qpos = jnp.arange(nb)[:, None] * BLK + jnp.arange(BLK)[None, :]
    kpos = (jnp.arange(nb)[:, None] - 1) * BLK + jnp.arange(3 * BLK)[None, :]
    rel = kpos[:, None, :] - qpos[:, :, None]
    valid = (jnp.abs(rel) <= SWA_WINDOW) & (kpos[:, None, :] >= 0) & (kpos[:, None, :] < L)
    scale = d ** -0.5
    s_loc = jnp.einsum('bnqkgd,bnskd->bnkgqs', qb, kw).astype(F32) * scale
    s_loc = jnp.where(valid[None, :, None, None], s_loc, -jnp.inf)
    s_ctx = jnp.einsum('bnqkgd,bskd->bnkgqs', qb, k_ctx).astype(F32) * scale
    s_sink = jnp.broadcast_to(sink.astype(F32).reshape(1, 1, Hkv, G, 1, 1), s_loc.shape[:-1] + (1,))
    p = jax.nn.softmax(jnp.concatenate([s_loc, s_ctx, s_sink], -1), -1)
    p_loc = p[..., :3 * BLK].astype(v.dtype)
    p_ctx = p[..., 3 * BLK:-1].astype(v.dtype)
    o = (jnp.einsum('bnkgqs,bnskd->bnqkgd', p_loc, vw)
         + jnp.einsum('bnkgqs,bskd->bnqkgd', p_ctx, v_ctx.astype(v.dtype)))
    return o.reshape(B, L, Hq * d)


def diff_attention(q, k, v, lam, norm_g, lambda_init):
    B, Lq, H, _ = q.shape
    Lk = k.shape[1]
    nb = Lq // ATTN_BLOCK
    qb = q.reshape(B, nb, ATTN_BLOCK, H, 2, DIFF_QK_DIM).swapaxes(0, 1)
    kk = k.reshape(B, Lk, H, 2, DIFF_QK_DIM)
    scale = DIFF_QK_DIM ** -0.5

    def block(qi):
        s = jnp.einsum('bqhcd,bshcd->bchqs', qi, kk).astype(F32) * scale
        p = jax.nn.softmax(s, -1)
        w = p[:, 0] - lam * p[:, 1]
        return jnp.einsum('bhqs,bshe->bqhe', w.astype(v.dtype), v)

    o = lax.map(block, qb).swapaxes(0, 1).reshape(B, Lq, H, HEAD_DIM)
    return head_rmsnorm(o, norm_g) * (1.0 - lambda_init)


def trunk_layer(x, cvec, l, p, ctx):
    B, L, _ = x.shape
    sh1, sc1, g1, sh2, sc2, g2 = modulation(cvec, p['w_ada'], p['b_ada'])
    h = rmsnorm(x, p['norm_mix_g']) * (1.0 + sc1) + sh1
    z = h @ p['w_in']
    points, acc = [], 0
    for s in IN_SIZES[:-1]:
        acc += s
        points.append(acc)
    rq, rk, rv, rg, lx, lgate, sq, sk, sv, dq, dk, dv = jnp.split(z, points, axis=-1)
    sq = sq.reshape(B, L, SWA_HEADS, HEAD_DIM)
    sk = sk.reshape(B, L, SWA_KV_HEADS, HEAD_DIM)
    sv = sv.reshape(B, L, SWA_KV_HEADS, HEAD_DIM)
    dq = dq.reshape(B, L, DIFF_HEADS, HEAD_DIM)
    dk = dk.reshape(B, L, DIFF_HEADS, HEAD_DIM)
    dv = dv.reshape(B, L, DIFF_HEADS, HEAD_DIM)

    lambda_init = 0.8 - 0.6 * math.exp(-0.3 * l)
    lv = p['diff_lambda'].astype(F32)
    lam = jnp.exp(jnp.sum(lv[0] * lv[1])) - jnp.exp(jnp.sum(lv[2] * lv[3])) + lambda_init

    if ctx is None:
        s_ret_in = jnp.zeros((B, 2, RET_HEADS, HEAD_DIM, HEAD_DIM), F32)
        s_lru_in = jnp.zeros((B, 2, LRU_WIDTH), F32)
    else:
        s_ret_in, s_lru_in, ck_swa, cv_swa, ck_diff, cv_diff = ctx

    y_ret, s_ret = retention_mixer(rq, rk, rv, rg, p['ret_decay'], p['ret_gn_g'], s_ret_in)
    y_lru, s_lru = rglru_mixer(lx, lgate, p['lru_conv_w'], p['lru_conv_b'], p['lru_w_a'], p['lru_b_a'],
                               p['lru_w_x'], p['lru_b_x'], p['lru_lambda'], s_lru_in)

    if ctx is None:
        y_swa = sink_attention_dense(sq, sk, sv, p['swa_sink'])
        y_diff = diff_attention(dq, dk, dv, lam, p['diff_norm_g'], lambda_init)
        new_ctx = (s_ret, s_lru, sk, sv, dk, dv)
    else:
        cos, sin = axial_rope(L, HEAD_DIM)
        y_swa = window_sink_attention(apply_rope(sq, cos, sin), apply_rope(sk, cos, sin), sv,
                                      ck_swa.astype(sk.dtype), cv_swa, p['swa_sink'])
        cos2, sin2 = axial_rope(L, DIFF_QK_DIM)
        dq_r = apply_rope(dq.reshape(B, L, 2 * DIFF_HEADS, DIFF_QK_DIM), cos2, sin2).reshape(B, L, DIFF_HEADS, HEAD_DIM)
        dk_r = apply_rope(dk.reshape(B, L, 2 * DIFF_HEADS, DIFF_QK_DIM), cos2, sin2).reshape(B, L, DIFF_HEADS, HEAD_DIM)
        k_all = jnp.concatenate([ck_diff.astype(dk_r.dtype), dk_r], axis=1)
        v_all = jnp.concatenate([cv_diff.astype(dv.dtype), dv], axis=1)
        y_diff = diff_attention(dq_r, k_all, v_all, lam, p['diff_norm_g'], lambda_init)
        new_ctx = None

    y = jnp.concatenate([y_ret.astype(x.dtype), y_lru.astype(x.dtype),
                         y_swa.astype(x.dtype), y_diff.astype(x.dtype)], -1) @ p['w_out']
    x = x + g1 * y
    h2 = rmsnorm(x, p['norm_mlp_g']) * (1.0 + sc2) + sh2
    x = x + g2 * (jnp.square(jax.nn.relu(h2 @ p['w_ff1'])) @ p['w_ff2'])
    return x, new_ctx


def setup_inputs(seed: int = 0) -> dict:
    key = jax.random.key(seed)
    ks = jax.random.split(key, 32)
    nrm = lambda k, shape, s=1.0: jax.random.normal(k, shape, F32) * s
    e = 2.0 ** (-5.0 - jnp.arange(RET_HEADS, dtype=F32))
    ret_logit = jnp.log1p(-e) - jnp.log(e)
    u = jax.random.uniform(ks[20], (DEPTH, 2, LRU_WIDTH), F32, minval=0.9, maxval=0.999)
    a0 = u ** (1.0 / LRU_C)
    return {
        "x_prompt": nrm(ks[0], (BATCH, SEQ, D_MODEL)),
        "x_sample": nrm(ks[1], (DEC_BATCH, DEC_SEQ, D_MODEL)),
        "c": nrm(ks[2], (DEC_BATCH, D_MODEL)),
        "state_ret": nrm(ks[3], (DEC_BATCH, DEPTH, 2, RET_HEADS, HEAD_DIM, HEAD_DIM), 0.5),
        "state_lru": nrm(ks[4], (DEC_BATCH, DEPTH, 2, LRU_WIDTH), 0.5),
        "cache_swa_k": nrm(ks[5], (DEC_BATCH, DEPTH, PAST_LEN, SWA_KV_HEADS, HEAD_DIM)),
        "cache_swa_v": nrm(ks[6], (DEC_BATCH, DEPTH, PAST_LEN, SWA_KV_HEADS, HEAD_DIM)),
        "cache_diff_k": nrm(ks[7], (DEC_BATCH, DEPTH, PAST_LEN, DIFF_HEADS, HEAD_DIM)),
        "cache_diff_v": nrm(ks[8], (DEC_BATCH, DEPTH, PAST_LEN, DIFF_HEADS, HEAD_DIM)),
        "c_ctx": nrm(ks[9], (D_MODEL,)),
        "w_ada": nrm(ks[10], (DEPTH, D_MODEL, N_MOD * D_MODEL), 0.5 * D_MODEL ** -0.5),
        "b_ada": nrm(ks[11], (DEPTH, N_MOD * D_MODEL), 0.01),
        "norm_mix_g": 1.0 + nrm(ks[12], (DEPTH, D_MODEL), 0.01),
        "w_in": nrm(ks[13], (DEPTH, D_MODEL, D_IN), D_MODEL ** -0.5),
        "ret_decay": ret_logit + nrm(ks[14], (DEPTH, 2, RET_HEADS), 0.1),
        "ret_gn_g": 1.0 + nrm(ks[15], (DEPTH, GROUP_W), 0.01),
        "lru_conv_w": nrm(ks[16], (DEPTH, LRU_CONV_W, LRU_WIDTH), LRU_CONV_W ** -0.5),
        "lru_conv_b": nrm(ks[17], (DEPTH, LRU_WIDTH), 0.01),
        "lru_w_a": nrm(ks[18], (DEPTH, 2, LRU_BLOCKS, LRU_BLOCK_W, LRU_BLOCK_W), LRU_BLOCK_W ** -0.5),
        "lru_b_a": nrm(ks[19], (DEPTH, 2, LRU_WIDTH), 0.01),
        "lru_w_x": nrm(ks[21], (DEPTH, 2, LRU_BLOCKS, LRU_BLOCK_W, LRU_BLOCK_W), LRU_BLOCK_W ** -0.5),
        "lru_b_x": nrm(ks[22], (DEPTH, 2, LRU_WIDTH), 0.01),
        "lru_lambda": jnp.log(a0) - jnp.log1p(-a0),
        "swa_sink": nrm(ks[23], (DEPTH, SWA_HEADS), 0.5),
        "diff_lambda": nrm(ks[24], (DEPTH, 4, DIFF_QK_DIM), 0.1),
        "diff_norm_g": 1.0 + nrm(ks[25], (DEPTH, HEAD_DIM), 0.01),
        "w_out": nrm(ks[26], (DEPTH, D_MODEL, D_MODEL), D_MODEL ** -0.5),
        "norm_mlp_g": 1.0 + nrm(ks[27], (DEPTH, D_MODEL), 0.01),
        "w_ff1": nrm(ks[28], (DEPTH, D_MODEL, D_FF), D_MODEL ** -0.5),
        "w_ff2": nrm(ks[29], (DEPTH, D_FF, D_MODEL), D_FF ** -0.5),
        "final_norm_g": 1.0 + nrm(ks[30], (D_MODEL,), 0.01),
    }


def reference(x_prompt, x_sample, c, state_ret, state_lru, cache_swa_k, cache_swa_v, cache_diff_k, cache_diff_v,
              c_ctx, w_ada, b_ada, norm_mix_g, w_in, ret_decay, ret_gn_g, lru_conv_w, lru_conv_b,
              lru_w_a, lru_b_a, lru_w_x, lru_b_x, lru_lambda, swa_sink, diff_lambda, diff_norm_g,
              w_out, norm_mlp_g, w_ff1, w_ff2, final_norm_g):
    stacked = dict(w_ada=w_ada, b_ada=b_ada, norm_mix_g=norm_mix_g, w_in=w_in, ret_decay=ret_decay,
                   ret_gn_g=ret_gn_g, lru_conv_w=lru_conv_w, lru_conv_b=lru_conv_b, lru_w_a=lru_w_a,
                   lru_b_a=lru_b_a, lru_w_x=lru_w_x, lru_b_x=lru_b_x, lru_lambda=lru_lambda,
                   swa_sink=swa_sink, diff_lambda=diff_lambda, diff_norm_g=diff_norm_g, w_out=w_out,
                   norm_mlp_g=norm_mlp_g, w_ff1=w_ff1, w_ff2=w_ff2)

    xp = x_prompt
    ctx_list = []
    for l in range(DEPTH):
        p = {name: arr[l] for name, arr in stacked.items()}
        xp, ctx_l = trunk_layer(xp, c_ctx, l, p, None)
        ctx_list.append(ctx_l)
    y_prompt = rmsnorm(xp, final_norm_g)
    new_state_ret = jnp.stack([t[0] for t in ctx_list], axis=1)
    new_state_lru = jnp.stack([t[1] for t in ctx_list], axis=1)
    new_cache_swa_k = jnp.stack([t[2] for t in ctx_list], axis=1)
    new_cache_swa_v = jnp.stack([t[3] for t in ctx_list], axis=1)
    new_cache_diff_k = jnp.stack([t[4] for t in ctx_list], axis=1)
    new_cache_diff_v = jnp.stack([t[5] for t in ctx_list], axis=1)

    xs = x_sample
    for l in range(DEPTH):
        p = {name: arr[l] for name, arr in stacked.items()}
        ctx_l = (state_ret[:, l], state_lru[:, l], cache_swa_k[:, l], cache_swa_v[:, l],
                 cache_diff_k[:, l], cache_diff_v[:, l])
        xs, _ = trunk_layer(xs, c, l, p, ctx_l)
    y_sample = rmsnorm(xs, final_norm_g)

    return (y_prompt, y_sample, new_state_ret, new_state_lru, new_cache_swa_k, new_cache_swa_v,
            new_cache_diff_k, new_cache_diff_v)
```

```python
import functools
import math

import jax
import jax.numpy as jnp
from jax import lax
from jax.experimental import pallas as pl
from jax.experimental.pallas import tpu as pltpu

F32 = jnp.float32
BF16 = jnp.bfloat16

D_MODEL = 1024
DEPTH = 4
GRID_W = 64
HEAD_DIM = 64
GROUP_W = D_MODEL // 4
RET_CHUNK = 256
LRU_BLOCKS = 4
LRU_C = 8.0
SWA_WINDOW = 128
DIFF_QK_DIM = HEAD_DIM // 2
D_FF = 4 * D_MODEL
ROPE_BASE = 10000.0
EPS = 1e-6
N_MOD = 6
D_IN = 11 * GROUP_W
MOD_ROWS = 16

LANES = 128
SUBLANES = 8
VMEM_LIMIT = 56 * 1024 * 1024
NEG_BIG = -1e30

ROW_TILE = 512
ATTN_Q_TILE = 256


def _dot(a, b):
    return jnp.dot(a, b, preferred_element_type=F32)


def _dot_nt(a, b):
    return lax.dot_general(a, b, (((1,), (1,)), ((), ())), preferred_element_type=F32)


def _dot_tn(a, b):
    return lax.dot_general(a, b, (((0,), (0,)), ((), ())), preferred_element_type=F32)


def _rms(x):
    return x * lax.rsqrt(jnp.mean(x * x, axis=-1, keepdims=True) + EPS)


def _lane(shape):
    return lax.broadcasted_iota(jnp.int32, shape, len(shape) - 1)


def _mod_kernel(c_ref, w_ref, b_ref, o_ref):
    s = jax.nn.silu(c_ref[...]).astype(BF16)
    o_ref[...] = _dot(s, w_ref[...].astype(BF16)) + b_ref[...]


def _modulation(cvec, w_ada, b_ada):
    tn = 1536
    n_mod = N_MOD * D_MODEL
    return pl.pallas_call(
        _mod_kernel,
        grid=(DEPTH, n_mod // tn),
        in_specs=[
            pl.BlockSpec((MOD_ROWS, D_MODEL), lambda l, j: (0, 0)),
            pl.BlockSpec((None, D_MODEL, tn), lambda l, j: (l, 0, j)),
            pl.BlockSpec((None, 1, tn), lambda l, j: (l, 0, j)),
        ],
        out_specs=pl.BlockSpec((None, MOD_ROWS, tn), lambda l, j: (l, 0, j)),
        out_shape=jax.ShapeDtypeStruct((DEPTH, MOD_ROWS, n_mod), F32),
        compiler_params=pltpu.CompilerParams(
            dimension_semantics=("arbitrary", "arbitrary"), vmem_limit_bytes=VMEM_LIMIT),
        name="modulation",
    )(cvec, w_ada, b_ada.reshape(DEPTH, 1, n_mod))


def _rope_slab(x, c, s, half):
    first = (_lane(x.shape) & (2 * half - 1)) < half
    swapped = jnp.where(first, pltpu.roll(x, LANES - half, axis=1), pltpu.roll(x, half, axis=1))
    return x * c + swapped * s


def _in_proj_kernel(*refs, rope):
    if rope:
        x_ref, mod_ref, g_ref, w_ref, c64_ref, s64_ref, c32_ref, s32_ref, zr_ref, zl_ref, zs_ref, zd_ref = refs
    else:
        x_ref, mod_ref, g_ref, w_ref, zr_ref, zl_ref, zs_ref, zd_ref = refs
    sh1 = mod_ref[:, 0:D_MODEL]
    sc1 = mod_ref[:, D_MODEL:2 * D_MODEL]
    h = (_rms(x_ref[...]) * g_ref[...] * (1.0 + sc1) + sh1).astype(BF16)
    zr_ref[...] = _dot(h, w_ref[:, 0:4 * GROUP_W])
    zl_ref[...] = _dot(h, w_ref[:, 4 * GROUP_W:6 * GROUP_W])
    zs = _dot(h, w_ref[:, 6 * GROUP_W:8 * GROUP_W])
    zd = _dot(h, w_ref[:, 8 * GROUP_W:11 * GROUP_W])
    if rope:
        c64, s64, c32, s32 = c64_ref[...], s64_ref[...], c32_ref[...], s32_ref[...]
        for i in range(3):
            sl = slice(i * LANES, (i + 1) * LANES)
            zs_ref[:, sl] = _rope_slab(zs[:, sl], c64, s64, HEAD_DIM // 2)
        zs_ref[:, 3 * LANES:4 * LANES] = zs[:, 3 * LANES:4 * LANES]
        for i in range(4):
            sl = slice(i * LANES, (i + 1) * LANES)
            zd_ref[:, sl] = _rope_slab(zd[:, sl], c32, s32, DIFF_QK_DIM // 2)
        zd_ref[:, 4 * LANES:6 * LANES] = zd[:, 4 * LANES:6 * LANES]
    else:
        zs_ref[...] = zs
        zd_ref[...] = zd


def _in_proj(x, mod4, norm_g, w_in_b, l, seq_len, rope_tabs):
    t = x.shape[0]
    tm = ROW_TILE
    rope = rope_tabs is not None
    if rope:
        tiles_per_seq = seq_len // tm
        mod_map = lambda i: (l, 1 + i // tiles_per_seq, 0, 0)
    else:
        mod_map = lambda i: (l, 0, 0, 0)
    in_specs = [
        pl.BlockSpec((tm, D_MODEL), lambda i: (i, 0)),
        pl.BlockSpec((None, None, 1, N_MOD * D_MODEL), mod_map),
        pl.BlockSpec((None, 1, D_MODEL), lambda i: (l, 0, 0)),
        pl.BlockSpec((None, D_MODEL, D_IN), lambda i: (l, 0, 0)),
    ]
    args = [x, mod4, norm_g, w_in_b]
    if rope:
        tab_spec = pl.BlockSpec((tm, LANES), lambda i: (i % tiles_per_seq, 0))
        in_specs += [tab_spec] * 4
        args += list(rope_tabs)
    widths = (4 * GROUP_W, 2 * GROUP_W, 2 * GROUP_W, 3 * GROUP_W)
    return pl.pallas_call(
        functools.partial(_in_proj_kernel, rope=rope),
        grid=(t // tm,),
        in_specs=in_specs,
        out_specs=[pl.BlockSpec((tm, w), lambda i: (i, 0)) for w in widths],
        out_shape=[jax.ShapeDtypeStruct((t, w), F32) for w in widths],
        compiler_params=pltpu.CompilerParams(
            dimension_semantics=("arbitrary",), vmem_limit_bytes=VMEM_LIMIT),
        name="in_proj_rope" if rope else "in_proj",
    )(*args)


def _scan_levels(a_slabs, u_slabs, pa_ref, pb_ref, sa_ref, sb_ref, h0, n0, reverse):
    order = list(range(SUBLANES - 1, -1, -1)) if reverse else list(range(SUBLANES))
    acc_a = acc_b = None
    for r in order:
        if acc_a is None:
            acc_a, acc_b = a_slabs[r], u_slabs[r]
        else:
            acc_b = a_slabs[r] * acc_b + u_slabs[r]
            acc_a = a_slabs[r] * acc_a
        pa_ref[r] = acc_a
        pb_ref[r] = acc_b
    pad = n0 // 2
    ident = pad + n0 if reverse else 0
    sa_ref[ident:ident + pad, :] = jnp.ones((pad, GROUP_W), F32)
    sb_ref[ident:ident + pad, :] = jnp.zeros((pad, GROUP_W), F32)
    sign = 1 if reverse else -1
    k = 1
    while k < n0:
        sa_ref[pad:pad + n0, :] = acc_a
        sb_ref[pad:pad + n0, :] = acc_b
        off = pad + sign * k
        acc_b = acc_a * sb_ref[off:off + n0, :] + acc_b
        acc_a = acc_a * sa_ref[off:off + n0, :]
        k *= 2
    sa_ref[pad:pad + n0, :] = acc_a
    sb_ref[pad:pad + n0, :] = acc_b
    off = pad + sign
    carry_in = sa_ref[off:off + n0, :] * h0 + sb_ref[off:off + n0, :]
    end = 0 if reverse else n0 - 1
    final = acc_a[end:end + 1, :] * h0 + acc_b[end:end + 1, :]
    h_slabs = [pa_ref[r] * carry_in + pb_ref[r] for r in range(SUBLANES)]
    return h_slabs, final


def _recurrent_kernel(*refs, seq_len, has_state):
    if has_state:
        (scal_ref, zr_ref, zl_ref, s0_ref, h0_ref, lgl_ref, gng_ref, cw_ref, cb_ref, wbd_ref, lb_ref, lam_ref,
         yr_ref, yl_ref,
         d_ref, vec_ref, pad_ref, xc_ref, a_ref, u_ref, pa_ref, pb_ref, sa_ref, sb_ref) = refs
    else:
        (scal_ref, zr_ref, zl_ref, lgl_ref, gng_ref, cw_ref, cb_ref, wbd_ref, lb_ref, lam_ref,
         yr_ref, yl_ref, sret_ref, slru_ref,
         d_ref, vec_ref, pad_ref, xc_ref, a_ref, u_ref, pa_ref, pb_ref, sa_ref, sb_ref) = refs
    ch = RET_CHUNK
    n_chunks = seq_len // ch
    n0 = seq_len // SUBLANES
    heads = GROUP_W // HEAD_DIM

    @pl.when(pl.program_id(0) == 0)
    def _():
        n = lax.broadcasted_iota(jnp.int32, (ch, ch), 0)
        m = lax.broadcasted_iota(jnp.int32, (ch, ch), 1)
        dist = (n - m).astype(F32)
        for h in range(heads):
            lg = jnp.where(dist > 0, scal_ref[h], scal_ref[heads + h])
            d_ref[h] = jnp.where(dist == 0, 2.0, jnp.exp(jnp.abs(dist) * lg))
        t = lax.broadcasted_iota(jnp.int32, (ch, GROUP_W), 0).astype(F32)
        lgf = lgl_ref[0:1, :]
        lgb = lgl_ref[1:2, :]
        vec_ref[0] = jnp.exp((t + 1.0) * lgf)
        vec_ref[1] = jnp.exp((ch - t) * lgb)
        vec_ref[2] = jnp.exp((ch - 1.0 - t) * lgf)
        vec_ref[3] = jnp.exp(t * lgb)

    low = _lane((1, LANES)) < HEAD_DIM
    rr = lax.broadcasted_iota(jnp.int32, (LANES, LANES), 0) < HEAD_DIM
    cc = lax.broadcasted_iota(jnp.int32, (LANES, LANES), 1) < HEAD_DIM
    block_diag = (rr == cc).astype(F32)
    g_chunk_f = jnp.exp(ch * lgl_ref[0:1, :])
    g_chunk_b = jnp.exp(ch * lgl_ref[1:2, :])

    def kv(c, p):
        rows = slice(c * ch, (c + 1) * ch)
        k = zr_ref[rows, GROUP_W + p * LANES:GROUP_W + (p + 1) * LANES] * (HEAD_DIM ** -0.5)
        v = zr_ref[rows, 2 * GROUP_W + p * LANES:2 * GROUP_W + (p + 1) * LANES].astype(BF16)
        return k, v

    def state_update(k, v, zeta):
        return _dot_tn((k * zeta).astype(BF16), v) * block_diag

    for p in range(2):
        pl_sl = slice(p * LANES, (p + 1) * LANES)
        if has_state:
            s_b = [None] * n_chunks
            s_b[n_chunks - 1] = s0_ref[1, p]
            for c in range(n_chunks - 1, 0, -1):
                k, v = kv(c, p)
                s_b[c - 1] = s_b[c] * g_chunk_b[:, pl_sl] + state_update(k, v, vec_ref[3, :, pl_sl])
            s_f = s0_ref[0, p]
        for c in range(n_chunks):
            rows = slice(c * ch, (c + 1) * ch)
            q = zr_ref[rows, pl_sl]
            k, v = kv(c, p)
            kb = k.astype(BF16)
            outs = []
            for j in range(2):
                qm = jnp.where(low if j == 0 else ~low, q, 0.0).astype(BF16)
                w = (_dot_nt(qm, kb) * d_ref[2 * p + j]).astype(BF16)
                outs.append(_dot(w, v))
            o = jnp.where(low, outs[0], outs[1])
            if has_state:
                qb = q.astype(BF16)
                o = o + _dot(qb, s_f.astype(BF16)) * vec_ref[0, :, pl_sl]
                o = o + _dot(qb, s_b[c].astype(BF16)) * vec_ref[1, :, pl_sl]
                if c + 1 < n_chunks:
                    s_f = s_f * g_chunk_f[:, pl_sl] + state_update(k, v, vec_ref[2, :, pl_sl])
            else:
                sret_ref[0, p] = state_update(k, v, vec_ref[2, :, pl_sl])
                sret_ref[1, p] = state_update(k, v, vec_ref[3, :, pl_sl])
            inv = 1.0 / HEAD_DIM
            mu = jnp.where(low, jnp.sum(jnp.where(low, o, 0.0), -1, keepdims=True),
                           jnp.sum(jnp.where(low, 0.0, o), -1, keepdims=True)) * inv
            dlt = o - mu
            sq = dlt * dlt
            var = jnp.where(low, jnp.sum(jnp.where(low, sq, 0.0), -1, keepdims=True),
                            jnp.sum(jnp.where(low, 0.0, sq), -1, keepdims=True)) * inv
            gate = zr_ref[rows, 3 * GROUP_W + p * LANES:3 * GROUP_W + (p + 1) * LANES]
            y = dlt * lax.rsqrt(var + EPS) * gng_ref[:, pl_sl] * jax.nn.silu(gate)
            yr_ref[rows, pl_sl] = y.astype(BF16)

    lw = 2 * GROUP_W
    xs = [zl_ref[:, r * lw:r * lw + GROUP_W] for r in range(SUBLANES)]
    zero8 = jnp.zeros((SUBLANES, GROUP_W), F32)
    for i, r in enumerate((SUBLANES - 2, SUBLANES - 1, 0)):
        pad_ref[i, 0:SUBLANES, :] = zero8
        pad_ref[i, SUBLANES:SUBLANES + n0, :] = xs[r]
        pad_ref[i, SUBLANES + n0:2 * SUBLANES + n0, :] = zero8
    prev_blk = {SUBLANES - 2: pad_ref[0, SUBLANES - 1:SUBLANES - 1 + n0, :],
                SUBLANES - 1: pad_ref[1, SUBLANES - 1:SUBLANES - 1 + n0, :]}
    next_blk0 = pad_ref[2, SUBLANES + 1:SUBLANES + 1 + n0, :]

    def tap(r, off):
        rr_ = r + off
        if rr_ < 0:
            return prev_blk[rr_ + SUBLANES]
        if rr_ >= SUBLANES:
            return next_blk0
        return xs[rr_]

    cw = cw_ref[...]
    for r in range(SUBLANES):
        xc_r = (cw[0:1] * tap(r, -2) + cw[1:2] * tap(r, -1) + cw[2:3] * xs[r] + cw[3:4] * tap(r, 1)
                + cb_ref[...])
        xc_ref[r * n0:(r + 1) * n0, :] = xc_r
    xc = xc_ref[...]
    pre = _dot(xc.astype(BF16), wbd_ref[...]) + lb_ref[...]
    finals = []
    h_dirs = []
    for d in range(2):
        r_gate = jax.nn.sigmoid(pre[:, (2 * d) * GROUP_W:(2 * d + 1) * GROUP_W])
        i_gate = jax.nn.sigmoid(pre[:, (2 * d + 1) * GROUP_W:(2 * d + 2) * GROUP_W])
        log_a = -LRU_C * r_gate * jax.nn.softplus(-lam_ref[d:d + 1, :])
        a_val = jnp.exp(log_a)
        a_ref[d] = a_val
        u_ref[d] = jnp.sqrt(-jnp.tanh(log_a) * (a_val * a_val + 1.0)) * (i_gate * xc)
    for d in range(2):
        a_slabs = [a_ref[d, r * n0:(r + 1) * n0, :] for r in range(SUBLANES)]
        u_slabs = [u_ref[d, r * n0:(r + 1) * n0, :] for r in range(SUBLANES)]
        h0 = h0_ref[d:d + 1, :] if has_state else jnp.zeros((1, GROUP_W), F32)
        h_slabs, fin = _scan_levels(a_slabs, u_slabs, pa_ref, pb_ref, sa_ref, sb_ref, h0, n0, reverse=(d == 1))
        finals.append(fin)
        if d == 0:
            for r in range(SUBLANES):
                a_ref[0, r * n0:(r + 1) * n0, :] = h_slabs[r]
        else:
            h_dirs = h_slabs
    for r in range(SUBLANES):
        gate = zl_ref[:, r * lw + GROUP_W:(r + 1) * lw]
        y = (a_ref[0, r * n0:(r + 1) * n0, :] + h_dirs[r]) * jax.nn.gelu(gate)
        yl_ref[:, r * GROUP_W:(r + 1) * GROUP_W] = y.astype(BF16)
    if not has_state:
        slru_ref[0:1, :] = finals[0]
        slru_ref[1:2, :] = finals[1]


def _recurrent(scal, zr, zl, lp, seq_len, state=None):
    t = zr.shape[0]
    b = t // seq_len
    n0 = seq_len // SUBLANES
    has_state = state is not None
    assert has_state or seq_len == RET_CHUNK
    zr3 = zr.reshape(b, seq_len, 4 * GROUP_W)
    zl3 = zl.reshape(b, n0, SUBLANES * 2 * GROUP_W)
    full = lambda a: pl.BlockSpec(a.shape, lambda i: (0,) * a.ndim)
    in_specs = [
        pl.BlockSpec(memory_space=pltpu.SMEM),
        pl.BlockSpec((None, seq_len, 4 * GROUP_W), lambda i: (i, 0, 0)),
        pl.BlockSpec((None, n0, SUBLANES * 2 * GROUP_W), lambda i: (i, 0, 0)),
    ]
    args = [scal, zr3, zl3]
    if has_state:
        s0, h0 = state
        in_specs += [pl.BlockSpec((None, 2, 2, LANES, LANES), lambda i: (i, 0, 0, 0, 0)),
                     pl.BlockSpec((None, 2, GROUP_W), lambda i: (i, 0, 0))]
        args += [s0, h0]
    params = [lp["lg_lane"], lp["gn_g"], lp["conv_w"], lp["conv_b"], lp["lru_wbd"], lp["lru_bias"], lp["lru_lam"]]
    in_specs += [full(a) for a in params]
    args += params
    out_specs = [pl.BlockSpec((None, seq_len, GROUP_W), lambda i: (i, 0, 0)),
                 pl.BlockSpec((None, n0, SUBLANES * GROUP_W), lambda i: (i, 0, 0))]
    out_shape = [jax.ShapeDtypeStruct((b, seq_len, GROUP_W), BF16),
                 jax.ShapeDtypeStruct((b, n0, SUBLANES * GROUP_W), BF16)]
    if not has_state:
        out_specs += [pl.BlockSpec((None, 2, 2, LANES, LANES), lambda i: (i, 0, 0, 0, 0)),
                      pl.BlockSpec((None, 2, GROUP_W), lambda i: (i, 0, 0))]
        out_shape += [jax.ShapeDtypeStruct((b, 2, 2, LANES, LANES), F32),
                      jax.ShapeDtypeStruct((b, 2, GROUP_W), F32)]
    scratch = [
        pltpu.VMEM((GROUP_W // HEAD_DIM, RET_CHUNK, RET_CHUNK), F32),
        pltpu.VMEM((4, RET_CHUNK, GROUP_W), F32),
        pltpu.VMEM((3, n0 + 2 * SUBLANES, GROUP_W), F32),
        pltpu.VMEM((seq_len, GROUP_W), F32),
        pltpu.VMEM((2, seq_len, GROUP_W), F32),
        pltpu.VMEM((2, seq_len, GROUP_W), F32),
        pltpu.VMEM((SUBLANES, n0, GROUP_W), F32),
        pltpu.VMEM((SUBLANES, n0, GROUP_W), F32),
        pltpu.VMEM((2 * n0, GROUP_W), F32),
        pltpu.VMEM((2 * n0, GROUP_W), F32),
    ]
    outs = pl.pallas_call(
        functools.partial(_recurrent_kernel, seq_len=seq_len, has_state=has_state),
        grid=(b,),
        in_specs=in_specs,
        out_specs=out_specs,
        out_shape=out_shape,
        scratch_shapes=scratch,
        compiler_params=pltpu.CompilerParams(
            dimension_semantics=("arbitrary",), vmem_limit_bytes=VMEM_LIMIT),
        name="recurrent_latent" if has_state else "recurrent_ctx",
    )(*args)
    y_ret = outs[0].reshape(t, GROUP_W)
    y_lru = outs[1].reshape(t, GROUP_W)
    return (y_ret, y_lru) + tuple(outs[2:])


def _dup_half(x, half):
    sw = pltpu.roll(x, HEAD_DIM, axis=1)
    low = _lane(x.shape) < HEAD_DIM
    return jnp.where(low, x, sw) if half == 0 else jnp.where(low, sw, x)


def _attention_kernel(*refs, seq_len, has_ctx):
    if has_ctx:
        (scal_ref, sq_ref, skv_ref, dq_ref, dkv_ref, cks_ref, cvs_ref, ckd_ref, cvd_ref, ng_ref, y_ref) = refs
    else:
        (scal_ref, sq_ref, skv_ref, dq_ref, dkv_ref, ng_ref, y_ref) = refs
    tq = ATTN_Q_TILE
    heads = GROUP_W // HEAD_DIM
    lane = _lane((1, LANES))
    low = lane < HEAD_DIM

    if has_ctx:
        win = 2 * tq
        i = pl.program_id(1)
        blk = jnp.clip(2 * i - 1, 0, (seq_len - win) // SWA_WINDOW)
        start = pl.multiple_of(blk * SWA_WINDOW, SWA_WINDOW)
        k_loc = skv_ref[pl.ds(start, win), 2 * LANES:3 * LANES]
        v_loc = skv_ref[pl.ds(start, win), 3 * LANES:4 * LANES]
        qpos = i * tq + lax.broadcasted_iota(jnp.int32, (tq, win), 0)
        kpos = start + lax.broadcasted_iota(jnp.int32, (tq, win), 1)
        valid = jnp.abs(kpos - qpos) <= SWA_WINDOW
    else:
        k_loc = skv_ref[:, 2 * LANES:3 * LANES]
        v_loc = skv_ref[:, 3 * LANES:4 * LANES]
    scale = HEAD_DIM ** -0.5
    for p in range(2):
        kd = _dup_half(k_loc, p).astype(BF16)
        vd = _dup_half(v_loc, p).astype(BF16)
        if has_ctx:
            ckd = _dup_half(cks_ref[...], p).astype(BF16)
            cvd = _dup_half(cvs_ref[...], p).astype(BF16)
        q = sq_ref[:, p * LANES:(p + 1) * LANES]
        outs = []
        for j in range(2):
            sink = scal_ref[2 * heads + 2 * p + j]
            qm = jnp.where(low if j == 0 else ~low, q, 0.0).astype(BF16)
            s = _dot_nt(qm, kd) * scale
            if has_ctx:
                s = jnp.where(valid, s, NEG_BIG)
                sc = _dot_nt(qm, ckd) * scale
                m = jnp.maximum(jnp.maximum(jnp.max(s, -1, keepdims=True), jnp.max(sc, -1, keepdims=True)), sink)
                ec = jnp.exp(sc - m)
            else:
                m = jnp.maximum(jnp.max(s, -1, keepdims=True), sink)
            e = jnp.exp(s - m)
            den = jnp.sum(e, -1, keepdims=True) + jnp.exp(sink - m)
            o = _dot(e.astype(BF16), vd)
            if has_ctx:
                den = den + jnp.sum(ec, -1, keepdims=True)
                o = o + _dot(ec.astype(BF16), cvd)
            outs.append(o * (1.0 / den))
        y_ref[:, p * LANES:(p + 1) * LANES] = jnp.where(low, outs[0], outs[1]).astype(BF16)

    lam = scal_ref[3 * heads]
    coef = scal_ref[3 * heads + 1]
    dscale = DIFF_QK_DIM ** -0.5
    grp = lane >> 5
    for p in range(2):
        q = dq_ref[:, p * LANES:(p + 1) * LANES]
        kb = dkv_ref[:, GROUP_W + p * LANES:GROUP_W + (p + 1) * LANES].astype(BF16)
        vb = dkv_ref[:, 2 * GROUP_W + p * LANES:2 * GROUP_W + (p + 1) * LANES].astype(BF16)
        if has_ctx:
            ckb = ckd_ref[:, p * LANES:(p + 1) * LANES].astype(BF16)
            cvb = cvd_ref[:, p * LANES:(p + 1) * LANES].astype(BF16)
        outs = []
        for j in range(2):
            es, ecs, invs = [], [], []
            for c in range(2):
                qm = jnp.where(grp == 2 * j + c, q, 0.0).astype(BF16)
                s = _dot_nt(qm, kb) * dscale
                m = jnp.max(s, -1, keepdims=True)
                if has_ctx:
                    sc = _dot_nt(qm, ckb) * dscale
                    m = jnp.maximum(m, jnp.max(sc, -1, keepdims=True))
                e = jnp.exp(s - m)
                den = jnp.sum(e, -1, keepdims=True)
                if has_ctx:
                    ec = jnp.exp(sc - m)
                    den = den + jnp.sum(ec, -1, keepdims=True)
                    ecs.append(ec)
                es.append(e)
                invs.append(1.0 / den)
            w1 = lam * invs[1]
            o = _dot((es[0] * invs[0] - es[1] * w1).astype(BF16), vb)
            if has_ctx:
                o = o + _dot((ecs[0] * invs[0] - ecs[1] * w1).astype(BF16), cvb)
            outs.append(o)
        o = jnp.where(low, outs[0], outs[1])
        sq = o * o
        ms = jnp.where(low, jnp.sum(jnp.where(low, sq, 0.0), -1, keepdims=True),
                       jnp.sum(jnp.where(low, 0.0, sq), -1, keepdims=True)) * (1.0 / HEAD_DIM)
        y = o * lax.rsqrt(ms + EPS) * ng_ref[:, p * LANES:(p + 1) * LANES] * coef
        y_ref[:, GROUP_W + p * LANES:GROUP_W + (p + 1) * LANES] = y.astype(BF16)


def _attention(scal, zs, zd, norm_g_tiled, seq_len, caches=None):
    t = zs.shape[0]
    b = t // seq_len
    tq = ATTN_Q_TILE
    nq = seq_len // tq
    has_ctx = caches is not None
    zs3 = zs.reshape(b, seq_len, 2 * GROUP_W)
    zd3 = zd.reshape(b, seq_len, 3 * GROUP_W)
    in_specs = [
        pl.BlockSpec(memory_space=pltpu.SMEM),
        pl.BlockSpec((None, tq, 2 * GROUP_W), lambda i, j: (i, j, 0)),
        pl.BlockSpec((None, seq_len, 2 * GROUP_W), lambda i, j: (i, 0, 0)),
        pl.BlockSpec((None, tq, 3 * GROUP_W), lambda i, j: (i, j, 0)),
        pl.BlockSpec((None, seq_len, 3 * GROUP_W), lambda i, j: (i, 0, 0)),
    ]
    args = [scal, zs3, zs3, zd3, zd3]
    if has_ctx:
        for a in caches:
            in_specs.append(pl.BlockSpec((None,) + a.shape[1:], lambda i, j: (i, 0, 0)))
            args.append(a)
    in_specs.append(pl.BlockSpec(norm_g_tiled.shape, lambda i, j: (0, 0)))
    args.append(norm_g_tiled)
    y = pl.pallas_call(
        functools.partial(_attention_kernel, seq_len=seq_len, has_ctx=has_ctx),
        grid=(b, nq),
        in_specs=in_specs,
        out_specs=pl.BlockSpec((None, tq, 2 * GROUP_W), lambda i, j: (i, j, 0)),
        out_shape=jax.ShapeDtypeStruct((b, seq_len, 2 * GROUP_W), BF16),
        compiler_params=pltpu.CompilerParams(
            dimension_semantics=("arbitrary", "arbitrary"), vmem_limit_bytes=VMEM_LIMIT),
        name="attention_latent" if has_ctx else "attention_ctx",
    )(*args)
    return y.reshape(t, 2 * GROUP_W)


def _out_mlp_kernel(x_ref, yr_ref, yl_ref, ya_ref, mod_ref, wo_ref, g_ref, w1_ref, w2_ref, gf_ref, o_ref, *, final):
    g1 = mod_ref[:, 2 * D_MODEL:3 * D_MODEL]
    sh2 = mod_ref[:, 3 * D_MODEL:4 * D_MODEL]
    sc2 = mod_ref[:, 4 * D_MODEL:5 * D_MODEL]
    g2 = mod_ref[:, 5 * D_MODEL:6 * D_MODEL]
    y = (_dot(yr_ref[...], wo_ref[0:GROUP_W, :]) + _dot(yl_ref[...], wo_ref[GROUP_W:2 * GROUP_W, :])
         + _dot(ya_ref[...], wo_ref[2 * GROUP_W:4 * GROUP_W, :]))
    x = x_ref[...] + g1 * y
    h = (_rms(x) * g_ref[...] * (1.0 + sc2) + sh2).astype(BF16)
    acc = None
    for c in range(D_FF // D_MODEL):
        sl = slice(c * D_MODEL, (c + 1) * D_MODEL)
        hid = jnp.square(jnp.maximum(_dot(h, w1_ref[:, sl]), 0.0)).astype(BF16)
        part = _dot(hid, w2_ref[sl, :])
        acc = part if acc is None else acc + part
    x = x + g2 * acc
    if final:
        x = _rms(x) * gf_ref[...]
    o_ref[...] = x


def _out_mlp(x, y_ret, y_lru, y_att, mod4, w_out_b, norm_g, w_ff1_b, w_ff2_b, final_g, l, seq_len, is_latent, final):
    t = x.shape[0]
    tm = ROW_TILE
    if is_latent:
        tiles_per_seq = seq_len // tm
        mod_map = lambda i: (l, 1 + i // tiles_per_seq, 0, 0)
    else:
        mod_map = lambda i: (l, 0, 0, 0)
    once = pl.Buffered(1)
    in_specs = [
        pl.BlockSpec((tm, D_MODEL), lambda i: (i, 0)),
        pl.BlockSpec((tm, GROUP_W), lambda i: (i, 0)),
        pl.BlockSpec((tm, GROUP_W), lambda i: (i, 0)),
        pl.BlockSpec((tm, 2 * GROUP_W), lambda i: (i, 0)),
        pl.BlockSpec((None, None, 1, N_MOD * D_MODEL), mod_map),
        pl.BlockSpec((None, D_MODEL, D_MODEL), lambda i: (l, 0, 0), pipeline_mode=once),
        pl.BlockSpec((None, 1, D_MODEL), lambda i: (l, 0, 0)),
        pl.BlockSpec((None, D_MODEL, D_FF), lambda i: (l, 0, 0), pipeline_mode=once),
        pl.BlockSpec((None, D_FF, D_MODEL), lambda i: (l, 0, 0), pipeline_mode=once),
        pl.BlockSpec((1, D_MODEL), lambda i: (0, 0)),
    ]
    return pl.pallas_call(
        functools.partial(_out_mlp_kernel, final=final),
        grid=(t // tm,),
        in_specs=in_specs,
        out_specs=pl.BlockSpec((tm, D_MODEL), lambda i: (i, 0)),
        out_shape=jax.ShapeDtypeStruct((t, D_MODEL), F32),
        compiler_params=pltpu.CompilerParams(
            dimension_semantics=("arbitrary",), vmem_limit_bytes=VMEM_LIMIT),
        name="out_mlp_final" if final else "out_mlp",
    )(x, y_ret, y_lru, y_att, mod4, w_out_b, norm_g, w_ff1_b, w_ff2_b, final_g)


def _rope_tables(length):
    row = (jnp.arange(length) // GRID_W).astype(F32)
    col = (jnp.arange(length) % GRID_W).astype(F32)
    tabs = []
    for dim in (HEAD_DIM, DIFF_QK_DIM):
        n = dim // 4
        inv = ROPE_BASE ** (-jnp.arange(n, dtype=F32) / n)
        ang = jnp.concatenate([row[:, None] * inv, col[:, None] * inv], -1)
        cos, sin = jnp.cos(ang), jnp.sin(ang)
        reps = LANES // dim
        tabs.append(jnp.tile(jnp.concatenate([cos, cos], -1), (1, reps)))
        tabs.append(jnp.tile(jnp.concatenate([-sin, sin], -1), (1, reps)))
    return tabs


def _block_diag(w):
    n, c, d = w.shape
    eye = jnp.eye(n, dtype=w.dtype)
    return jnp.einsum("ncd,nm->ncmd", w, eye).reshape(n * c, n * d)


def _pair_states(s):
    b, two, h, d, _ = s.shape
    s = s.reshape(b, two, h // 2, 2, d, d)
    eye = jnp.eye(2, dtype=s.dtype)
    return jnp.einsum("bxpjde,jk->bxpjdke", s, eye).reshape(b, two, h // 2, 2 * d, 2 * d)


def _unpair_states(s):
    b, two, hp, dd, _ = s.shape
    d = dd // 2
    blocks = jnp.stack([s[..., 0:d, 0:d], s[..., d:dd, d:dd]], axis=3)
    return blocks.reshape(b, two, 2 * hp, d, d)


def _layer_params(l, ret_decay, ret_gn_g, lru_conv_w, lru_conv_b, lru_w_a, lru_b_a, lru_w_x, lru_b_x, lru_lambda,
                  swa_sink, diff_lambda, diff_norm_g):
    heads = GROUP_W // HEAD_DIM
    log_gamma = jax.nn.log_sigmoid(ret_decay[l].astype(F32))
    lambda_init = 0.8 - 0.6 * math.exp(-0.3 * l)
    lv = diff_lambda[l].astype(F32)
    lam = jnp.exp(jnp.sum(lv[0] * lv[1])) - jnp.exp(jnp.sum(lv[2] * lv[3])) + lambda_init
    scal = jnp.concatenate([log_gamma[0], log_gamma[1], swa_sink[l].astype(F32),
                            jnp.stack([lam, jnp.asarray(1.0 - lambda_init, F32)]),
                            jnp.zeros((2,), F32)])
    wbd = jnp.concatenate([_block_diag(lru_w_a[l, 0]), _block_diag(lru_w_x[l, 0]),
                           _block_diag(lru_w_a[l, 1]), _block_diag(lru_w_x[l, 1])], axis=1).astype(BF16)
    bias = jnp.concatenate([lru_b_a[l, 0], lru_b_x[l, 0], lru_b_a[l, 1], lru_b_x[l, 1]])[None, :]
    return dict(
        scal=scal,
        lg_lane=jnp.repeat(log_gamma, HEAD_DIM, axis=1),
        gn_g=ret_gn_g[l][None, :],
        conv_w=lru_conv_w[l],
        conv_b=lru_conv_b[l][None, :],
        lru_wbd=wbd,
        lru_bias=bias,
        lru_lam=lru_lambda[l],
        diff_ng=jnp.tile(diff_norm_g[l], heads)[None, :],
    )


def kernel(x_prompt, x_sample, c, state_ret, state_lru, cache_swa_k, cache_swa_v, cache_diff_k, cache_diff_v,
           c_ctx, w_ada, b_ada, norm_mix_g, w_in, ret_decay, ret_gn_g, lru_conv_w, lru_conv_b,
           lru_w_a, lru_b_a, lru_w_x, lru_b_x, lru_lambda, swa_sink, diff_lambda, diff_norm_g,
           w_out, norm_mlp_g, w_ff1, w_ff2, final_norm_g):
    batch, seq, _ = x_prompt.shape
    dec_batch, dec_seq, _ = x_sample.shape
    past = cache_swa_k.shape[2]

    cvec = jnp.concatenate([c_ctx[None, :], c, jnp.zeros((MOD_ROWS - 1 - dec_batch, D_MODEL), F32)], axis=0)
    mod4 = _modulation(cvec, w_ada, b_ada).reshape(DEPTH, MOD_ROWS, 1, N_MOD * D_MODEL)

    w_in_b = w_in.astype(BF16)
    w_out_b = w_out.astype(BF16)
    w_ff1_b = w_ff1.astype(BF16)
    w_ff2_b = w_ff2.astype(BF16)
    norm_mix3 = norm_mix_g.reshape(DEPTH, 1, D_MODEL)
    norm_mlp3 = norm_mlp_g.reshape(DEPTH, 1, D_MODEL)
    final_g = final_norm_g[None, :]
    rope_tabs = _rope_tables(dec_seq)
    layers = [_layer_params(l, ret_decay, ret_gn_g, lru_conv_w, lru_conv_b, lru_w_a, lru_b_a, lru_w_x, lru_b_x,
                            lru_lambda, swa_sink, diff_lambda, diff_norm_g) for l in range(DEPTH)]

    xp = x_prompt.reshape(batch * seq, D_MODEL)
    new_ret, new_lru, new_sk, new_sv, new_dk, new_dv = [], [], [], [], [], []
    for l in range(DEPTH):
        lp = layers[l]
        zr, zl, zs, zd = _in_proj(xp, mod4, norm_mix3, w_in_b, l, seq, None)
        y_ret, y_lru, s_ret, s_lru = _recurrent(lp["scal"], zr, zl, lp, seq)
        y_att = _attention(lp["scal"], zs, zd, lp["diff_ng"], seq)
        xp = _out_mlp(xp, y_ret, y_lru, y_att, mod4, w_out_b, norm_mlp3, w_ff1_b, w_ff2_b, final_g,
                      l, seq, False, l == DEPTH - 1)
        new_ret.append(_unpair_states(s_ret))
        new_lru.append(s_lru)
        new_sk.append(zs[:, 2 * LANES:3 * LANES].reshape(batch, seq, 2, HEAD_DIM))
        new_sv.append(zs[:, 3 * LANES:4 * LANES].reshape(batch, seq, 2, HEAD_DIM))
        new_dk.append(zd[:, GROUP_W:2 * GROUP_W].reshape(batch, seq, GROUP_W // HEAD_DIM, HEAD_DIM))
        new_dv.append(zd[:, 2 * GROUP_W:3 * GROUP_W].reshape(batch, seq, GROUP_W // HEAD_DIM, HEAD_DIM))
    y_prompt = xp.reshape(batch, seq, D_MODEL)

    xs = x_sample.reshape(dec_batch * dec_seq, D_MODEL)
    for l in range(DEPTH):
        lp = layers[l]
        zr, zl, zs, zd = _in_proj(xs, mod4, norm_mix3, w_in_b, l, dec_seq, rope_tabs)
        state = (_pair_states(state_ret[:, l]), state_lru[:, l])
        y_ret, y_lru = _recurrent(lp["scal"], zr, zl, lp, dec_seq, state)
        caches = (cache_swa_k[:, l].reshape(dec_batch, past, 2 * HEAD_DIM),
                  cache_swa_v[:, l].reshape(dec_batch, past, 2 * HEAD_DIM),
                  cache_diff_k[:, l].reshape(dec_batch, past, GROUP_W),
                  cache_diff_v[:, l].reshape(dec_batch, past, GROUP_W))
        y_att = _attention(lp["scal"], zs, zd, lp["diff_ng"], dec_seq, caches)
        xs = _out_mlp(xs, y_ret, y_lru, y_att, mod4, w_out_b, norm_mlp3, w_ff1_b, w_ff2_b, final_g,
                      l, dec_seq, True, l == DEPTH - 1)
    y_sample = xs.reshape(dec_batch, dec_seq, D_MODEL)

    return (y_prompt, y_sample, jnp.stack(new_ret, axis=1), jnp.stack(new_lru, axis=1),
            jnp.stack(new_sk, axis=1), jnp.stack(new_sv, axis=1),
            jnp.stack(new_dk, axis=1), jnp.stack(new_dv, axis=1))
```

```python
import functools
import math

import jax
import jax.numpy as jnp
from jax import lax
from jax.experimental import pallas as pl
from jax.experimental.pallas import tpu as pltpu

F32 = jnp.float32
BF16 = jnp.bfloat16

D_MODEL = 1024
DEPTH = 4
GRID_W = 64
HEAD_DIM = 64
GROUP_W = D_MODEL // 4
RET_CHUNK = 256
LRU_BLOCKS = 4
LRU_C = 8.0
SWA_WINDOW = 128
DIFF_QK_DIM = HEAD_DIM // 2
D_FF = 4 * D_MODEL
ROPE_BASE = 10000.0
EPS = 1e-6
N_MOD = 6
D_IN = 11 * GROUP_W
MOD_ROWS = 16

LANES = 128
SUBLANES = 8
VMEM_LIMIT = 56 * 1024 * 1024
NEG_BIG = -1e30
LOG2E = math.log2(math.e)

ROW_TILE = 512
ATTN_Q_TILE = 256


def _dot(a, b):
    return jnp.dot(a, b, preferred_element_type=F32)


def _dot_nt(a, b):
    return lax.dot_general(a, b, (((1,), (1,)), ((), ())), preferred_element_type=F32)


def _dot_tn(a, b):
    return lax.dot_general(a, b, (((0,), (0,)), ((), ())), preferred_element_type=F32)


def _rms(x):
    return x * lax.rsqrt(jnp.mean(x * x, axis=-1, keepdims=True) + EPS)


def _lane(shape):
    return lax.broadcasted_iota(jnp.int32, shape, len(shape) - 1)


def _mod_kernel(c_ref, w_ref, b_ref, o_ref):
    s = jax.nn.silu(c_ref[...]).astype(BF16)
    o_ref[...] = _dot(s, w_ref[...].astype(BF16)) + b_ref[...]


def _modulation(cvec, w_ada, b_ada):
    tn = 1536
    n_mod = N_MOD * D_MODEL
    return pl.pallas_call(
        _mod_kernel,
        grid=(DEPTH, n_mod // tn),
        in_specs=[
            pl.BlockSpec((MOD_ROWS, D_MODEL), lambda l, j: (0, 0)),
            pl.BlockSpec((None, D_MODEL, tn), lambda l, j: (l, 0, j)),
            pl.BlockSpec((None, 1, tn), lambda l, j: (l, 0, j)),
        ],
        out_specs=pl.BlockSpec((None, MOD_ROWS, tn), lambda l, j: (l, 0, j)),
        out_shape=jax.ShapeDtypeStruct((DEPTH, MOD_ROWS, n_mod), F32),
        compiler_params=pltpu.CompilerParams(
            dimension_semantics=("arbitrary", "arbitrary"), vmem_limit_bytes=VMEM_LIMIT),
        name="modulation",
    )(cvec, w_ada, b_ada.reshape(DEPTH, 1, n_mod))


def _rope_slab(x, c, s, half):
    first = (_lane(x.shape) & (2 * half - 1)) < half
    swapped = jnp.where(first, pltpu.roll(x, LANES - half, axis=1), pltpu.roll(x, half, axis=1))
    return x * c + swapped * s


def _in_proj_kernel(*refs, rope):
    if rope:
        x_ref, mod_ref, g_ref, w_ref, c64_ref, s64_ref, c32_ref, s32_ref, zr_ref, zl_ref, zs_ref, zd_ref = refs
    else:
        x_ref, mod_ref, g_ref, w_ref, zr_ref, zl_ref, zs_ref, zd_ref = refs
    sh1 = mod_ref[:, 0:D_MODEL]
    sc1 = mod_ref[:, D_MODEL:2 * D_MODEL]
    h = (_rms(x_ref[...]) * g_ref[...] * (1.0 + sc1) + sh1).astype(BF16)
    zr_ref[...] = _dot(h, w_ref[:, 0:4 * GROUP_W])
    zl_ref[...] = _dot(h, w_ref[:, 4 * GROUP_W:6 * GROUP_W])
    zs = _dot(h, w_ref[:, 6 * GROUP_W:8 * GROUP_W])
    zd = _dot(h, w_ref[:, 8 * GROUP_W:11 * GROUP_W])
    if rope:
        c64, s64, c32, s32 = c64_ref[...], s64_ref[...], c32_ref[...], s32_ref[...]
        for i in range(3):
            sl = slice(i * LANES, (i + 1) * LANES)
            zs_ref[:, sl] = _rope_slab(zs[:, sl], c64, s64, HEAD_DIM // 2)
        zs_ref[:, 3 * LANES:4 * LANES] = zs[:, 3 * LANES:4 * LANES]
        for i in range(4):
            sl = slice(i * LANES, (i + 1) * LANES)
            zd_ref[:, sl] = _rope_slab(zd[:, sl], c32, s32, DIFF_QK_DIM // 2)
        zd_ref[:, 4 * LANES:6 * LANES] = zd[:, 4 * LANES:6 * LANES]
    else:
        zs_ref[...] = zs
        zd_ref[...] = zd


def _in_proj(x, mod4, norm_g, w_in_b, l, seq_len, rope_tabs):
    t = x.shape[0]
    tm = ROW_TILE
    rope = rope_tabs is not None
    if rope:
        tiles_per_seq = seq_len // tm
        mod_map = lambda i: (l, 1 + i // tiles_per_seq, 0, 0)
    else:
        mod_map = lambda i: (l, 0, 0, 0)
    in_specs = [
        pl.BlockSpec((tm, D_MODEL), lambda i: (i, 0)),
        pl.BlockSpec((None, None, 1, N_MOD * D_MODEL), mod_map),
        pl.BlockSpec((None, 1, D_MODEL), lambda i: (l, 0, 0)),
        pl.BlockSpec((None, D_MODEL, D_IN), lambda i: (l, 0, 0)),
    ]
    args = [x, mod4, norm_g, w_in_b]
    if rope:
        tab_spec = pl.BlockSpec((tm, LANES), lambda i: (i % tiles_per_seq, 0))
        in_specs += [tab_spec] * 4
        args += list(rope_tabs)
    widths = (4 * GROUP_W, 2 * GROUP_W, 2 * GROUP_W, 3 * GROUP_W)
    return pl.pallas_call(
        functools.partial(_in_proj_kernel, rope=rope),
        grid=(t // tm,),
        in_specs=in_specs,
        out_specs=[pl.BlockSpec((tm, w), lambda i: (i, 0)) for w in widths],
        out_shape=[jax.ShapeDtypeStruct((t, w), F32) for w in widths],
        compiler_params=pltpu.CompilerParams(
            dimension_semantics=("arbitrary",), vmem_limit_bytes=VMEM_LIMIT),
        name="in_proj_rope" if rope else "in_proj",
    )(*args)


def _scan_levels(a_slabs, u_slabs, pa_ref, pb_ref, sa_ref, sb_ref, h0, n0, reverse):
    order = list(range(SUBLANES - 1, -1, -1)) if reverse else list(range(SUBLANES))
    acc_a = acc_b = None
    for r in order:
        if acc_a is None:
            acc_a, acc_b = a_slabs[r], u_slabs[r]
        else:
            acc_b = a_slabs[r] * acc_b + u_slabs[r]
            acc_a = a_slabs[r] * acc_a
        pa_ref[r] = acc_a
        pb_ref[r] = acc_b
    pad = n0 // 2
    ident = pad + n0 if reverse else 0
    sa_ref[ident:ident + pad, :] = jnp.ones((pad, GROUP_W), F32)
    sb_ref[ident:ident + pad, :] = jnp.zeros((pad, GROUP_W), F32)
    sign = 1 if reverse else -1
    k = 1
    while k < n0:
        sa_ref[pad:pad + n0, :] = acc_a
        sb_ref[pad:pad + n0, :] = acc_b
        off = pad + sign * k
        acc_b = acc_a * sb_ref[off:off + n0, :] + acc_b
        acc_a = acc_a * sa_ref[off:off + n0, :]
        k *= 2
    sa_ref[pad:pad + n0, :] = acc_a
    sb_ref[pad:pad + n0, :] = acc_b
    off = pad + sign
    carry_in = sa_ref[off:off + n0, :] * h0 + sb_ref[off:off + n0, :]
    end = 0 if reverse else n0 - 1
    final = acc_a[end:end + 1, :] * h0 + acc_b[end:end + 1, :]
    h_slabs = [pa_ref[r] * carry_in + pb_ref[r] for r in range(SUBLANES)]
    return h_slabs, final


def _recurrent_kernel(*refs, seq_len, has_state):
    if has_state:
        (scal_ref, zr_ref, zl_ref, s0_ref, h0_ref, lgl_ref, gng_ref, cw_ref, cb_ref, wbd_ref, lb_ref, lam_ref,
         y_ref,
         d_ref, vec_ref, pad_ref, a_ref, u_ref, hf_ref, hs_ref, pa_ref, pb_ref, sa_ref, sb_ref) = refs
    else:
        (scal_ref, zr_ref, zl_ref, lgl_ref, gng_ref, cw_ref, cb_ref, wbd_ref, lb_ref, lam_ref,
         y_ref, sret_ref, slru_ref,
         d_ref, vec_ref, pad_ref, a_ref, u_ref, hf_ref, hs_ref, pa_ref, pb_ref, sa_ref, sb_ref) = refs
    ch = RET_CHUNK
    n_chunks = seq_len // ch
    n0 = seq_len // SUBLANES
    heads = GROUP_W // HEAD_DIM

    @pl.when(pl.program_id(0) == 0)
    def _():
        n = lax.broadcasted_iota(jnp.int32, (ch, ch), 0)
        m = lax.broadcasted_iota(jnp.int32, (ch, ch), 1)
        dist = (n - m).astype(F32)
        for h in range(heads):
            lg = jnp.where(dist > 0, scal_ref[h], scal_ref[heads + h])
            d_ref[h] = jnp.where(dist == 0, 2.0, jnp.exp(jnp.abs(dist) * lg))
        t = lax.broadcasted_iota(jnp.int32, (ch, GROUP_W), 0).astype(F32)
        lgf = lgl_ref[0:1, :]
        lgb = lgl_ref[1:2, :]
        vec_ref[0] = jnp.exp((t + 1.0) * lgf)
        vec_ref[1] = jnp.exp((ch - t) * lgb)
        vec_ref[2] = jnp.exp((ch - 1.0 - t) * lgf)
        vec_ref[3] = jnp.exp(t * lgb)

    low = _lane((1, LANES)) < HEAD_DIM
    rr = lax.broadcasted_iota(jnp.int32, (LANES, LANES), 0) < HEAD_DIM
    cc = lax.broadcasted_iota(jnp.int32, (LANES, LANES), 1) < HEAD_DIM
    block_diag = (rr == cc).astype(F32)
    g_chunk_f = jnp.exp(ch * lgl_ref[0:1, :])
    g_chunk_b = jnp.exp(ch * lgl_ref[1:2, :])

    def kv(c, p):
        rows = slice(c * ch, (c + 1) * ch)
        k = zr_ref[rows, GROUP_W + p * LANES:GROUP_W + (p + 1) * LANES] * (HEAD_DIM ** -0.5)
        v = zr_ref[rows, 2 * GROUP_W + p * LANES:2 * GROUP_W + (p + 1) * LANES].astype(BF16)
        return k, v

    def state_update(k, v, zeta):
        return _dot_tn((k * zeta).astype(BF16), v) * block_diag

    for p in range(2):
        pl_sl = slice(p * LANES, (p + 1) * LANES)
        if has_state:
            s_b = [None] * n_chunks
            s_b[n_chunks - 1] = s0_ref[1, p]
            for c in range(n_chunks - 1, 0, -1):
                k, v = kv(c, p)
                s_b[c - 1] = s_b[c] * g_chunk_b[:, pl_sl] + state_update(k, v, vec_ref[3, :, pl_sl])
            s_f = s0_ref[0, p]
        for c in range(n_chunks):
            rows = slice(c * ch, (c + 1) * ch)
            q = zr_ref[rows, pl_sl]
            k, v = kv(c, p)
            kb = k.astype(BF16)
            outs = []
            for j in range(2):
                qm = jnp.where(low if j == 0 else ~low, q, 0.0).astype(BF16)
                w = (_dot_nt(qm, kb) * d_ref[2 * p + j]).astype(BF16)
                outs.append(_dot(w, v))
            o = jnp.where(low, outs[0], outs[1])
            if has_state:
                qb = q.astype(BF16)
                o = o + _dot(qb, s_f.astype(BF16)) * vec_ref[0, :, pl_sl]
                o = o + _dot(qb, s_b[c].astype(BF16)) * vec_ref[1, :, pl_sl]
                if c + 1 < n_chunks:
                    s_f = s_f * g_chunk_f[:, pl_sl] + state_update(k, v, vec_ref[2, :, pl_sl])
            else:
                sret_ref[0, p] = state_update(k, v, vec_ref[2, :, pl_sl])
                sret_ref[1, p] = state_update(k, v, vec_ref[3, :, pl_sl])
            inv = 1.0 / HEAD_DIM
            mu = jnp.where(low, jnp.sum(jnp.where(low, o, 0.0), -1, keepdims=True),
                           jnp.sum(jnp.where(low, 0.0, o), -1, keepdims=True)) * inv
            dlt = o - mu
            sq = dlt * dlt
            var = jnp.where(low, jnp.sum(jnp.where(low, sq, 0.0), -1, keepdims=True),
                            jnp.sum(jnp.where(low, 0.0, sq), -1, keepdims=True)) * inv
            gate = zr_ref[rows, 3 * GROUP_W + p * LANES:3 * GROUP_W + (p + 1) * LANES]
            y = dlt * lax.rsqrt(var + EPS) * gng_ref[:, pl_sl] * jax.nn.silu(gate)
            y_ref[rows, pl_sl] = y.astype(BF16)

    x = zl_ref[:, 0:GROUP_W]
    zero8 = jnp.zeros((SUBLANES, GROUP_W), F32)
    pad_ref[0:SUBLANES, :] = zero8
    pad_ref[SUBLANES:SUBLANES + seq_len, :] = x
    pad_ref[SUBLANES + seq_len:2 * SUBLANES + seq_len, :] = zero8
    cw = cw_ref[...]
    xc = (cw[0:1] * pad_ref[SUBLANES - 2:SUBLANES - 2 + seq_len, :]
          + cw[1:2] * pad_ref[SUBLANES - 1:SUBLANES - 1 + seq_len, :]
          + cw[2:3] * x
          + cw[3:4] * pad_ref[SUBLANES + 1:SUBLANES + 1 + seq_len, :]
          + cb_ref[...])
    pre = _dot(xc.astype(BF16), wbd_ref[...]) + lb_ref[...]
    for d in range(2):
        r_gate = jax.nn.sigmoid(pre[:, (2 * d) * GROUP_W:(2 * d + 1) * GROUP_W])
        i_gate = jax.nn.sigmoid(pre[:, (2 * d + 1) * GROUP_W:(2 * d + 2) * GROUP_W])
        log_a = -LRU_C * r_gate * jax.nn.softplus(-lam_ref[d:d + 1, :])
        a_val = jnp.exp(log_a)
        u_val = jnp.sqrt(-jnp.tanh(log_a) * (a_val * a_val + 1.0)) * (i_gate * xc)
        for hh in range(2):
            a_ref[d, hh] = a_val[:, hh * LANES:(hh + 1) * LANES]
            u_ref[d, hh] = u_val[:, hh * LANES:(hh + 1) * LANES]

    def slab(ref, d, r):
        return jnp.concatenate([ref[d, hh, pl.ds(r, n0, stride=SUBLANES), :] for hh in range(2)], axis=1)

    finals = []
    for d in range(2):
        a_slabs = [slab(a_ref, d, r) for r in range(SUBLANES)]
        u_slabs = [slab(u_ref, d, r) for r in range(SUBLANES)]
        h0 = h0_ref[d:d + 1, :] if has_state else jnp.zeros((1, GROUP_W), F32)
        h_slabs, fin = _scan_levels(a_slabs, u_slabs, pa_ref, pb_ref, sa_ref, sb_ref, h0, n0, reverse=(d == 1))
        finals.append(fin)
        for r in range(SUBLANES):
            if d == 0:
                hf_ref[r] = h_slabs[r]
            else:
                h_sum = hf_ref[r] + h_slabs[r]
                for hh in range(2):
                    hs_ref[hh, pl.ds(r, n0, stride=SUBLANES), :] = h_sum[:, hh * LANES:(hh + 1) * LANES]
    h_both = jnp.concatenate([hs_ref[0], hs_ref[1]], axis=1)
    y = h_both * jax.nn.gelu(zl_ref[:, GROUP_W:2 * GROUP_W])
    y_ref[:, GROUP_W:2 * GROUP_W] = y.astype(BF16)
    if not has_state:
        slru_ref[0:1, :] = finals[0]
        slru_ref[1:2, :] = finals[1]


def _recurrent(scal, zr, zl, lp, l, seq_len, state=None):
    t = zr.shape[0]
    b = t // seq_len
    n0 = seq_len // SUBLANES
    has_state = state is not None
    assert has_state or seq_len == RET_CHUNK
    zr3 = zr.reshape(b, seq_len, 4 * GROUP_W)
    zl3 = zl.reshape(b, seq_len, 2 * GROUP_W)
    full = lambda a: pl.BlockSpec(a.shape, lambda i: (0,) * a.ndim)
    in_specs = [
        pl.BlockSpec(memory_space=pltpu.SMEM),
        pl.BlockSpec((None, seq_len, 4 * GROUP_W), lambda i: (i, 0, 0)),
        pl.BlockSpec((None, seq_len, 2 * GROUP_W), lambda i: (i, 0, 0)),
    ]
    args = [scal, zr3, zl3]
    if has_state:
        s0, h0 = state
        in_specs += [pl.BlockSpec((None, None, 2, 2, LANES, LANES), lambda i: (i, l, 0, 0, 0, 0)),
                     pl.BlockSpec((None, None, 2, GROUP_W), lambda i: (i, l, 0, 0))]
        args += [s0, h0]
    params = [lp["lg_lane"], lp["gn_g"], lp["conv_w"], lp["conv_b"], lp["lru_wbd"], lp["lru_bias"], lp["lru_lam"]]
    in_specs += [full(a) for a in params]
    args += params
    out_specs = [pl.BlockSpec((None, seq_len, 2 * GROUP_W), lambda i: (i, 0, 0))]
    out_shape = [jax.ShapeDtypeStruct((b, seq_len, 2 * GROUP_W), BF16)]
    if not has_state:
        out_specs += [pl.BlockSpec((None, 2, 2, LANES, LANES), lambda i: (i, 0, 0, 0, 0)),
                      pl.BlockSpec((None, 2, GROUP_W), lambda i: (i, 0, 0))]
        out_shape += [jax.ShapeDtypeStruct((b, 2, 2, LANES, LANES), F32),
                      jax.ShapeDtypeStruct((b, 2, GROUP_W), F32)]
    scratch = [
        pltpu.VMEM((GROUP_W // HEAD_DIM, RET_CHUNK, RET_CHUNK), F32),
        pltpu.VMEM((4, RET_CHUNK, GROUP_W), F32),
        pltpu.VMEM((seq_len + 2 * SUBLANES, GROUP_W), F32),
        pltpu.VMEM((2, 2, seq_len, LANES), F32),
        pltpu.VMEM((2, 2, seq_len, LANES), F32),
        pltpu.VMEM((SUBLANES, n0, GROUP_W), F32),
        pltpu.VMEM((2, seq_len, LANES), F32),
        pltpu.VMEM((SUBLANES, n0, GROUP_W), F32),
        pltpu.VMEM((SUBLANES, n0, GROUP_W), F32),
        pltpu.VMEM((2 * n0, GROUP_W), F32),
        pltpu.VMEM((2 * n0, GROUP_W), F32),
    ]
    outs = pl.pallas_call(
        functools.partial(_recurrent_kernel, seq_len=seq_len, has_state=has_state),
        grid=(b,),
        in_specs=in_specs,
        out_specs=out_specs,
        out_shape=out_shape,
        scratch_shapes=scratch,
        compiler_params=pltpu.CompilerParams(
            dimension_semantics=("arbitrary",), vmem_limit_bytes=VMEM_LIMIT),
        name="recurrent_latent" if has_state else "recurrent_ctx",
    )(*args)
    return (outs[0].reshape(t, 2 * GROUP_W),) + tuple(outs[1:])


def _dup_half(x, half):
    sw = pltpu.roll(x, HEAD_DIM, axis=1)
    low = _lane(x.shape) < HEAD_DIM
    return jnp.where(low, x, sw) if half == 0 else jnp.where(low, sw, x)


def _attention_kernel(*refs, seq_len, has_ctx):
    if has_ctx:
        (scal_ref, sq_ref, skv_ref, dq_ref, dkv_ref, cks_ref, cvs_ref, ckd_ref, cvd_ref, ng_ref, y_ref,
         sk_s, sv_s, dk_s, dv_s, csk_s, csv_s, cdk_s, cdv_s) = refs
    else:
        (scal_ref, sq_ref, skv_ref, dq_ref, dkv_ref, ng_ref, y_ref, sk_s, sv_s, dk_s, dv_s) = refs
    tq = ATTN_Q_TILE
    heads = GROUP_W // HEAD_DIM
    lane = _lane((1, LANES))
    low = lane < HEAD_DIM
    halves = (low, ~low)

    @pl.when(pl.program_id(1) == 0)
    def _():
        def fill(k_all, v_all, k_dst, v_dst, dup):
            for p in range(2):
                if dup:
                    k_p, v_p = _dup_half(k_all, p), _dup_half(v_all, p)
                else:
                    k_p, v_p = k_all[:, p * LANES:(p + 1) * LANES], v_all[:, p * LANES:(p + 1) * LANES]
                k_dst[p] = k_p.astype(BF16)
                for j in range(2):
                    v_dst[2 * p + j] = jnp.where(halves[j], v_p, 1.0).astype(BF16)

        fill(skv_ref[:, 2 * LANES:3 * LANES], skv_ref[:, 3 * LANES:4 * LANES], sk_s, sv_s, True)
        fill(dkv_ref[:, GROUP_W:2 * GROUP_W], dkv_ref[:, 2 * GROUP_W:3 * GROUP_W], dk_s, dv_s, False)
        if has_ctx:
            fill(cks_ref[...], cvs_ref[...], csk_s, csv_s, True)
            fill(ckd_ref[...], cvd_ref[...], cdk_s, cdv_s, False)

    def softmax_pv(qm, keep, k_loc, v_loc, k_ctx, v_ctx, valid, extra2):
        s = _dot_nt(qm, k_loc)
        if valid is not None:
            s = jnp.where(valid, s, NEG_BIG)
        m = jnp.max(s, -1, keepdims=True)
        if k_ctx is not None:
            sc = _dot_nt(qm, k_ctx)
            m = jnp.maximum(m, jnp.max(sc, -1, keepdims=True))
        if extra2 is not None:
            m = jnp.maximum(m, extra2)
        o = _dot(jnp.exp2((s - m).astype(BF16)), v_loc)
        if k_ctx is not None:
            o = o + _dot(jnp.exp2((sc - m).astype(BF16)), v_ctx)
        den = pltpu.roll(o, HEAD_DIM, axis=1)
        if extra2 is not None:
            den = den + jnp.exp2(extra2 - m)
        return o / jnp.where(keep, den, 1.0)

    if has_ctx:
        win = 2 * tq
        i = pl.program_id(1)
        blk = jnp.clip(2 * i - 1, 0, (seq_len - win) // SWA_WINDOW)
        start = pl.multiple_of(blk * SWA_WINDOW, SWA_WINDOW)
        rows = pl.ds(start, win)
        qpos = i * tq + lax.broadcasted_iota(jnp.int32, (tq, win), 0)
        kpos = start + lax.broadcasted_iota(jnp.int32, (tq, win), 1)
        valid = jnp.abs(kpos - qpos) <= SWA_WINDOW
    else:
        rows = slice(None)
        valid = None
    qscale = (HEAD_DIM ** -0.5) * LOG2E
    for p in range(2):
        q = sq_ref[:, p * LANES:(p + 1) * LANES] * qscale
        outs = []
        for j in range(2):
            sink2 = scal_ref[2 * heads + 2 * p + j] * LOG2E
            qm = jnp.where(halves[j], q, 0.0).astype(BF16)
            outs.append(softmax_pv(qm, halves[j], sk_s[p, rows, :], sv_s[2 * p + j, rows, :],
                                   csk_s[p] if has_ctx else None, csv_s[2 * p + j] if has_ctx else None,
                                   valid, sink2))
        y_ref[:, p * LANES:(p + 1) * LANES] = jnp.where(low, outs[0], outs[1]).astype(BF16)

    lam = scal_ref[3 * heads]
    coef = scal_ref[3 * heads + 1]
    dscale = (DIFF_QK_DIM ** -0.5) * LOG2E
    grp = lane >> 5
    for p in range(2):
        q = dq_ref[:, p * LANES:(p + 1) * LANES] * dscale
        outs = []
        for j in range(2):
            parts = []
            for c in range(2):
                qm = jnp.where(grp == 2 * j + c, q, 0.0).astype(BF16)
                parts.append(softmax_pv(qm, halves[j], dk_s[p], dv_s[2 * p + j],
                                        cdk_s[p] if has_ctx else None, cdv_s[2 * p + j] if has_ctx else None,
                                        None, None))
            outs.append(parts[0] - lam * parts[1])
        o = jnp.where(low, outs[0], outs[1])
        sq = o * o
        ms = jnp.where(low, jnp.sum(jnp.where(low, sq, 0.0), -1, keepdims=True),
                       jnp.sum(jnp.where(low, 0.0, sq), -1, keepdims=True)) * (1.0 / HEAD_DIM)
        y = o * lax.rsqrt(ms + EPS) * ng_ref[:, p * LANES:(p + 1) * LANES] * coef
        y_ref[:, GROUP_W + p * LANES:GROUP_W + (p + 1) * LANES] = y.astype(BF16)


def _attention(scal, zs, zd, norm_g_tiled, l, seq_len, caches=None):
    t = zs.shape[0]
    b = t // seq_len
    tq = ATTN_Q_TILE
    nq = seq_len // tq
    has_ctx = caches is not None
    zs3 = zs.reshape(b, seq_len, 2 * GROUP_W)
    zd3 = zd.reshape(b, seq_len, 3 * GROUP_W)
    in_specs = [
        pl.BlockSpec(memory_space=pltpu.SMEM),
        pl.BlockSpec((None, tq, 2 * GROUP_W), lambda i, j: (i, j, 0)),
        pl.BlockSpec((None, seq_len, 2 * GROUP_W), lambda i, j: (i, 0, 0)),
        pl.BlockSpec((None, tq, 3 * GROUP_W), lambda i, j: (i, j, 0)),
        pl.BlockSpec((None, seq_len, 3 * GROUP_W), lambda i, j: (i, 0, 0)),
    ]
    args = [scal, zs3, zs3, zd3, zd3]
    heads = GROUP_W // HEAD_DIM
    scratch = [pltpu.VMEM((2, seq_len, LANES), BF16), pltpu.VMEM((heads, seq_len, LANES), BF16),
               pltpu.VMEM((2, seq_len, LANES), BF16), pltpu.VMEM((heads, seq_len, LANES), BF16)]
    if has_ctx:
        past = caches[0].shape[2]
        for a in caches:
            in_specs.append(pl.BlockSpec((None, None) + a.shape[2:], lambda i, j: (i, l, 0, 0)))
            args.append(a)
        scratch += [pltpu.VMEM((2, past, LANES), BF16), pltpu.VMEM((heads, past, LANES), BF16),
                    pltpu.VMEM((2, past, LANES), BF16), pltpu.VMEM((heads, past, LANES), BF16)]
    in_specs.append(pl.BlockSpec(norm_g_tiled.shape, lambda i, j: (0, 0)))
    args.append(norm_g_tiled)
    y = pl.pallas_call(
        functools.partial(_attention_kernel, seq_len=seq_len, has_ctx=has_ctx),
        grid=(b, nq),
        in_specs=in_specs,
        out_specs=pl.BlockSpec((None, tq, 2 * GROUP_W), lambda i, j: (i, j, 0)),
        out_shape=jax.ShapeDtypeStruct((b, seq_len, 2 * GROUP_W), BF16),
        scratch_shapes=scratch,
        compiler_params=pltpu.CompilerParams(
            dimension_semantics=("arbitrary", "arbitrary"), vmem_limit_bytes=VMEM_LIMIT),
        name="attention_latent" if has_ctx else "attention_ctx",
    )(*args)
    return y.reshape(t, 2 * GROUP_W)


def _out_mlp_kernel(x_ref, yr_ref, ya_ref, mod_ref, wo_ref, g_ref, w1_ref, w2_ref, gf_ref, o_ref, *, final):
    g1 = mod_ref[:, 2 * D_MODEL:3 * D_MODEL]
    sh2 = mod_ref[:, 3 * D_MODEL:4 * D_MODEL]
    sc2 = mod_ref[:, 4 * D_MODEL:5 * D_MODEL]
    g2 = mod_ref[:, 5 * D_MODEL:6 * D_MODEL]
    y = _dot(yr_ref[...], wo_ref[0:2 * GROUP_W, :]) + _dot(ya_ref[...], wo_ref[2 * GROUP_W:4 * GROUP_W, :])
    x = x_ref[...] + g1 * y
    h = (_rms(x) * g_ref[...] * (1.0 + sc2) + sh2).astype(BF16)
    acc = None
    for c in range(D_FF // D_MODEL):
        sl = slice(c * D_MODEL, (c + 1) * D_MODEL)
        hid = jnp.square(jnp.maximum(_dot(h, w1_ref[:, sl]), 0.0)).astype(BF16)
        part = _dot(hid, w2_ref[sl, :])
        acc = part if acc is None else acc + part
    x = x + g2 * acc
    if final:
        x = _rms(x) * gf_ref[...]
    o_ref[...] = x


def _out_mlp(x, y_rec, y_att, mod4, w_out_b, norm_g, w_ff1_b, w_ff2_b, final_g, l, seq_len, is_latent, final):
    t = x.shape[0]
    tm = ROW_TILE
    if is_latent:
        tiles_per_seq = seq_len // tm
        mod_map = lambda i: (l, 1 + i // tiles_per_seq, 0, 0)
    else:
        mod_map = lambda i: (l, 0, 0, 0)
    once = pl.Buffered(1)
    in_specs = [
        pl.BlockSpec((tm, D_MODEL), lambda i: (i, 0)),
        pl.BlockSpec((tm, 2 * GROUP_W), lambda i: (i, 0)),
        pl.BlockSpec((tm, 2 * GROUP_W), lambda i: (i, 0)),
        pl.BlockSpec((None, None, 1, N_MOD * D_MODEL), mod_map),
        pl.BlockSpec((None, D_MODEL, D_MODEL), lambda i: (l, 0, 0), pipeline_mode=once),
        pl.BlockSpec((None, 1, D_MODEL), lambda i: (l, 0, 0)),
        pl.BlockSpec((None, D_MODEL, D_FF), lambda i: (l, 0, 0), pipeline_mode=once),
        pl.BlockSpec((None, D_FF, D_MODEL), lambda i: (l, 0, 0), pipeline_mode=once),
        pl.BlockSpec((1, D_MODEL), lambda i: (0, 0)),
    ]
    return pl.pallas_call(
        functools.partial(_out_mlp_kernel, final=final),
        grid=(t // tm,),
        in_specs=in_specs,
        out_specs=pl.BlockSpec((tm, D_MODEL), lambda i: (i, 0)),
        out_shape=jax.ShapeDtypeStruct((t, D_MODEL), F32),
        compiler_params=pltpu.CompilerParams(
            dimension_semantics=("arbitrary",), vmem_limit_bytes=VMEM_LIMIT),
        name="out_mlp_final" if final else "out_mlp",
    )(x, y_rec, y_att, mod4, w_out_b, norm_g, w_ff1_b, w_ff2_b, final_g)


def _rope_tables(length):
    row = (jnp.arange(length) // GRID_W).astype(F32)
    col = (jnp.arange(length) % GRID_W).astype(F32)
    tabs = []
    for dim in (HEAD_DIM, DIFF_QK_DIM):
        n = dim // 4
        inv = ROPE_BASE ** (-jnp.arange(n, dtype=F32) / n)
        ang = jnp.concatenate([row[:, None] * inv, col[:, None] * inv], -1)
        cos, sin = jnp.cos(ang), jnp.sin(ang)
        reps = LANES // dim
        tabs.append(jnp.tile(jnp.concatenate([cos, cos], -1), (1, reps)))
        tabs.append(jnp.tile(jnp.concatenate([-sin, sin], -1), (1, reps)))
    return tabs


def _block_diag(w):
    n, c, d = w.shape
    eye = jnp.eye(n, dtype=w.dtype)
    return jnp.einsum("ncd,nm->ncmd", w, eye).reshape(n * c, n * d)


def _pair_states(s):
    *lead, h, d, _ = s.shape
    s = s.reshape(-1, h // 2, 2, d, d)
    eye = jnp.eye(2, dtype=s.dtype)
    return jnp.einsum("npjde,jk->npjdke", s, eye).reshape(*lead, h // 2, 2 * d, 2 * d)


def _unpair_states(s):
    b, two, hp, dd, _ = s.shape
    d = dd // 2
    blocks = jnp.stack([s[..., 0:d, 0:d], s[..., d:dd, d:dd]], axis=3)
    return blocks.reshape(b, two, 2 * hp, d, d)


def _layer_params(l, ret_decay, ret_gn_g, lru_conv_w, lru_conv_b, lru_w_a, lru_b_a, lru_w_x, lru_b_x, lru_lambda,
                  swa_sink, diff_lambda, diff_norm_g):
    heads = GROUP_W // HEAD_DIM
    log_gamma = jax.nn.log_sigmoid(ret_decay[l].astype(F32))
    lambda_init = 0.8 - 0.6 * math.exp(-0.3 * l)
    lv = diff_lambda[l].astype(F32)
    lam = jnp.exp(jnp.sum(lv[0] * lv[1])) - jnp.exp(jnp.sum(lv[2] * lv[3])) + lambda_init
    scal = jnp.concatenate([log_gamma[0], log_gamma[1], swa_sink[l].astype(F32),
                            jnp.stack([lam, jnp.asarray(1.0 - lambda_init, F32)]),
                            jnp.zeros((2,), F32)])
    wbd = jnp.concatenate([_block_diag(lru_w_a[l, 0]), _block_diag(lru_w_x[l, 0]),
                           _block_diag(lru_w_a[l, 1]), _block_diag(lru_w_x[l, 1])], axis=1).astype(BF16)
    bias = jnp.concatenate([lru_b_a[l, 0], lru_b_x[l, 0], lru_b_a[l, 1], lru_b_x[l, 1]])[None, :]
    return dict(
        scal=scal,
        lg_lane=jnp.repeat(log_gamma, HEAD_DIM, axis=1),
        gn_g=ret_gn_g[l][None, :],
        conv_w=lru_conv_w[l],
        conv_b=lru_conv_b[l][None, :],
        lru_wbd=wbd,
        lru_bias=bias,
        lru_lam=lru_lambda[l],
        diff_ng=jnp.tile(diff_norm_g[l], heads)[None, :],
    )


def kernel(x_prompt, x_sample, c, state_ret, state_lru, cache_swa_k, cache_swa_v, cache_diff_k, cache_diff_v,
           c_ctx, w_ada, b_ada, norm_mix_g, w_in, ret_decay, ret_gn_g, lru_conv_w, lru_conv_b,
           lru_w_a, lru_b_a, lru_w_x, lru_b_x, lru_lambda, swa_sink, diff_lambda, diff_norm_g,
           w_out, norm_mlp_g, w_ff1, w_ff2, final_norm_g):
    batch, seq, _ = x_prompt.shape
    dec_batch, dec_seq, _ = x_sample.shape
    past = cache_swa_k.shape[2]

    cvec = jnp.concatenate([c_ctx[None, :], c, jnp.zeros((MOD_ROWS - 1 - dec_batch, D_MODEL), F32)], axis=0)
    mod4 = _modulation(cvec, w_ada, b_ada).reshape(DEPTH, MOD_ROWS, 1, N_MOD * D_MODEL)

    w_in_b = w_in.astype(BF16)
    w_out_b = w_out.astype(BF16)
    w_ff1_b = w_ff1.astype(BF16)
    w_ff2_b = w_ff2.astype(BF16)
    norm_mix3 = norm_mix_g.reshape(DEPTH, 1, D_MODEL)
    norm_mlp3 = norm_mlp_g.reshape(DEPTH, 1, D_MODEL)
    final_g = final_norm_g[None, :]
    rope_tabs = _rope_tables(dec_seq)
    layers = [_layer_params(l, ret_decay, ret_gn_g, lru_conv_w, lru_conv_b, lru_w_a, lru_b_a, lru_w_x, lru_b_x,
                            lru_lambda, swa_sink, diff_lambda, diff_norm_g) for l in range(DEPTH)]

    heads = GROUP_W // HEAD_DIM

    xp = x_prompt.reshape(batch * seq, D_MODEL)
    new_ret, new_lru, new_swa, new_diff = [], [], [], []
    for l in range(DEPTH):
        lp = layers[l]
        zr, zl, zs, zd = _in_proj(xp, mod4, norm_mix3, w_in_b, l, seq, None)
        y_rec, s_ret, s_lru = _recurrent(lp["scal"], zr, zl, lp, l, seq)
        y_att = _attention(lp["scal"], zs, zd, lp["diff_ng"], l, seq)
        xp = _out_mlp(xp, y_rec, y_att, mod4, w_out_b, norm_mlp3, w_ff1_b, w_ff2_b, final_g,
                      l, seq, False, l == DEPTH - 1)
        new_ret.append(s_ret)
        new_lru.append(s_lru)
        new_swa.append(zs[:, GROUP_W:2 * GROUP_W])
        new_diff.append(zd[:, GROUP_W:3 * GROUP_W])
    y_prompt = xp.reshape(batch, seq, D_MODEL)
    new_ret = _unpair_states(jnp.stack(new_ret, axis=1).reshape(batch * DEPTH, 2, 2, LANES, LANES))
    new_ret = new_ret.reshape(batch, DEPTH, 2, heads, HEAD_DIM, HEAD_DIM)
    new_swa = jnp.stack(new_swa, axis=1).reshape(batch, seq, DEPTH, 2, 2, HEAD_DIM)
    new_swa = jnp.transpose(new_swa, (3, 0, 2, 1, 4, 5))
    new_diff = jnp.stack(new_diff, axis=1).reshape(batch, seq, DEPTH, 2, heads, HEAD_DIM)
    new_diff = jnp.transpose(new_diff, (3, 0, 2, 1, 4, 5))

    xs = x_sample.reshape(dec_batch * dec_seq, D_MODEL)
    state = (_pair_states(state_ret), state_lru)
    caches = (cache_swa_k.reshape(dec_batch, DEPTH, past, 2 * HEAD_DIM),
              cache_swa_v.reshape(dec_batch, DEPTH, past, 2 * HEAD_DIM),
              cache_diff_k.reshape(dec_batch, DEPTH, past, GROUP_W),
              cache_diff_v.reshape(dec_batch, DEPTH, past, GROUP_W))
    for l in range(DEPTH):
        lp = layers[l]
        zr, zl, zs, zd = _in_proj(xs, mod4, norm_mix3, w_in_b, l, dec_seq, rope_tabs)
        (y_rec,) = _recurrent(lp["scal"], zr, zl, lp, l, dec_seq, state)
        y_att = _attention(lp["scal"], zs, zd, lp["diff_ng"], l, dec_seq, caches)
        xs = _out_mlp(xs, y_rec, y_att, mod4, w_out_b, norm_mlp3, w_ff1_b, w_ff2_b, final_g,
                      l, dec_seq, True, l == DEPTH - 1)
    y_sample = xs.reshape(dec_batch, dec_seq, D_MODEL)

    return (y_prompt, y_sample, new_ret, jnp.stack(new_lru, axis=1),
            new_swa[0], new_swa[1], new_diff[0], new_diff[1])
```

```python
import functools
import math

import jax
import jax.numpy as jnp
from jax import lax
from jax.experimental import pallas as pl
from jax.experimental.pallas import tpu as pltpu

F32 = jnp.float32
BF16 = jnp.bfloat16

D_MODEL = 1024
DEPTH = 4
GRID_W = 64
HEAD_DIM = 64
GROUP_W = D_MODEL // 4
RET_CHUNK = 256
LRU_BLOCKS = 4
LRU_C = 8.0
SWA_WINDOW = 128
DIFF_QK_DIM = HEAD_DIM // 2
D_FF = 4 * D_MODEL
ROPE_BASE = 10000.0
EPS = 1e-6
N_MOD = 6
D_IN = 11 * GROUP_W
MOD_ROWS = 16

LANES = 128
SUBLANES = 8
VMEM_LIMIT = 56 * 1024 * 1024
NEG_BIG = -1e30
LOG2E = math.log2(math.e)

ROW_TILE = 512
ATTN_Q_TILE = 256
CTX_SEQS_PER_STEP = 2


def _dot(a, b):
    return jnp.dot(a, b, preferred_element_type=F32)


def _dot_nt(a, b):
    return lax.dot_general(a, b, (((1,), (1,)), ((), ())), preferred_element_type=F32)


def _dot_tn(a, b):
    return lax.dot_general(a, b, (((0,), (0,)), ((), ())), preferred_element_type=F32)


def _rms(x):
    return x * lax.rsqrt(jnp.mean(x * x, axis=-1, keepdims=True) + EPS)


def _sigmoid(x):
    return 0.5 * jnp.tanh(0.5 * x) + 0.5


def _lane(shape):
    return lax.broadcasted_iota(jnp.int32, shape, len(shape) - 1)


def _mod_kernel(c_ref, w_ref, b_ref, o_ref):
    s = jax.nn.silu(c_ref[...]).astype(BF16)
    o_ref[...] = _dot(s, w_ref[...].astype(BF16)) + b_ref[...]


def _modulation(cvec, w_ada, b_ada):
    tn = 1536
    n_mod = N_MOD * D_MODEL
    return pl.pallas_call(
        _mod_kernel,
        grid=(DEPTH, n_mod // tn),
        in_specs=[
            pl.BlockSpec((MOD_ROWS, D_MODEL), lambda l, j: (0, 0)),
            pl.BlockSpec((None, D_MODEL, tn), lambda l, j: (l, 0, j)),
            pl.BlockSpec((None, 1, tn), lambda l, j: (l, 0, j)),
        ],
        out_specs=pl.BlockSpec((None, MOD_ROWS, tn), lambda l, j: (l, 0, j)),
        out_shape=jax.ShapeDtypeStruct((DEPTH, MOD_ROWS, n_mod), F32),
        compiler_params=pltpu.CompilerParams(
            dimension_semantics=("arbitrary", "arbitrary"), vmem_limit_bytes=VMEM_LIMIT),
        name="modulation",
    )(cvec, w_ada, b_ada.reshape(DEPTH, 1, n_mod))


def _rope_slab(x, c, s, half):
    first = (_lane(x.shape) & (2 * half - 1)) < half
    swapped = jnp.where(first, pltpu.roll(x, LANES - half, axis=1), pltpu.roll(x, half, axis=1))
    return x * c + swapped * s


def _in_proj_kernel(*refs, rope):
    if rope:
        x_ref, mod_ref, g_ref, w_ref, c64_ref, s64_ref, c32_ref, s32_ref, zr_ref, zl_ref, zs_ref, zd_ref = refs
    else:
        x_ref, mod_ref, g_ref, w_ref, zr_ref, zl_ref, zs_ref, zd_ref = refs
    sh1 = mod_ref[:, 0:D_MODEL]
    sc1 = mod_ref[:, D_MODEL:2 * D_MODEL]
    h = (_rms(x_ref[...]) * g_ref[...] * (1.0 + sc1) + sh1).astype(BF16)
    zr_ref[...] = _dot(h, w_ref[:, 0:4 * GROUP_W])
    zl_ref[...] = _dot(h, w_ref[:, 4 * GROUP_W:6 * GROUP_W])
    zs = _dot(h, w_ref[:, 6 * GROUP_W:8 * GROUP_W])
    zd = _dot(h, w_ref[:, 8 * GROUP_W:11 * GROUP_W])
    if rope:
        c64, s64, c32, s32 = c64_ref[...], s64_ref[...], c32_ref[...], s32_ref[...]
        for i in range(3):
            sl = slice(i * LANES, (i + 1) * LANES)
            zs_ref[:, sl] = _rope_slab(zs[:, sl], c64, s64, HEAD_DIM // 2)
        zs_ref[:, 3 * LANES:4 * LANES] = zs[:, 3 * LANES:4 * LANES]
        for i in range(4):
            sl = slice(i * LANES, (i + 1) * LANES)
            zd_ref[:, sl] = _rope_slab(zd[:, sl], c32, s32, DIFF_QK_DIM // 2)
        zd_ref[:, 4 * LANES:6 * LANES] = zd[:, 4 * LANES:6 * LANES]
    else:
        zs_ref[...] = zs
        zd_ref[...] = zd


def _in_proj(x, mod4, norm_g, w_in_b, l, seq_len, rope_tabs):
    t = x.shape[0]
    tm = ROW_TILE
    rope = rope_tabs is not None
    if rope:
        tiles_per_seq = seq_len // tm
        mod_map = lambda i: (l, 1 + i // tiles_per_seq, 0, 0)
    else:
        mod_map = lambda i: (l, 0, 0, 0)
    in_specs = [
        pl.BlockSpec((tm, D_MODEL), lambda i: (i, 0)),
        pl.BlockSpec((None, None, 1, N_MOD * D_MODEL), mod_map),
        pl.BlockSpec((None, 1, D_MODEL), lambda i: (l, 0, 0)),
        pl.BlockSpec((None, D_MODEL, D_IN), lambda i: (l, 0, 0)),
    ]
    args = [x, mod4, norm_g, w_in_b]
    if rope:
        tab_spec = pl.BlockSpec((tm, LANES), lambda i: (i % tiles_per_seq, 0))
        in_specs += [tab_spec] * 4
        args += list(rope_tabs)
    widths = (4 * GROUP_W, 2 * GROUP_W, 2 * GROUP_W, 3 * GROUP_W)
    return pl.pallas_call(
        functools.partial(_in_proj_kernel, rope=rope),
        grid=(t // tm,),
        in_specs=in_specs,
        out_specs=[pl.BlockSpec((tm, w), lambda i: (i, 0)) for w in widths],
        out_shape=[jax.ShapeDtypeStruct((t, w), F32) for w in widths],
        compiler_params=pltpu.CompilerParams(
            dimension_semantics=("arbitrary",), vmem_limit_bytes=VMEM_LIMIT),
        name="in_proj_rope" if rope else "in_proj",
    )(*args)


def _scan_levels(a_slabs, u_slabs, pa_ref, pb_ref, sa_ref, sb_ref, h0, n0, reverse):
    order = list(range(SUBLANES - 1, -1, -1)) if reverse else list(range(SUBLANES))
    acc_a = acc_b = None
    for r in order:
        if acc_a is None:
            acc_a, acc_b = a_slabs[r], u_slabs[r]
        else:
            acc_b = a_slabs[r] * acc_b + u_slabs[r]
            acc_a = a_slabs[r] * acc_a
        pa_ref[r] = acc_a
        pb_ref[r] = acc_b
    pad = n0 // 2
    ident = pad + n0 if reverse else 0
    sa_ref[ident:ident + pad, :] = jnp.ones((pad, GROUP_W), F32)
    sb_ref[ident:ident + pad, :] = jnp.zeros((pad, GROUP_W), F32)
    sign = 1 if reverse else -1
    k = 1
    while k < n0:
        sa_ref[pad:pad + n0, :] = acc_a
        sb_ref[pad:pad + n0, :] = acc_b
        off = pad + sign * k
        acc_b = acc_a * sb_ref[off:off + n0, :] + acc_b
        acc_a = acc_a * sa_ref[off:off + n0, :]
        k *= 2
    sa_ref[pad:pad + n0, :] = acc_a
    sb_ref[pad:pad + n0, :] = acc_b
    off = pad + sign
    carry_in = sa_ref[off:off + n0, :] * h0 + sb_ref[off:off + n0, :]
    end = 0 if reverse else n0 - 1
    final = acc_a[end:end + 1, :] * h0 + acc_b[end:end + 1, :]
    h_slabs = [pa_ref[r] * carry_in + pb_ref[r] for r in range(SUBLANES)]
    return h_slabs, final


def _recurrent_kernel(*refs, seq_len, has_state):
    if has_state:
        (scal_ref, zr_ref, zl_ref, s0_ref, h0_ref, lgl_ref, gng_ref, cw_ref, cb_ref, wbd_ref, lb_ref, lam_ref,
         y_ref,
         d_ref, vec_ref, pad_ref, a_ref, u_ref, hf_ref, hs_ref, pa_ref, pb_ref, sa_ref, sb_ref) = refs
    else:
        (scal_ref, zr_ref, zl_ref, lgl_ref, gng_ref, cw_ref, cb_ref, wbd_ref, lb_ref, lam_ref,
         y_ref, sret_ref, slru_ref,
         d_ref, vec_ref, pad_ref, a_ref, u_ref, hf_ref, hs_ref, pa_ref, pb_ref, sa_ref, sb_ref) = refs
    ch = RET_CHUNK
    n_chunks = seq_len // ch
    n0 = seq_len // SUBLANES
    heads = GROUP_W // HEAD_DIM

    @pl.when(pl.program_id(0) == 0)
    def _():
        n = lax.broadcasted_iota(jnp.int32, (ch, ch), 0)
        m = lax.broadcasted_iota(jnp.int32, (ch, ch), 1)
        dist = (n - m).astype(F32)
        for h in range(heads):
            lg = jnp.where(dist > 0, scal_ref[h], scal_ref[heads + h])
            d_ref[h] = jnp.where(dist == 0, 2.0, jnp.exp(jnp.abs(dist) * lg))
        t = lax.broadcasted_iota(jnp.int32, (ch, GROUP_W), 0).astype(F32)
        lgf = lgl_ref[0:1, :]
        lgb = lgl_ref[1:2, :]
        vec_ref[0] = jnp.exp((t + 1.0) * lgf)
        vec_ref[1] = jnp.exp((ch - t) * lgb)
        vec_ref[2] = jnp.exp((ch - 1.0 - t) * lgf)
        vec_ref[3] = jnp.exp(t * lgb)

    low = _lane((1, LANES)) < HEAD_DIM
    rr = lax.broadcasted_iota(jnp.int32, (LANES, LANES), 0) < HEAD_DIM
    cc = lax.broadcasted_iota(jnp.int32, (LANES, LANES), 1) < HEAD_DIM
    block_diag = (rr == cc).astype(F32)
    g_chunk_f = jnp.exp(ch * lgl_ref[0:1, :])
    g_chunk_b = jnp.exp(ch * lgl_ref[1:2, :])

    def kv(c, p):
        rows = slice(c * ch, (c + 1) * ch)
        k = zr_ref[rows, GROUP_W + p * LANES:GROUP_W + (p + 1) * LANES] * (HEAD_DIM ** -0.5)
        v = zr_ref[rows, 2 * GROUP_W + p * LANES:2 * GROUP_W + (p + 1) * LANES].astype(BF16)
        return k, v

    def state_update(k, v, zeta):
        return _dot_tn((k * zeta).astype(BF16), v) * block_diag

    def pair_state(d, p):
        zero = jnp.zeros((HEAD_DIM, HEAD_DIM), F32)
        top = jnp.concatenate([s0_ref[d, 2 * p], zero], axis=1)
        bottom = jnp.concatenate([zero, s0_ref[d, 2 * p + 1]], axis=1)
        return jnp.concatenate([top, bottom], axis=0)

    def store_state(d, p, s_pair):
        sret_ref[d, 2 * p] = s_pair[0:HEAD_DIM, 0:HEAD_DIM]
        sret_ref[d, 2 * p + 1] = s_pair[HEAD_DIM:LANES, HEAD_DIM:LANES]

    for p in range(2):
        pl_sl = slice(p * LANES, (p + 1) * LANES)
        if has_state:
            s_b = [None] * n_chunks
            s_b[n_chunks - 1] = pair_state(1, p)
            for c in range(n_chunks - 1, 0, -1):
                k, v = kv(c, p)
                s_b[c - 1] = s_b[c] * g_chunk_b[:, pl_sl] + state_update(k, v, vec_ref[3, :, pl_sl])
            s_f = pair_state(0, p)
        for c in range(n_chunks):
            rows = slice(c * ch, (c + 1) * ch)
            q = zr_ref[rows, pl_sl]
            k, v = kv(c, p)
            kb = k.astype(BF16)
            outs = []
            for j in range(2):
                qm = jnp.where(low if j == 0 else ~low, q, 0.0).astype(BF16)
                w = (_dot_nt(qm, kb) * d_ref[2 * p + j]).astype(BF16)
                outs.append(_dot(w, v))
            o = jnp.where(low, outs[0], outs[1])
            if has_state:
                qb = q.astype(BF16)
                o = o + _dot(qb, s_f.astype(BF16)) * vec_ref[0, :, pl_sl]
                o = o + _dot(qb, s_b[c].astype(BF16)) * vec_ref[1, :, pl_sl]
                if c + 1 < n_chunks:
                    s_f = s_f * g_chunk_f[:, pl_sl] + state_update(k, v, vec_ref[2, :, pl_sl])
            else:
                store_state(0, p, state_update(k, v, vec_ref[2, :, pl_sl]))
                store_state(1, p, state_update(k, v, vec_ref[3, :, pl_sl]))
            inv = 1.0 / HEAD_DIM
            mu = jnp.where(low, jnp.sum(jnp.where(low, o, 0.0), -1, keepdims=True),
                           jnp.sum(jnp.where(low, 0.0, o), -1, keepdims=True)) * inv
            dlt = o - mu
            sq = dlt * dlt
            var = jnp.where(low, jnp.sum(jnp.where(low, sq, 0.0), -1, keepdims=True),
                            jnp.sum(jnp.where(low, 0.0, sq), -1, keepdims=True)) * inv
            gate = zr_ref[rows, 3 * GROUP_W + p * LANES:3 * GROUP_W + (p + 1) * LANES]
            y = dlt * lax.rsqrt(var + EPS) * gng_ref[:, pl_sl] * (gate * _sigmoid(gate))
            y_ref[rows, pl_sl] = y.astype(BF16)

    x = zl_ref[:, 0:GROUP_W]
    zero8 = jnp.zeros((SUBLANES, GROUP_W), F32)
    pad_ref[0:SUBLANES, :] = zero8
    pad_ref[SUBLANES:SUBLANES + seq_len, :] = x
    pad_ref[SUBLANES + seq_len:2 * SUBLANES + seq_len, :] = zero8
    cw = cw_ref[...]
    xc = (cw[0:1] * pad_ref[SUBLANES - 2:SUBLANES - 2 + seq_len, :]
          + cw[1:2] * pad_ref[SUBLANES - 1:SUBLANES - 1 + seq_len, :]
          + cw[2:3] * x
          + cw[3:4] * pad_ref[SUBLANES + 1:SUBLANES + 1 + seq_len, :]
          + cb_ref[...])
    pre = _dot(xc.astype(BF16), wbd_ref[...]) + lb_ref[...]
    for d in range(2):
        r_gate = _sigmoid(pre[:, (2 * d) * GROUP_W:(2 * d + 1) * GROUP_W])
        i_gate = _sigmoid(pre[:, (2 * d + 1) * GROUP_W:(2 * d + 2) * GROUP_W])
        log_a = -LRU_C * r_gate * jax.nn.softplus(-lam_ref[d:d + 1, :])
        a_val = jnp.exp(log_a)
        u_val = jnp.sqrt(-jnp.tanh(log_a) * (a_val * a_val + 1.0)) * (i_gate * xc)
        for hh in range(2):
            a_ref[d, hh] = a_val[:, hh * LANES:(hh + 1) * LANES]
            u_ref[d, hh] = u_val[:, hh * LANES:(hh + 1) * LANES]

    def slab(ref, d, r):
        return jnp.concatenate([ref[d, hh, pl.ds(r, n0, stride=SUBLANES), :] for hh in range(2)], axis=1)

    finals = []
    for d in range(2):
        a_slabs = [slab(a_ref, d, r) for r in range(SUBLANES)]
        u_slabs = [slab(u_ref, d, r) for r in range(SUBLANES)]
        h0 = h0_ref[d:d + 1, :] if has_state else jnp.zeros((1, GROUP_W), F32)
        h_slabs, fin = _scan_levels(a_slabs, u_slabs, pa_ref, pb_ref, sa_ref, sb_ref, h0, n0, reverse=(d == 1))
        finals.append(fin)
        for r in range(SUBLANES):
            if d == 0:
                hf_ref[r] = h_slabs[r]
            else:
                h_sum = hf_ref[r] + h_slabs[r]
                for hh in range(2):
                    hs_ref[hh, pl.ds(r, n0, stride=SUBLANES), :] = h_sum[:, hh * LANES:(hh + 1) * LANES]
    h_both = jnp.concatenate([hs_ref[0], hs_ref[1]], axis=1)
    y = h_both * jax.nn.gelu(zl_ref[:, GROUP_W:2 * GROUP_W])
    y_ref[:, GROUP_W:2 * GROUP_W] = y.astype(BF16)
    if not has_state:
        slru_ref[0:1, :] = finals[0]
        slru_ref[1:2, :] = finals[1]


def _recurrent(scal, zr, zl, lp, l, seq_len, state=None):
    t = zr.shape[0]
    b = t // seq_len
    n0 = seq_len // SUBLANES
    has_state = state is not None
    assert has_state or seq_len == RET_CHUNK
    zr3 = zr.reshape(b, seq_len, 4 * GROUP_W)
    zl3 = zl.reshape(b, seq_len, 2 * GROUP_W)
    full = lambda a: pl.BlockSpec(a.shape, lambda i: (0,) * a.ndim)
    in_specs = [
        pl.BlockSpec(memory_space=pltpu.SMEM),
        pl.BlockSpec((None, seq_len, 4 * GROUP_W), lambda i: (i, 0, 0)),
        pl.BlockSpec((None, seq_len, 2 * GROUP_W), lambda i: (i, 0, 0)),
    ]
    args = [scal, zr3, zl3]
    if has_state:
        s0, h0 = state
        heads = GROUP_W // HEAD_DIM
        in_specs += [pl.BlockSpec((None, None, 2, heads, HEAD_DIM, HEAD_DIM), lambda i: (i, l, 0, 0, 0, 0)),
                     pl.BlockSpec((None, None, 2, GROUP_W), lambda i: (i, l, 0, 0))]
        args += [s0, h0]
    params = [lp["lg_lane"], lp["gn_g"], lp["conv_w"], lp["conv_b"], lp["lru_wbd"], lp["lru_bias"], lp["lru_lam"]]
    in_specs += [full(a) for a in params]
    args += params
    out_specs = [pl.BlockSpec((None, seq_len, 2 * GROUP_W), lambda i: (i, 0, 0))]
    out_shape = [jax.ShapeDtypeStruct((b, seq_len, 2 * GROUP_W), BF16)]
    if not has_state:
        heads = GROUP_W // HEAD_DIM
        out_specs += [pl.BlockSpec((None, 2, heads, HEAD_DIM, HEAD_DIM), lambda i: (i, 0, 0, 0, 0)),
                      pl.BlockSpec((None, 2, GROUP_W), lambda i: (i, 0, 0))]
        out_shape += [jax.ShapeDtypeStruct((b, 2, heads, HEAD_DIM, HEAD_DIM), F32),
                      jax.ShapeDtypeStruct((b, 2, GROUP_W), F32)]
    scratch = [
        pltpu.VMEM((GROUP_W // HEAD_DIM, RET_CHUNK, RET_CHUNK), F32),
        pltpu.VMEM((4, RET_CHUNK, GROUP_W), F32),
        pltpu.VMEM((seq_len + 2 * SUBLANES, GROUP_W), F32),
        pltpu.VMEM((2, 2, seq_len, LANES), F32),
        pltpu.VMEM((2, 2, seq_len, LANES), F32),
        pltpu.VMEM((SUBLANES, n0, GROUP_W), F32),
        pltpu.VMEM((2, seq_len, LANES), F32),
        pltpu.VMEM((SUBLANES, n0, GROUP_W), F32),
        pltpu.VMEM((SUBLANES, n0, GROUP_W), F32),
        pltpu.VMEM((2 * n0, GROUP_W), F32),
        pltpu.VMEM((2 * n0, GROUP_W), F32),
    ]
    outs = pl.pallas_call(
        functools.partial(_recurrent_kernel, seq_len=seq_len, has_state=has_state),
        grid=(b,),
        in_specs=in_specs,
        out_specs=out_specs,
        out_shape=out_shape,
        scratch_shapes=scratch,
        compiler_params=pltpu.CompilerParams(
            dimension_semantics=("arbitrary",), vmem_limit_bytes=VMEM_LIMIT),
        name="recurrent_latent" if has_state else "recurrent_ctx",
    )(*args)
    return (outs[0].reshape(t, 2 * GROUP_W),) + tuple(outs[1:])


def _dup_half(x, half):
    sw = pltpu.roll(x, HEAD_DIM, axis=1)
    low = _lane(x.shape) < HEAD_DIM
    return jnp.where(low, x, sw) if half == 0 else jnp.where(low, sw, x)


def _attention_kernel(*refs, seq_len, has_ctx, seqs_per_step):
    n_in = 10 if has_ctx else 6
    scal_ref, ng_ref = refs[0], refs[n_in - 1]
    per_seq = refs[1:n_in - 1] + refs[n_in:]
    for s in range(seqs_per_step):
        _attention_one(scal_ref, ng_ref, *[r.at[s] for r in per_seq], seq_len=seq_len, has_ctx=has_ctx)


def _attention_one(scal_ref, ng_ref, *refs, seq_len, has_ctx):
    if has_ctx:
        (sq_ref, skv_ref, dq_ref, dkv_ref, cks_ref, cvs_ref, ckd_ref, cvd_ref, y_ref,
         sk_s, sv_s, dk_s, dv_s, csk_s, csv_s, cdk_s, cdv_s) = refs
    else:
        (sq_ref, skv_ref, dq_ref, dkv_ref, y_ref, sk_s, sv_s, dk_s, dv_s) = refs
    tq = ATTN_Q_TILE
    heads = GROUP_W // HEAD_DIM
    lane = _lane((1, LANES))
    low = lane < HEAD_DIM
    halves = (low, ~low)

    def build():
        def fill(k_all, v_all, k_dst, v_dst, dup):
            for p in range(2):
                if dup:
                    k_p, v_p = _dup_half(k_all, p), _dup_half(v_all, p)
                else:
                    k_p, v_p = k_all[:, p * LANES:(p + 1) * LANES], v_all[:, p * LANES:(p + 1) * LANES]
                k_dst[p] = k_p.astype(BF16)
                for j in range(2):
                    v_dst[2 * p + j] = jnp.where(halves[j], v_p, 1.0).astype(BF16)

        fill(skv_ref[:, 2 * LANES:3 * LANES], skv_ref[:, 3 * LANES:4 * LANES], sk_s, sv_s, True)
        fill(dkv_ref[:, GROUP_W:2 * GROUP_W], dkv_ref[:, 2 * GROUP_W:3 * GROUP_W], dk_s, dv_s, False)
        if has_ctx:
            fill(cks_ref[...], cvs_ref[...], csk_s, csv_s, True)
            fill(ckd_ref[...], cvd_ref[...], cdk_s, cdv_s, False)

    if seq_len == tq:
        build()
    else:
        pl.when(pl.program_id(1) == 0)(build)

    def softmax_pv(q_rows, k_loc, k_ctx, v_of_head, valid, extra2):
        s = _dot_nt(q_rows, k_loc)
        if valid is not None:
            s = jnp.where(valid, s, NEG_BIG)
        m = jnp.max(s, -1, keepdims=True)
        if k_ctx is not None:
            sc = _dot_nt(q_rows, k_ctx)
            m = jnp.maximum(m, jnp.max(sc, -1, keepdims=True))
        if extra2 is not None:
            m = jnp.maximum(m, extra2)
        e = jnp.exp2((s - m).astype(BF16))
        if k_ctx is not None:
            ec = jnp.exp2((sc - m).astype(BF16))
        half_rows = q_rows.shape[0] // 2
        res = []
        for j in range(2):
            rs = slice(j * half_rows, (j + 1) * half_rows)
            v_loc, v_ctx = v_of_head(j)
            o = _dot(e[rs], v_loc)
            if k_ctx is not None:
                o = o + _dot(ec[rs], v_ctx)
            den = pltpu.roll(o, HEAD_DIM, axis=1)
            if extra2 is not None:
                den = den + jnp.exp2(extra2[rs] - m[rs])
            res.append(o / jnp.where(halves[j], den, 1.0))
        return res

    if has_ctx:
        win = 2 * tq
        i = pl.program_id(1)
        blk = jnp.clip(2 * i - 1, 0, (seq_len - win) // SWA_WINDOW)
        start = pl.multiple_of(blk * SWA_WINDOW, SWA_WINDOW)
        rows = pl.ds(start, win)
        qpos = i * tq + (lax.broadcasted_iota(jnp.int32, (2 * tq, win), 0) & (tq - 1))
        kpos = start + lax.broadcasted_iota(jnp.int32, (2 * tq, win), 1)
        valid = jnp.abs(kpos - qpos) <= SWA_WINDOW
    else:
        rows = slice(None)
        valid = None
    qscale = (HEAD_DIM ** -0.5) * LOG2E
    first_member = lax.broadcasted_iota(jnp.int32, (2 * tq, 1), 0) < tq
    for p in range(2):
        q = sq_ref[:, p * LANES:(p + 1) * LANES] * qscale
        q_rows = jnp.concatenate([jnp.where(halves[j], q, 0.0).astype(BF16) for j in range(2)], axis=0)
        sink2 = jnp.where(first_member, scal_ref[2 * heads + 2 * p], scal_ref[2 * heads + 2 * p + 1]) * LOG2E
        outs = softmax_pv(q_rows, sk_s[p, rows, :], csk_s[p] if has_ctx else None,
                          lambda j: (sv_s[2 * p + j, rows, :], csv_s[2 * p + j] if has_ctx else None),
                          valid, sink2)
        y_ref[:, p * LANES:(p + 1) * LANES] = jnp.where(low, outs[0], outs[1]).astype(BF16)

    lam = scal_ref[3 * heads]
    coef = scal_ref[3 * heads + 1]
    dscale = (DIFF_QK_DIM ** -0.5) * LOG2E
    grp = lane >> 5
    for p in range(2):
        q = dq_ref[:, p * LANES:(p + 1) * LANES] * dscale
        q_rows = jnp.concatenate([jnp.where(grp == g, q, 0.0).astype(BF16) for g in range(4)], axis=0)
        parts = softmax_pv(q_rows, dk_s[p], cdk_s[p] if has_ctx else None,
                           lambda j: (dv_s[2 * p + j], cdv_s[2 * p + j] if has_ctx else None),
                           None, None)
        outs = [r[0:tq] - lam * r[tq:2 * tq] for r in parts]
        o = jnp.where(low, outs[0], outs[1])
        sq = o * o
        ms = jnp.where(low, jnp.sum(jnp.where(low, sq, 0.0), -1, keepdims=True),
                       jnp.sum(jnp.where(low, 0.0, sq), -1, keepdims=True)) * (1.0 / HEAD_DIM)
        y = o * lax.rsqrt(ms + EPS) * ng_ref[:, p * LANES:(p + 1) * LANES] * coef
        y_ref[:, GROUP_W + p * LANES:GROUP_W + (p + 1) * LANES] = y.astype(BF16)


def _attention(scal, zs, zd, norm_g_tiled, l, seq_len, caches=None):
    t = zs.shape[0]
    b = t // seq_len
    tq = ATTN_Q_TILE
    nq = seq_len // tq
    has_ctx = caches is not None
    zs3 = zs.reshape(b, seq_len, 2 * GROUP_W)
    zd3 = zd.reshape(b, seq_len, 3 * GROUP_W)
    ns = 1 if has_ctx else CTX_SEQS_PER_STEP
    in_specs = [
        pl.BlockSpec(memory_space=pltpu.SMEM),
        pl.BlockSpec((ns, tq, 2 * GROUP_W), lambda i, j: (i, j, 0)),
        pl.BlockSpec((ns, seq_len, 2 * GROUP_W), lambda i, j: (i, 0, 0)),
        pl.BlockSpec((ns, tq, 3 * GROUP_W), lambda i, j: (i, j, 0)),
        pl.BlockSpec((ns, seq_len, 3 * GROUP_W), lambda i, j: (i, 0, 0)),
    ]
    args = [scal, zs3, zs3, zd3, zd3]
    heads = GROUP_W // HEAD_DIM
    scratch = [pltpu.VMEM((ns, 2, seq_len, LANES), BF16), pltpu.VMEM((ns, heads, seq_len, LANES), BF16),
               pltpu.VMEM((ns, 2, seq_len, LANES), BF16), pltpu.VMEM((ns, heads, seq_len, LANES), BF16)]
    if has_ctx:
        past = caches[0].shape[2]
        for a in caches:
            in_specs.append(pl.BlockSpec((ns, None) + a.shape[2:], lambda i, j: (i, l, 0, 0)))
            args.append(a)
        scratch += [pltpu.VMEM((ns, 2, past, LANES), BF16), pltpu.VMEM((ns, heads, past, LANES), BF16),
                    pltpu.VMEM((ns, 2, past, LANES), BF16), pltpu.VMEM((ns, heads, past, LANES), BF16)]
    in_specs.append(pl.BlockSpec(norm_g_tiled.shape, lambda i, j: (0, 0)))
    args.append(norm_g_tiled)
    y = pl.pallas_call(
        functools.partial(_attention_kernel, seq_len=seq_len, has_ctx=has_ctx, seqs_per_step=ns),
        grid=(b // ns, nq),
        in_specs=in_specs,
        out_specs=pl.BlockSpec((ns, tq, 2 * GROUP_W), lambda i, j: (i, j, 0)),
        out_shape=jax.ShapeDtypeStruct((b, seq_len, 2 * GROUP_W), BF16),
        scratch_shapes=scratch,
        compiler_params=pltpu.CompilerParams(
            dimension_semantics=("arbitrary", "arbitrary"), vmem_limit_bytes=VMEM_LIMIT),
        name="attention_latent" if has_ctx else "attention_ctx",
    )(*args)
    return y.reshape(t, 2 * GROUP_W)


def _out_mlp_kernel(x_ref, yr_ref, ya_ref, mod_ref, wo_ref, g_ref, w1_ref, w2_ref, gf_ref, o_ref, *, final):
    g1 = mod_ref[:, 2 * D_MODEL:3 * D_MODEL]
    sh2 = mod_ref[:, 3 * D_MODEL:4 * D_MODEL]
    sc2 = mod_ref[:, 4 * D_MODEL:5 * D_MODEL]
    g2 = mod_ref[:, 5 * D_MODEL:6 * D_MODEL]
    y = _dot(yr_ref[...], wo_ref[0:2 * GROUP_W, :]) + _dot(ya_ref[...], wo_ref[2 * GROUP_W:4 * GROUP_W, :])
    x = x_ref[...] + g1 * y
    h = (_rms(x) * g_ref[...] * (1.0 + sc2) + sh2).astype(BF16)
    acc = None
    for c in range(D_FF // D_MODEL):
        sl = slice(c * D_MODEL, (c + 1) * D_MODEL)
        hid = jnp.square(jnp.maximum(_dot(h, w1_ref[:, sl]), 0.0)).astype(BF16)
        part = _dot(hid, w2_ref[sl, :])
        acc = part if acc is None else acc + part
    x = x + g2 * acc
    if final:
        x = _rms(x) * gf_ref[...]
    o_ref[...] = x


def _out_mlp(x, y_rec, y_att, mod4, w_out_b, norm_g, w_ff1_b, w_ff2_b, final_g, l, seq_len, is_latent, final):
    t = x.shape[0]
    tm = ROW_TILE
    if is_latent:
        tiles_per_seq = seq_len // tm
        mod_map = lambda i: (l, 1 + i // tiles_per_seq, 0, 0)
    else:
        mod_map = lambda i: (l, 0, 0, 0)
    once = pl.Buffered(1)
    in_specs = [
        pl.BlockSpec((tm, D_MODEL), lambda i: (i, 0)),
        pl.BlockSpec((tm, 2 * GROUP_W), lambda i: (i, 0)),
        pl.BlockSpec((tm, 2 * GROUP_W), lambda i: (i, 0)),
        pl.BlockSpec((None, None, 1, N_MOD * D_MODEL), mod_map),
        pl.BlockSpec((None, D_MODEL, D_MODEL), lambda i: (l, 0, 0), pipeline_mode=once),
        pl.BlockSpec((None, 1, D_MODEL), lambda i: (l, 0, 0)),
        pl.BlockSpec((None, D_MODEL, D_FF), lambda i: (l, 0, 0), pipeline_mode=once),
        pl.BlockSpec((None, D_FF, D_MODEL), lambda i: (l, 0, 0), pipeline_mode=once),
        pl.BlockSpec((1, D_MODEL), lambda i: (0, 0)),
    ]
    return pl.pallas_call(
        functools.partial(_out_mlp_kernel, final=final),
        grid=(t // tm,),
        in_specs=in_specs,
        out_specs=pl.BlockSpec((tm, D_MODEL), lambda i: (i, 0)),
        out_shape=jax.ShapeDtypeStruct((t, D_MODEL), F32),
        compiler_params=pltpu.CompilerParams(
            dimension_semantics=("arbitrary",), vmem_limit_bytes=VMEM_LIMIT),
        name="out_mlp_final" if final else "out_mlp",
    )(x, y_rec, y_att, mod4, w_out_b, norm_g, w_ff1_b, w_ff2_b, final_g)


def _rope_tables(length):
    row = (jnp.arange(length) // GRID_W).astype(F32)
    col = (jnp.arange(length) % GRID_W).astype(F32)
    tabs = []
    for dim in (HEAD_DIM, DIFF_QK_DIM):
        n = dim // 4
        inv = ROPE_BASE ** (-jnp.arange(n, dtype=F32) / n)
        ang = jnp.concatenate([row[:, None] * inv, col[:, None] * inv], -1)
        cos, sin = jnp.cos(ang), jnp.sin(ang)
        reps = LANES // dim
        tabs.append(jnp.tile(jnp.concatenate([cos, cos], -1), (1, reps)))
        tabs.append(jnp.tile(jnp.concatenate([-sin, sin], -1), (1, reps)))
    return tabs


def _block_diag(w):
    n, c, d = w.shape
    eye = jnp.eye(n, dtype=w.dtype)
    return jnp.einsum("ncd,nm->ncmd", w, eye).reshape(n * c, n * d)


def _layer_params(l, ret_decay, ret_gn_g, lru_conv_w, lru_conv_b, lru_w_a, lru_b_a, lru_w_x, lru_b_x, lru_lambda,
                  swa_sink, diff_lambda, diff_norm_g):
    heads = GROUP_W // HEAD_DIM
    log_gamma = jax.nn.log_sigmoid(ret_decay[l].astype(F32))
    lambda_init = 0.8 - 0.6 * math.exp(-0.3 * l)
    lv = diff_lambda[l].astype(F32)
    lam = jnp.exp(jnp.sum(lv[0] * lv[1])) - jnp.exp(jnp.sum(lv[2] * lv[3])) + lambda_init
    scal = jnp.concatenate([log_gamma[0], log_gamma[1], swa_sink[l].astype(F32),
                            jnp.stack([lam, jnp.asarray(1.0 - lambda_init, F32)]),
                            jnp.zeros((2,), F32)])
    wbd = jnp.concatenate([_block_diag(lru_w_a[l, 0]), _block_diag(lru_w_x[l, 0]),
                           _block_diag(lru_w_a[l, 1]), _block_diag(lru_w_x[l, 1])], axis=1).astype(BF16)
    bias = jnp.concatenate([lru_b_a[l, 0], lru_b_x[l, 0], lru_b_a[l, 1], lru_b_x[l, 1]])[None, :]
    return dict(
        scal=scal,
        lg_lane=jnp.repeat(log_gamma, HEAD_DIM, axis=1),
        gn_g=ret_gn_g[l][None, :],
        conv_w=lru_conv_w[l],
        conv_b=lru_conv_b[l][None, :],
        lru_wbd=wbd,
        lru_bias=bias,
        lru_lam=lru_lambda[l],
        diff_ng=jnp.tile(diff_norm_g[l], heads)[None, :],
    )


def kernel(x_prompt, x_sample, c, state_ret, state_lru, cache_swa_k, cache_swa_v, cache_diff_k, cache_diff_v,
           c_ctx, w_ada, b_ada, norm_mix_g, w_in, ret_decay, ret_gn_g, lru_conv_w, lru_conv_b,
           lru_w_a, lru_b_a, lru_w_x, lru_b_x, lru_lambda, swa_sink, diff_lambda, diff_norm_g,
           w_out, norm_mlp_g, w_ff1, w_ff2, final_norm_g):
    batch, seq, _ = x_prompt.shape
    dec_batch, dec_seq, _ = x_sample.shape
    past = cache_swa_k.shape[2]

    cvec = jnp.concatenate([c_ctx[None, :], c, jnp.zeros((MOD_ROWS - 1 - dec_batch, D_MODEL), F32)], axis=0)
    mod4 = _modulation(cvec, w_ada, b_ada).reshape(DEPTH, MOD_ROWS, 1, N_MOD * D_MODEL)

    w_in_b = w_in.astype(BF16)
    w_out_b = w_out.astype(BF16)
    w_ff1_b = w_ff1.astype(BF16)
    w_ff2_b = w_ff2.astype(BF16)
    norm_mix3 = norm_mix_g.reshape(DEPTH, 1, D_MODEL)
    norm_mlp3 = norm_mlp_g.reshape(DEPTH, 1, D_MODEL)
    final_g = final_norm_g[None, :]
    rope_tabs = _rope_tables(dec_seq)
    layers = [_layer_params(l, ret_decay, ret_gn_g, lru_conv_w, lru_conv_b, lru_w_a, lru_b_a, lru_w_x, lru_b_x,
                            lru_lambda, swa_sink, diff_lambda, diff_norm_g) for l in range(DEPTH)]

    heads = GROUP_W // HEAD_DIM

    xp = x_prompt.reshape(batch * seq, D_MODEL)
    new_ret, new_lru, new_swa, new_diff = [], [], [], []
    for l in range(DEPTH):
        lp = layers[l]
        zr, zl, zs, zd = _in_proj(xp, mod4, norm_mix3, w_in_b, l, seq, None)
        y_rec, s_ret, s_lru = _recurrent(lp["scal"], zr, zl, lp, l, seq)
        y_att = _attention(lp["scal"], zs, zd, lp["diff_ng"], l, seq)
        xp = _out_mlp(xp, y_rec, y_att, mod4, w_out_b, norm_mlp3, w_ff1_b, w_ff2_b, final_g,
                      l, seq, False, l == DEPTH - 1)
        new_ret.append(s_ret)
        new_lru.append(s_lru)
        new_swa.append(zs[:, GROUP_W:2 * GROUP_W])
        new_diff.append(zd[:, GROUP_W:3 * GROUP_W])
    y_prompt = xp.reshape(batch, seq, D_MODEL)
    new_ret = jnp.stack(new_ret, axis=1)
    new_swa = jnp.stack(new_swa, axis=1).reshape(batch, seq, DEPTH, 2, 2, HEAD_DIM)
    new_swa = jnp.transpose(new_swa, (3, 0, 2, 1, 4, 5))
    new_diff = jnp.stack(new_diff, axis=1).reshape(batch, seq, DEPTH, 2, heads, HEAD_DIM)
    new_diff = jnp.transpose(new_diff, (3, 0, 2, 1, 4, 5))

    xs = x_sample.reshape(dec_batch * dec_seq, D_MODEL)
    state = (state_ret, state_lru)
    caches = (cache_swa_k.reshape(dec_batch, DEPTH, past, 2 * HEAD_DIM),
              cache_swa_v.reshape(dec_batch, DEPTH, past, 2 * HEAD_DIM),
              cache_diff_k.reshape(dec_batch, DEPTH, past, GROUP_W),
              cache_diff_v.reshape(dec_batch, DEPTH, past, GROUP_W))
    for l in range(DEPTH):
        lp = layers[l]
        zr, zl, zs, zd = _in_proj(xs, mod4, norm_mix3, w_in_b, l, dec_seq, rope_tabs)
        (y_rec,) = _recurrent(lp["scal"], zr, zl, lp, l, dec_seq, state)
        y_att = _attention(lp["scal"], zs, zd, lp["diff_ng"], l, dec_seq, caches)
        xs = _out_mlp(xs, y_rec, y_att, mod4, w_out_b, norm_mlp3, w_ff1_b, w_ff2_b, final_g,
                      l, dec_seq, True, l == DEPTH - 1)
    y_sample = xs.reshape(dec_batch, dec_seq, D_MODEL)

    return (y_prompt, y_sample, new_ret, jnp.stack(new_lru, axis=1),
            new_swa[0], new_swa[1], new_diff[0], new_diff[1])
```

```python
import functools
import math

import jax
import jax.numpy as jnp
from jax import lax
from jax.experimental import pallas as pl
from jax.experimental.pallas import tpu as pltpu

F32 = jnp.float32
BF16 = jnp.bfloat16

D_MODEL = 1024
DEPTH = 4
GRID_W = 64
HEAD_DIM = 64
GROUP_W = D_MODEL // 4
RET_CHUNK = 256
LRU_BLOCKS = 4
LRU_C = 8.0
SWA_WINDOW = 128
DIFF_QK_DIM = HEAD_DIM // 2
D_FF = 4 * D_MODEL
ROPE_BASE = 10000.0
EPS = 1e-6
N_MOD = 6
D_IN = 11 * GROUP_W
MOD_ROWS = 16

LANES = 128
SUBLANES = 8
VMEM_LIMIT = 56 * 1024 * 1024
NEG_BIG = -1e30
LOG2E = math.log2(math.e)

ROW_TILE = 512
ATTN_Q_TILE = 256
CTX_SEQS_PER_STEP = 2
ATTN_LOOKAHEAD_LATENT = 2
ATTN_LOOKAHEAD_CTX = 1


def _dot(a, b):
    return jnp.dot(a, b, preferred_element_type=F32)


def _dot_nt(a, b):
    return lax.dot_general(a, b, (((1,), (1,)), ((), ())), preferred_element_type=F32)


def _dot_tn(a, b):
    return lax.dot_general(a, b, (((0,), (0,)), ((), ())), preferred_element_type=F32)


def _rms(x):
    return x * lax.rsqrt(jnp.mean(x * x, axis=-1, keepdims=True) + EPS)


def _sigmoid(x):
    return 0.5 * jnp.tanh(0.5 * x) + 0.5


def _lane(shape):
    return lax.broadcasted_iota(jnp.int32, shape, len(shape) - 1)


def _mod_kernel(c_ref, w_ref, b_ref, o_ref):
    s = jax.nn.silu(c_ref[...]).astype(BF16)
    o_ref[...] = _dot(s, w_ref[...].astype(BF16)) + b_ref[...]


def _modulation(cvec, w_ada, b_ada):
    tn = 1536
    n_mod = N_MOD * D_MODEL
    return pl.pallas_call(
        _mod_kernel,
        grid=(DEPTH, n_mod // tn),
        in_specs=[
            pl.BlockSpec((MOD_ROWS, D_MODEL), lambda l, j: (0, 0)),
            pl.BlockSpec((None, D_MODEL, tn), lambda l, j: (l, 0, j)),
            pl.BlockSpec((None, 1, tn), lambda l, j: (l, 0, j)),
        ],
        out_specs=pl.BlockSpec((None, MOD_ROWS, tn), lambda l, j: (l, 0, j)),
        out_shape=jax.ShapeDtypeStruct((DEPTH, MOD_ROWS, n_mod), F32),
        compiler_params=pltpu.CompilerParams(
            dimension_semantics=("arbitrary", "arbitrary"), vmem_limit_bytes=VMEM_LIMIT),
        name="modulation",
    )(cvec, w_ada, b_ada.reshape(DEPTH, 1, n_mod))


def _rope_slab(x, c, s, half):
    first = (_lane(x.shape) & (2 * half - 1)) < half
    swapped = jnp.where(first, pltpu.roll(x, LANES - half, axis=1), pltpu.roll(x, half, axis=1))
    return x * c + swapped * s


def _in_proj_kernel(*refs, rope):
    if rope:
        x_ref, mod_ref, g_ref, w_ref, c64_ref, s64_ref, c32_ref, s32_ref, zr_ref, zl_ref, zs_ref, zd_ref = refs
    else:
        x_ref, mod_ref, g_ref, w_ref, zr_ref, zl_ref, zs_ref, zd_ref = refs
    sh1 = mod_ref[:, 0:D_MODEL]
    sc1 = mod_ref[:, D_MODEL:2 * D_MODEL]
    h = (_rms(x_ref[...]) * g_ref[...] * (1.0 + sc1) + sh1).astype(BF16)
    zr_ref[...] = _dot(h, w_ref[:, 0:4 * GROUP_W])
    zl_ref[...] = _dot(h, w_ref[:, 4 * GROUP_W:6 * GROUP_W])
    zs = _dot(h, w_ref[:, 6 * GROUP_W:8 * GROUP_W])
    zd = _dot(h, w_ref[:, 8 * GROUP_W:11 * GROUP_W])
    if rope:
        c64, s64, c32, s32 = c64_ref[...], s64_ref[...], c32_ref[...], s32_ref[...]
        for i in range(3):
            sl = slice(i * LANES, (i + 1) * LANES)
            zs_ref[:, sl] = _rope_slab(zs[:, sl], c64, s64, HEAD_DIM // 2)
        zs_ref[:, 3 * LANES:4 * LANES] = zs[:, 3 * LANES:4 * LANES]
        for i in range(4):
            sl = slice(i * LANES, (i + 1) * LANES)
            zd_ref[:, sl] = _rope_slab(zd[:, sl], c32, s32, DIFF_QK_DIM // 2)
        zd_ref[:, 4 * LANES:6 * LANES] = zd[:, 4 * LANES:6 * LANES]
    else:
        zs_ref[...] = zs
        zd_ref[...] = zd


def _in_proj(x, mod4, norm_g, w_in_b, l, seq_len, rope_tabs):
    t = x.shape[0]
    tm = ROW_TILE
    rope = rope_tabs is not None
    if rope:
        tiles_per_seq = seq_len // tm
        mod_map = lambda i: (l, 1 + i // tiles_per_seq, 0, 0)
    else:
        mod_map = lambda i: (l, 0, 0, 0)
    in_specs = [
        pl.BlockSpec((tm, D_MODEL), lambda i: (i, 0)),
        pl.BlockSpec((None, None, 1, N_MOD * D_MODEL), mod_map),
        pl.BlockSpec((None, 1, D_MODEL), lambda i: (l, 0, 0)),
        pl.BlockSpec((None, D_MODEL, D_IN), lambda i: (l, 0, 0)),
    ]
    args = [x, mod4, norm_g, w_in_b]
    if rope:
        tab_spec = pl.BlockSpec((tm, LANES), lambda i: (i % tiles_per_seq, 0))
        in_specs += [tab_spec] * 4
        args += list(rope_tabs)
    widths = (4 * GROUP_W, 2 * GROUP_W, 2 * GROUP_W, 3 * GROUP_W)
    return pl.pallas_call(
        functools.partial(_in_proj_kernel, rope=rope),
        grid=(t // tm,),
        in_specs=in_specs,
        out_specs=[pl.BlockSpec((tm, w), lambda i: (i, 0)) for w in widths],
        out_shape=[jax.ShapeDtypeStruct((t, w), F32) for w in widths],
        compiler_params=pltpu.CompilerParams(
            dimension_semantics=("arbitrary",), vmem_limit_bytes=VMEM_LIMIT),
        name="in_proj_rope" if rope else "in_proj",
    )(*args)


def _scan_levels(a_slabs, u_slabs, pa_ref, pb_ref, sa_ref, sb_ref, h0, n0, reverse):
    order = list(range(SUBLANES - 1, -1, -1)) if reverse else list(range(SUBLANES))
    acc_a = acc_b = None
    for r in order:
        if acc_a is None:
            acc_a, acc_b = a_slabs[r], u_slabs[r]
        else:
            acc_b = a_slabs[r] * acc_b + u_slabs[r]
            acc_a = a_slabs[r] * acc_a
        pa_ref[r] = acc_a
        pb_ref[r] = acc_b
    pad = n0 // 2
    ident = pad + n0 if reverse else 0
    sa_ref[ident:ident + pad, :] = jnp.ones((pad, GROUP_W), F32)
    sb_ref[ident:ident + pad, :] = jnp.zeros((pad, GROUP_W), F32)
    sign = 1 if reverse else -1
    k = 1
    while k < n0:
        sa_ref[pad:pad + n0, :] = acc_a
        sb_ref[pad:pad + n0, :] = acc_b
        off = pad + sign * k
        acc_b = acc_a * sb_ref[off:off + n0, :] + acc_b
        acc_a = acc_a * sa_ref[off:off + n0, :]
        k *= 2
    sa_ref[pad:pad + n0, :] = acc_a
    sb_ref[pad:pad + n0, :] = acc_b
    off = pad + sign
    carry_in = sa_ref[off:off + n0, :] * h0 + sb_ref[off:off + n0, :]
    end = 0 if reverse else n0 - 1
    final = acc_a[end:end + 1, :] * h0 + acc_b[end:end + 1, :]
    h_slabs = [pa_ref[r] * carry_in + pb_ref[r] for r in range(SUBLANES)]
    return h_slabs, final


def _recurrent_kernel(*refs, seq_len, has_state):
    if has_state:
        (scal_ref, zr_ref, zl_ref, s0_ref, h0_ref, lgl_ref, gng_ref, cw_ref, cb_ref, wbd_ref, lb_ref, lam_ref,
         y_ref,
         d_ref, vec_ref, pad_ref, a_ref, u_ref, hf_ref, hs_ref, pa_ref, pb_ref, sa_ref, sb_ref) = refs
    else:
        (scal_ref, zr_ref, zl_ref, lgl_ref, gng_ref, cw_ref, cb_ref, wbd_ref, lb_ref, lam_ref,
         y_ref, sret_ref, slru_ref,
         d_ref, vec_ref, pad_ref, a_ref, u_ref, hf_ref, hs_ref, pa_ref, pb_ref, sa_ref, sb_ref) = refs
    ch = RET_CHUNK
    n_chunks = seq_len // ch
    n0 = seq_len // SUBLANES
    heads = GROUP_W // HEAD_DIM

    @pl.when(pl.program_id(0) == 0)
    def _():
        n = lax.broadcasted_iota(jnp.int32, (ch, ch), 0)
        m = lax.broadcasted_iota(jnp.int32, (ch, ch), 1)
        dist = (n - m).astype(F32)
        for h in range(heads):
            lg = jnp.where(dist > 0, scal_ref[h], scal_ref[heads + h])
            d_ref[h] = jnp.where(dist == 0, 2.0, jnp.exp(jnp.abs(dist) * lg))
        t = lax.broadcasted_iota(jnp.int32, (ch, GROUP_W), 0).astype(F32)
        lgf = lgl_ref[0:1, :]
        lgb = lgl_ref[1:2, :]
        vec_ref[0] = jnp.exp((t + 1.0) * lgf)
        vec_ref[1] = jnp.exp((ch - t) * lgb)
        vec_ref[2] = jnp.exp((ch - 1.0 - t) * lgf)
        vec_ref[3] = jnp.exp(t * lgb)

    low = _lane((1, LANES)) < HEAD_DIM
    rr = lax.broadcasted_iota(jnp.int32, (LANES, LANES), 0) < HEAD_DIM
    cc = lax.broadcasted_iota(jnp.int32, (LANES, LANES), 1) < HEAD_DIM
    block_diag = (rr == cc).astype(F32)
    g_chunk_f = jnp.exp(ch * lgl_ref[0:1, :])
    g_chunk_b = jnp.exp(ch * lgl_ref[1:2, :])

    def kv(c, p):
        rows = slice(c * ch, (c + 1) * ch)
        k = zr_ref[rows, GROUP_W + p * LANES:GROUP_W + (p + 1) * LANES] * (HEAD_DIM ** -0.5)
        v = zr_ref[rows, 2 * GROUP_W + p * LANES:2 * GROUP_W + (p + 1) * LANES].astype(BF16)
        return k, v

    def state_update(k, v, zeta):
        return _dot_tn((k * zeta).astype(BF16), v) * block_diag

    def pair_state(d, p):
        zero = jnp.zeros((HEAD_DIM, HEAD_DIM), F32)
        top = jnp.concatenate([s0_ref[d, 2 * p], zero], axis=1)
        bottom = jnp.concatenate([zero, s0_ref[d, 2 * p + 1]], axis=1)
        return jnp.concatenate([top, bottom], axis=0)

    def store_state(d, p, s_pair):
        sret_ref[d, 2 * p] = s_pair[0:HEAD_DIM, 0:HEAD_DIM]
        sret_ref[d, 2 * p + 1] = s_pair[HEAD_DIM:LANES, HEAD_DIM:LANES]

    for p in range(2):
        pl_sl = slice(p * LANES, (p + 1) * LANES)
        if has_state:
            s_b = [None] * n_chunks
            s_b[n_chunks - 1] = pair_state(1, p)
            for c in range(n_chunks - 1, 0, -1):
                k, v = kv(c, p)
                s_b[c - 1] = s_b[c] * g_chunk_b[:, pl_sl] + state_update(k, v, vec_ref[3, :, pl_sl])
            s_f = pair_state(0, p)
        for c in range(n_chunks):
            rows = slice(c * ch, (c + 1) * ch)
            q = zr_ref[rows, pl_sl]
            k, v = kv(c, p)
            kb = k.astype(BF16)
            outs = []
            for j in range(2):
                qm = jnp.where(low if j == 0 else ~low, q, 0.0).astype(BF16)
                w = (_dot_nt(qm, kb) * d_ref[2 * p + j]).astype(BF16)
                outs.append(_dot(w, v))
            o = jnp.where(low, outs[0], outs[1])
            if has_state:
                qb = q.astype(BF16)
                o = o + _dot(qb, s_f.astype(BF16)) * vec_ref[0, :, pl_sl]
                o = o + _dot(qb, s_b[c].astype(BF16)) * vec_ref[1, :, pl_sl]
                if c + 1 < n_chunks:
                    s_f = s_f * g_chunk_f[:, pl_sl] + state_update(k, v, vec_ref[2, :, pl_sl])
            else:
                store_state(0, p, state_update(k, v, vec_ref[2, :, pl_sl]))
                store_state(1, p, state_update(k, v, vec_ref[3, :, pl_sl]))
            inv = 1.0 / HEAD_DIM
            mu = jnp.where(low, jnp.sum(jnp.where(low, o, 0.0), -1, keepdims=True),
                           jnp.sum(jnp.where(low, 0.0, o), -1, keepdims=True)) * inv
            dlt = o - mu
            sq = dlt * dlt
            var = jnp.where(low, jnp.sum(jnp.where(low, sq, 0.0), -1, keepdims=True),
                            jnp.sum(jnp.where(low, 0.0, sq), -1, keepdims=True)) * inv
            gate = zr_ref[rows, 3 * GROUP_W + p * LANES:3 * GROUP_W + (p + 1) * LANES]
            y = dlt * lax.rsqrt(var + EPS) * gng_ref[:, pl_sl] * (gate * _sigmoid(gate))
            y_ref[rows, pl_sl] = y.astype(BF16)

    x = zl_ref[:, 0:GROUP_W]
    zero8 = jnp.zeros((SUBLANES, GROUP_W), F32)
    pad_ref[0:SUBLANES, :] = zero8
    pad_ref[SUBLANES:SUBLANES + seq_len, :] = x
    pad_ref[SUBLANES + seq_len:2 * SUBLANES + seq_len, :] = zero8
    cw = cw_ref[...]
    xc = (cw[0:1] * pad_ref[SUBLANES - 2:SUBLANES - 2 + seq_len, :]
          + cw[1:2] * pad_ref[SUBLANES - 1:SUBLANES - 1 + seq_len, :]
          + cw[2:3] * x
          + cw[3:4] * pad_ref[SUBLANES + 1:SUBLANES + 1 + seq_len, :]
          + cb_ref[...])
    pre = _dot(xc.astype(BF16), wbd_ref[...]) + lb_ref[...]
    for d in range(2):
        r_gate = _sigmoid(pre[:, (2 * d) * GROUP_W:(2 * d + 1) * GROUP_W])
        i_gate = _sigmoid(pre[:, (2 * d + 1) * GROUP_W:(2 * d + 2) * GROUP_W])
        log_a = -LRU_C * r_gate * jax.nn.softplus(-lam_ref[d:d + 1, :])
        a_val = jnp.exp(log_a)
        u_val = jnp.sqrt(-jnp.tanh(log_a) * (a_val * a_val + 1.0)) * (i_gate * xc)
        for hh in range(2):
            a_ref[d, hh] = a_val[:, hh * LANES:(hh + 1) * LANES]
            u_ref[d, hh] = u_val[:, hh * LANES:(hh + 1) * LANES]

    def slab(ref, d, r):
        return jnp.concatenate([ref[d, hh, pl.ds(r, n0, stride=SUBLANES), :] for hh in range(2)], axis=1)

    finals = []
    for d in range(2):
        a_slabs = [slab(a_ref, d, r) for r in range(SUBLANES)]
        u_slabs = [slab(u_ref, d, r) for r in range(SUBLANES)]
        h0 = h0_ref[d:d + 1, :] if has_state else jnp.zeros((1, GROUP_W), F32)
        h_slabs, fin = _scan_levels(a_slabs, u_slabs, pa_ref, pb_ref, sa_ref, sb_ref, h0, n0, reverse=(d == 1))
        finals.append(fin)
        for r in range(SUBLANES):
            if d == 0:
                hf_ref[r] = h_slabs[r]
            else:
                h_sum = hf_ref[r] + h_slabs[r]
                for hh in range(2):
                    hs_ref[hh, pl.ds(r, n0, stride=SUBLANES), :] = h_sum[:, hh * LANES:(hh + 1) * LANES]
    h_both = jnp.concatenate([hs_ref[0], hs_ref[1]], axis=1)
    y = h_both * jax.nn.gelu(zl_ref[:, GROUP_W:2 * GROUP_W])
    y_ref[:, GROUP_W:2 * GROUP_W] = y.astype(BF16)
    if not has_state:
        slru_ref[0:1, :] = finals[0]
        slru_ref[1:2, :] = finals[1]


def _recurrent(scal, zr, zl, lp, l, seq_len, state=None):
    t = zr.shape[0]
    b = t // seq_len
    n0 = seq_len // SUBLANES
    has_state = state is not None
    assert has_state or seq_len == RET_CHUNK
    zr3 = zr.reshape(b, seq_len, 4 * GROUP_W)
    zl3 = zl.reshape(b, seq_len, 2 * GROUP_W)
    full = lambda a: pl.BlockSpec(a.shape, lambda i: (0,) * a.ndim)
    in_specs = [
        pl.BlockSpec(memory_space=pltpu.SMEM),
        pl.BlockSpec((None, seq_len, 4 * GROUP_W), lambda i: (i, 0, 0)),
        pl.BlockSpec((None, seq_len, 2 * GROUP_W), lambda i: (i, 0, 0)),
    ]
    args = [scal, zr3, zl3]
    if has_state:
        s0, h0 = state
        heads = GROUP_W // HEAD_DIM
        in_specs += [pl.BlockSpec((None, None, 2, heads, HEAD_DIM, HEAD_DIM), lambda i: (i, l, 0, 0, 0, 0)),
                     pl.BlockSpec((None, None, 2, GROUP_W), lambda i: (i, l, 0, 0))]
        args += [s0, h0]
    params = [lp["lg_lane"], lp["gn_g"], lp["conv_w"], lp["conv_b"], lp["lru_wbd"], lp["lru_bias"], lp["lru_lam"]]
    in_specs += [full(a) for a in params]
    args += params
    out_specs = [pl.BlockSpec((None, seq_len, 2 * GROUP_W), lambda i: (i, 0, 0))]
    out_shape = [jax.ShapeDtypeStruct((b, seq_len, 2 * GROUP_W), BF16)]
    if not has_state:
        heads = GROUP_W // HEAD_DIM
        out_specs += [pl.BlockSpec((None, 2, heads, HEAD_DIM, HEAD_DIM), lambda i: (i, 0, 0, 0, 0)),
                      pl.BlockSpec((None, 2, GROUP_W), lambda i: (i, 0, 0))]
        out_shape += [jax.ShapeDtypeStruct((b, 2, heads, HEAD_DIM, HEAD_DIM), F32),
                      jax.ShapeDtypeStruct((b, 2, GROUP_W), F32)]
    scratch = [
        pltpu.VMEM((GROUP_W // HEAD_DIM, RET_CHUNK, RET_CHUNK), F32),
        pltpu.VMEM((4, RET_CHUNK, GROUP_W), F32),
        pltpu.VMEM((seq_len + 2 * SUBLANES, GROUP_W), F32),
        pltpu.VMEM((2, 2, seq_len, LANES), F32),
        pltpu.VMEM((2, 2, seq_len, LANES), F32),
        pltpu.VMEM((SUBLANES, n0, GROUP_W), F32),
        pltpu.VMEM((2, seq_len, LANES), F32),
        pltpu.VMEM((SUBLANES, n0, GROUP_W), F32),
        pltpu.VMEM((SUBLANES, n0, GROUP_W), F32),
        pltpu.VMEM((2 * n0, GROUP_W), F32),
        pltpu.VMEM((2 * n0, GROUP_W), F32),
    ]
    outs = pl.pallas_call(
        functools.partial(_recurrent_kernel, seq_len=seq_len, has_state=has_state),
        grid=(b,),
        in_specs=in_specs,
        out_specs=out_specs,
        out_shape=out_shape,
        scratch_shapes=scratch,
        compiler_params=pltpu.CompilerParams(
            dimension_semantics=("arbitrary",), vmem_limit_bytes=VMEM_LIMIT),
        name="recurrent_latent" if has_state else "recurrent_ctx",
    )(*args)
    return (outs[0].reshape(t, 2 * GROUP_W),) + tuple(outs[1:])


def _dup_half(x, half):
    sw = pltpu.roll(x, HEAD_DIM, axis=1)
    low = _lane(x.shape) < HEAD_DIM
    return jnp.where(low, x, sw) if half == 0 else jnp.where(low, sw, x)


def _attention_kernel(*refs, seq_len, has_ctx, seqs_per_step):
    n_in = 10 if has_ctx else 6
    scal_ref, ng_ref = refs[0], refs[n_in - 1]
    per_seq = refs[1:n_in - 1] + refs[n_in:]
    per_sequence = [_attention_stages(scal_ref, ng_ref, *[r.at[s] for r in per_seq], seq_len=seq_len,
                                      has_ctx=has_ctx) for s in range(seqs_per_step)]
    stages = [st for seq_stages in per_sequence for st in seq_stages]
    pending = []
    for make_weights, make_out, p in stages:
        pending.append((make_out, p, make_weights(p)))
        if len(pending) > (ATTN_LOOKAHEAD_LATENT if has_ctx else ATTN_LOOKAHEAD_CTX):
            make_out_, p_, w_ = pending.pop(0)
            make_out_(p_, w_)
    for make_out_, p_, w_ in pending:
        make_out_(p_, w_)


def _attention_stages(scal_ref, ng_ref, *refs, seq_len, has_ctx):
    if has_ctx:
        (sq_ref, skv_ref, dq_ref, dkv_ref, cks_ref, cvs_ref, ckd_ref, cvd_ref, y_ref,
         sk_s, sv_s, dk_s, dv_s, csk_s, csv_s, cdk_s, cdv_s) = refs
    else:
        (sq_ref, skv_ref, dq_ref, dkv_ref, y_ref, sk_s, sv_s, dk_s, dv_s) = refs
    tq = ATTN_Q_TILE
    heads = GROUP_W // HEAD_DIM
    lane = _lane((1, LANES))
    low = lane < HEAD_DIM
    halves = (low, ~low)

    def build():
        def fill(k_all, v_all, k_dst, v_dst, dup):
            for p in range(2):
                if dup:
                    k_p, v_p = _dup_half(k_all, p), _dup_half(v_all, p)
                else:
                    k_p, v_p = k_all[:, p * LANES:(p + 1) * LANES], v_all[:, p * LANES:(p + 1) * LANES]
                k_dst[p] = k_p.astype(BF16)
                for j in range(2):
                    v_dst[2 * p + j] = jnp.where(halves[j], v_p, 1.0).astype(BF16)

        fill(skv_ref[:, 2 * LANES:3 * LANES], skv_ref[:, 3 * LANES:4 * LANES], sk_s, sv_s, True)
        fill(dkv_ref[:, GROUP_W:2 * GROUP_W], dkv_ref[:, 2 * GROUP_W:3 * GROUP_W], dk_s, dv_s, False)
        if has_ctx:
            fill(cks_ref[...], cvs_ref[...], csk_s, csv_s, True)
            fill(ckd_ref[...], cvd_ref[...], cdk_s, cdv_s, False)

    if seq_len == tq:
        build()
    else:
        pl.when(pl.program_id(1) == 0)(build)

    def weights(q_rows, k_loc, k_ctx, valid, extra2):
        s = _dot_nt(q_rows, k_loc)
        if valid is not None:
            s = jnp.where(valid, s, NEG_BIG)
        m = jnp.max(s, -1, keepdims=True)
        sc = None
        if k_ctx is not None:
            sc = _dot_nt(q_rows, k_ctx)
            m = jnp.maximum(m, jnp.max(sc, -1, keepdims=True))
        if extra2 is not None:
            m = jnp.maximum(m, extra2)
        e = jnp.exp2((s - m).astype(BF16))
        ec = None if sc is None else jnp.exp2((sc - m).astype(BF16))
        extra = None if extra2 is None else jnp.exp2(extra2 - m)
        return e, ec, extra

    def weighted_values(w, v_of_head):
        e, ec, extra = w
        half_rows = e.shape[0] // 2
        res = []
        for j in range(2):
            rs = slice(j * half_rows, (j + 1) * half_rows)
            v_loc, v_ctx = v_of_head(j)
            o = _dot(e[rs], v_loc)
            if ec is not None:
                o = o + _dot(ec[rs], v_ctx)
            den = pltpu.roll(o, HEAD_DIM, axis=1)
            if extra is not None:
                den = den + extra[rs]
            res.append(o / jnp.where(halves[j], den, 1.0))
        return res

    if has_ctx:
        win = 2 * tq
        i = pl.program_id(1)
        blk = jnp.clip(2 * i - 1, 0, (seq_len - win) // SWA_WINDOW)
        start = pl.multiple_of(blk * SWA_WINDOW, SWA_WINDOW)
        rows = pl.ds(start, win)
        qpos = i * tq + (lax.broadcasted_iota(jnp.int32, (2 * tq, win), 0) & (tq - 1))
        kpos = start + lax.broadcasted_iota(jnp.int32, (2 * tq, win), 1)
        valid = jnp.abs(kpos - qpos) <= SWA_WINDOW
    else:
        rows = slice(None)
        valid = None
    qscale = (HEAD_DIM ** -0.5) * LOG2E
    dscale = (DIFF_QK_DIM ** -0.5) * LOG2E
    first_member = lax.broadcasted_iota(jnp.int32, (2 * tq, 1), 0) < tq
    lam = scal_ref[3 * heads]
    coef = scal_ref[3 * heads + 1]
    grp = lane >> 5

    def swa_weights(p):
        q = sq_ref[:, p * LANES:(p + 1) * LANES] * qscale
        q_rows = jnp.concatenate([jnp.where(halves[j], q, 0.0).astype(BF16) for j in range(2)], axis=0)
        sink2 = jnp.where(first_member, scal_ref[2 * heads + 2 * p], scal_ref[2 * heads + 2 * p + 1]) * LOG2E
        return weights(q_rows, sk_s[p, rows, :], csk_s[p] if has_ctx else None, valid, sink2)

    def swa_out(p, w):
        outs = weighted_values(w, lambda j: (sv_s[2 * p + j, rows, :], csv_s[2 * p + j] if has_ctx else None))
        y_ref[:, p * LANES:(p + 1) * LANES] = jnp.where(low, outs[0], outs[1]).astype(BF16)

    def diff_weights(p):
        q = dq_ref[:, p * LANES:(p + 1) * LANES] * dscale
        q_rows = jnp.concatenate([jnp.where(grp == g, q, 0.0).astype(BF16) for g in range(4)], axis=0)
        return weights(q_rows, dk_s[p], cdk_s[p] if has_ctx else None, None, None)

    def diff_out(p, w):
        parts = weighted_values(w, lambda j: (dv_s[2 * p + j], cdv_s[2 * p + j] if has_ctx else None))
        outs = [r[0:tq] - lam * r[tq:2 * tq] for r in parts]
        o = jnp.where(low, outs[0], outs[1])
        sq = o * o
        ms = jnp.where(low, jnp.sum(jnp.where(low, sq, 0.0), -1, keepdims=True),
                       jnp.sum(jnp.where(low, 0.0, sq), -1, keepdims=True)) * (1.0 / HEAD_DIM)
        y = o * lax.rsqrt(ms + EPS) * ng_ref[:, p * LANES:(p + 1) * LANES] * coef
        y_ref[:, GROUP_W + p * LANES:GROUP_W + (p + 1) * LANES] = y.astype(BF16)

    return [(swa_weights, swa_out, 0), (diff_weights, diff_out, 0),
            (swa_weights, swa_out, 1), (diff_weights, diff_out, 1)]


def _attention(scal, zs, zd, norm_g_tiled, l, seq_len, caches=None):
    t = zs.shape[0]
    b = t // seq_len
    tq = ATTN_Q_TILE
    nq = seq_len // tq
    has_ctx = caches is not None
    zs3 = zs.reshape(b, seq_len, 2 * GROUP_W)
    zd3 = zd.reshape(b, seq_len, 3 * GROUP_W)
    ns = 1 if has_ctx else CTX_SEQS_PER_STEP
    in_specs = [
        pl.BlockSpec(memory_space=pltpu.SMEM),
        pl.BlockSpec((ns, tq, 2 * GROUP_W), lambda i, j: (i, j, 0)),
        pl.BlockSpec((ns, seq_len, 2 * GROUP_W), lambda i, j: (i, 0, 0)),
        pl.BlockSpec((ns, tq, 3 * GROUP_W), lambda i, j: (i, j, 0)),
        pl.BlockSpec((ns, seq_len, 3 * GROUP_W), lambda i, j: (i, 0, 0)),
    ]
    args = [scal, zs3, zs3, zd3, zd3]
    heads = GROUP_W // HEAD_DIM
    scratch = [pltpu.VMEM((ns, 2, seq_len, LANES), BF16), pltpu.VMEM((ns, heads, seq_len, LANES), BF16),
               pltpu.VMEM((ns, 2, seq_len, LANES), BF16), pltpu.VMEM((ns, heads, seq_len, LANES), BF16)]
    if has_ctx:
        past = caches[0].shape[2]
        for a in caches:
            in_specs.append(pl.BlockSpec((ns, None) + a.shape[2:], lambda i, j: (i, l, 0, 0)))
            args.append(a)
        scratch += [pltpu.VMEM((ns, 2, past, LANES), BF16), pltpu.VMEM((ns, heads, past, LANES), BF16),
                    pltpu.VMEM((ns, 2, past, LANES), BF16), pltpu.VMEM((ns, heads, past, LANES), BF16)]
    in_specs.append(pl.BlockSpec(norm_g_tiled.shape, lambda i, j: (0, 0)))
    args.append(norm_g_tiled)
    y = pl.pallas_call(
        functools.partial(_attention_kernel, seq_len=seq_len, has_ctx=has_ctx, seqs_per_step=ns),
        grid=(b // ns, nq),
        in_specs=in_specs,
        out_specs=pl.BlockSpec((ns, tq, 2 * GROUP_W), lambda i, j: (i, j, 0)),
        out_shape=jax.ShapeDtypeStruct((b, seq_len, 2 * GROUP_W), BF16),
        scratch_shapes=scratch,
        compiler_params=pltpu.CompilerParams(
            dimension_semantics=("arbitrary", "arbitrary"), vmem_limit_bytes=VMEM_LIMIT),
        name="attention_latent" if has_ctx else "attention_ctx",
    )(*args)
    return y.reshape(t, 2 * GROUP_W)


def _out_mlp_kernel(x_ref, yr_ref, ya_ref, mod_ref, wo_ref, g_ref, w1_ref, w2_ref, gf_ref, o_ref, *, final):
    g1 = mod_ref[:, 2 * D_MODEL:3 * D_MODEL]
    sh2 = mod_ref[:, 3 * D_MODEL:4 * D_MODEL]
    sc2 = mod_ref[:, 4 * D_MODEL:5 * D_MODEL]
    g2 = mod_ref[:, 5 * D_MODEL:6 * D_MODEL]
    y = _dot(yr_ref[...], wo_ref[0:2 * GROUP_W, :]) + _dot(ya_ref[...], wo_ref[2 * GROUP_W:4 * GROUP_W, :])
    x = x_ref[...] + g1 * y
    h = (_rms(x) * g_ref[...] * (1.0 + sc2) + sh2).astype(BF16)
    acc = None
    for c in range(D_FF // D_MODEL):
        sl = slice(c * D_MODEL, (c + 1) * D_MODEL)
        hid = jnp.square(jnp.maximum(_dot(h, w1_ref[:, sl]), 0.0)).astype(BF16)
        part = _dot(hid, w2_ref[sl, :])
        acc = part if acc is None else acc + part
    x = x + g2 * acc
    if final:
        x = _rms(x) * gf_ref[...]
    o_ref[...] = x


def _out_mlp(x, y_rec, y_att, mod4, w_out_b, norm_g, w_ff1_b, w_ff2_b, final_g, l, seq_len, is_latent, final):
    t = x.shape[0]
    tm = ROW_TILE
    if is_latent:
        tiles_per_seq = seq_len // tm
        mod_map = lambda i: (l, 1 + i // tiles_per_seq, 0, 0)
    else:
        mod_map = lambda i: (l, 0, 0, 0)
    once = pl.Buffered(1)
    in_specs = [
        pl.BlockSpec((tm, D_MODEL), lambda i: (i, 0)),
        pl.BlockSpec((tm, 2 * GROUP_W), lambda i: (i, 0)),
        pl.BlockSpec((tm, 2 * GROUP_W), lambda i: (i, 0)),
        pl.BlockSpec((None, None, 1, N_MOD * D_MODEL), mod_map),
        pl.BlockSpec((None, D_MODEL, D_MODEL), lambda i: (l, 0, 0), pipeline_mode=once),
        pl.BlockSpec((None, 1, D_MODEL), lambda i: (l, 0, 0)),
        pl.BlockSpec((None, D_MODEL, D_FF), lambda i: (l, 0, 0), pipeline_mode=once),
        pl.BlockSpec((None, D_FF, D_MODEL), lambda i: (l, 0, 0), pipeline_mode=once),
        pl.BlockSpec((1, D_MODEL), lambda i: (0, 0)),
    ]
    return pl.pallas_call(
        functools.partial(_out_mlp_kernel, final=final),
        grid=(t // tm,),
        in_specs=in_specs,
        out_specs=pl.BlockSpec((tm, D_MODEL), lambda i: (i, 0)),
        out_shape=jax.ShapeDtypeStruct((t, D_MODEL), F32),
        compiler_params=pltpu.CompilerParams(
            dimension_semantics=("arbitrary",), vmem_limit_bytes=VMEM_LIMIT),
        name="out_mlp_final" if final else "out_mlp",
    )(x, y_rec, y_att, mod4, w_out_b, norm_g, w_ff1_b, w_ff2_b, final_g)


def _rope_tables(length):
    row = (jnp.arange(length) // GRID_W).astype(F32)
    col = (jnp.arange(length) % GRID_W).astype(F32)
    tabs = []
    for dim in (HEAD_DIM, DIFF_QK_DIM):
        n = dim // 4
        inv = ROPE_BASE ** (-jnp.arange(n, dtype=F32) / n)
        ang = jnp.concatenate([row[:, None] * inv, col[:, None] * inv], -1)
        cos, sin = jnp.cos(ang), jnp.sin(ang)
        reps = LANES // dim
        tabs.append(jnp.tile(jnp.concatenate([cos, cos], -1), (1, reps)))
        tabs.append(jnp.tile(jnp.concatenate([-sin, sin], -1), (1, reps)))
    return tabs


def _block_diag(w):
    n, c, d = w.shape
    eye = jnp.eye(n, dtype=w.dtype)
    return jnp.einsum("ncd,nm->ncmd", w, eye).reshape(n * c, n * d)


def _layer_params(l, ret_decay, ret_gn_g, lru_conv_w, lru_conv_b, lru_w_a, lru_b_a, lru_w_x, lru_b_x, lru_lambda,
                  swa_sink, diff_lambda, diff_norm_g):
    heads = GROUP_W // HEAD_DIM
    log_gamma = jax.nn.log_sigmoid(ret_decay[l].astype(F32))
    lambda_init = 0.8 - 0.6 * math.exp(-0.3 * l)
    lv = diff_lambda[l].astype(F32)
    lam = jnp.exp(jnp.sum(lv[0] * lv[1])) - jnp.exp(jnp.sum(lv[2] * lv[3])) + lambda_init
    scal = jnp.concatenate([log_gamma[0], log_gamma[1], swa_sink[l].astype(F32),
                            jnp.stack([lam, jnp.asarray(1.0 - lambda_init, F32)]),
                            jnp.zeros((2,), F32)])
    wbd = jnp.concatenate([_block_diag(lru_w_a[l, 0]), _block_diag(lru_w_x[l, 0]),
                           _block_diag(lru_w_a[l, 1]), _block_diag(lru_w_x[l, 1])], axis=1).astype(BF16)
    bias = jnp.concatenate([lru_b_a[l, 0], lru_b_x[l, 0], lru_b_a[l, 1], lru_b_x[l, 1]])[None, :]
    return dict(
        scal=scal,
        lg_lane=jnp.repeat(log_gamma, HEAD_DIM, axis=1),
        gn_g=ret_gn_g[l][None, :],
        conv_w=lru_conv_w[l],
        conv_b=lru_conv_b[l][None, :],
        lru_wbd=wbd,
        lru_bias=bias,
        lru_lam=lru_lambda[l],
        diff_ng=jnp.tile(diff_norm_g[l], heads)[None, :],
    )


def kernel(x_prompt, x_sample, c, state_ret, state_lru, cache_swa_k, cache_swa_v, cache_diff_k, cache_diff_v,
           c_ctx, w_ada, b_ada, norm_mix_g, w_in, ret_decay, ret_gn_g, lru_conv_w, lru_conv_b,
           lru_w_a, lru_b_a, lru_w_x, lru_b_x, lru_lambda, swa_sink, diff_lambda, diff_norm_g,
           w_out, norm_mlp_g, w_ff1, w_ff2, final_norm_g):
    batch, seq, _ = x_prompt.shape
    dec_batch, dec_seq, _ = x_sample.shape
    past = cache_swa_k.shape[2]

    cvec = jnp.concatenate([c_ctx[None, :], c, jnp.zeros((MOD_ROWS - 1 - dec_batch, D_MODEL), F32)], axis=0)
    mod4 = _modulation(cvec, w_ada, b_ada).reshape(DEPTH, MOD_ROWS, 1, N_MOD * D_MODEL)

    w_in_b = w_in.astype(BF16)
    w_out_b = w_out.astype(BF16)
    w_ff1_b = w_ff1.astype(BF16)
    w_ff2_b = w_ff2.astype(BF16)
    norm_mix3 = norm_mix_g.reshape(DEPTH, 1, D_MODEL)
    norm_mlp3 = norm_mlp_g.reshape(DEPTH, 1, D_MODEL)
    final_g = final_norm_g[None, :]
    rope_tabs = _rope_tables(dec_seq)
    layers = [_layer_params(l, ret_decay, ret_gn_g, lru_conv_w, lru_conv_b, lru_w_a, lru_b_a, lru_w_x, lru_b_x,
                            lru_lambda, swa_sink, diff_lambda, diff_norm_g) for l in range(DEPTH)]

    heads = GROUP_W // HEAD_DIM

    xp = x_prompt.reshape(batch * seq, D_MODEL)
    new_ret, new_lru, new_swa, new_diff = [], [], [], []
    for l in range(DEPTH):
        lp = layers[l]
        zr, zl, zs, zd = _in_proj(xp, mod4, norm_mix3, w_in_b, l, seq, None)
        y_rec, s_ret, s_lru = _recurrent(lp["scal"], zr, zl, lp, l, seq)
        y_att = _attention(lp["scal"], zs, zd, lp["diff_ng"], l, seq)
        xp = _out_mlp(xp, y_rec, y_att, mod4, w_out_b, norm_mlp3, w_ff1_b, w_ff2_b, final_g,
                      l, seq, False, l == DEPTH - 1)
        new_ret.append(s_ret)
        new_lru.append(s_lru)
        new_swa.append(zs[:, GROUP_W:2 * GROUP_W])
        new_diff.append(zd[:, GROUP_W:3 * GROUP_W])
    y_prompt = xp.reshape(batch, seq, D_MODEL)
    new_ret = jnp.stack(new_ret, axis=1)
    new_swa = jnp.stack(new_swa, axis=1).reshape(batch, seq, DEPTH, 2, 2, HEAD_DIM)
    new_swa = jnp.transpose(new_swa, (3, 0, 2, 1, 4, 5))
    new_diff = jnp.stack(new_diff, axis=1).reshape(batch, seq, DEPTH, 2, heads, HEAD_DIM)
    new_diff = jnp.transpose(new_diff, (3, 0, 2, 1, 4, 5))

    xs = x_sample.reshape(dec_batch * dec_seq, D_MODEL)
    state = (state_ret, state_lru)
    caches = (cache_swa_k.reshape(dec_batch, DEPTH, past, 2 * HEAD_DIM),
              cache_swa_v.reshape(dec_batch, DEPTH, past, 2 * HEAD_DIM),
              cache_diff_k.reshape(dec_batch, DEPTH, past, GROUP_W),
              cache_diff_v.reshape(dec_batch, DEPTH, past, GROUP_W))
    for l in range(DEPTH):
        lp = layers[l]
        zr, zl, zs, zd = _in_proj(xs, mod4, norm_mix3, w_in_b, l, dec_seq, rope_tabs)
        (y_rec,) = _recurrent(lp["scal"], zr, zl, lp, l, dec_seq, state)
        y_att = _attention(lp["scal"], zs, zd, lp["diff_ng"], l, dec_seq, caches)
        xs = _out_mlp(xs, y_rec, y_att, mod4, w_out_b, norm_mlp3, w_ff1_b, w_ff2_b, final_g,
                      l, dec_seq, True, l == DEPTH - 1)
    y_sample = xs.reshape(dec_batch, dec_seq, D_MODEL)

    return (y_prompt, y_sample, new_ret, jnp.stack(new_lru, axis=1),
            new_swa[0], new_swa[1], new_diff[0], new_diff[1])
```

```python
import functools
import math

import jax
import jax.numpy as jnp
from jax import lax
from jax.experimental import pallas as pl
from jax.experimental.pallas import tpu as pltpu

F32 = jnp.float32
BF16 = jnp.bfloat16

D_MODEL = 1024
DEPTH = 4
GRID_W = 64
HEAD_DIM = 64
GROUP_W = D_MODEL // 4
RET_CHUNK = 256
LRU_BLOCKS = 4
LRU_C = 8.0
SWA_WINDOW = 128
DIFF_QK_DIM = HEAD_DIM // 2
D_FF = 4 * D_MODEL
ROPE_BASE = 10000.0
EPS = 1e-6
N_MOD = 6
D_IN = 11 * GROUP_W
MOD_ROWS = 16

LANES = 128
SUBLANES = 8
VMEM_LIMIT = 56 * 1024 * 1024
NEG_BIG = -1e30
LOG2E = math.log2(math.e)
SWA_QSCALE = (HEAD_DIM ** -0.5) * LOG2E
DIFF_QSCALE = (DIFF_QK_DIM ** -0.5) * LOG2E

ROW_TILE = 512
MLP_ROW_TILE = 1024
ATTN_Q_TILE = 256
CTX_SEQS_PER_STEP = 2
ATTN_LOOKAHEAD_LATENT = 2
ATTN_LOOKAHEAD_CTX = 1


def _dot(a, b):
    return jnp.dot(a, b, preferred_element_type=F32)


def _dot_nt(a, b):
    return lax.dot_general(a, b, (((1,), (1,)), ((), ())), preferred_element_type=F32)


def _dot_tn(a, b):
    return lax.dot_general(a, b, (((0,), (0,)), ((), ())), preferred_element_type=F32)


def _rms(x):
    return x * lax.rsqrt(jnp.mean(x * x, axis=-1, keepdims=True) + EPS)


def _sigmoid(x):
    return 0.5 * jnp.tanh(0.5 * x) + 0.5


def _lane(shape):
    return lax.broadcasted_iota(jnp.int32, shape, len(shape) - 1)


def _mod_kernel(c_ref, w_ref, b_ref, o_ref):
    s = jax.nn.silu(c_ref[...]).astype(BF16)
    o_ref[...] = _dot(s, w_ref[...].astype(BF16)) + b_ref[...]


def _modulation(cvec, w_ada, b_ada):
    tn = 1536
    n_mod = N_MOD * D_MODEL
    return pl.pallas_call(
        _mod_kernel,
        grid=(DEPTH, n_mod // tn),
        in_specs=[
            pl.BlockSpec((MOD_ROWS, D_MODEL), lambda l, j: (0, 0)),
            pl.BlockSpec((None, D_MODEL, tn), lambda l, j: (l, 0, j)),
            pl.BlockSpec((None, 1, tn), lambda l, j: (l, 0, j)),
        ],
        out_specs=pl.BlockSpec((None, MOD_ROWS, tn), lambda l, j: (l, 0, j)),
        out_shape=jax.ShapeDtypeStruct((DEPTH, MOD_ROWS, n_mod), F32),
        compiler_params=pltpu.CompilerParams(
            dimension_semantics=("arbitrary", "arbitrary"), vmem_limit_bytes=VMEM_LIMIT),
        name="modulation",
    )(cvec, w_ada, b_ada.reshape(DEPTH, 1, n_mod))


def _rope_slab(x, c, s, half):
    first = (_lane(x.shape) & (2 * half - 1)) < half
    swapped = jnp.where(first, pltpu.roll(x, LANES - half, axis=1), pltpu.roll(x, half, axis=1))
    return x * c + swapped * s


def _in_proj_kernel(*refs, rope):
    if rope:
        (x_ref, mod_ref, g_ref, w_ref, c64_ref, s64_ref, c32_ref, s32_ref,
         zr_ref, zl_ref, qa_ref, kvs_ref, kvd_ref) = refs
    else:
        x_ref, mod_ref, g_ref, w_ref, zr_ref, zl_ref, qa_ref, kvs_ref, kvd_ref = refs
    sh1 = mod_ref[:, 0:D_MODEL]
    sc1 = mod_ref[:, D_MODEL:2 * D_MODEL]
    h = (_rms(x_ref[...]) * g_ref[...] * (1.0 + sc1) + sh1).astype(BF16)
    zr_ref[...] = _dot(h, w_ref[:, 0:4 * GROUP_W])
    zl_ref[...] = _dot(h, w_ref[:, 4 * GROUP_W:6 * GROUP_W])
    zs = _dot(h, w_ref[:, 6 * GROUP_W:8 * GROUP_W])
    zd = _dot(h, w_ref[:, 8 * GROUP_W:11 * GROUP_W])
    if rope:
        c64, s64, c32, s32 = c64_ref[...], s64_ref[...], c32_ref[...], s32_ref[...]
        rope_s = lambda v: _rope_slab(v, c64, s64, HEAD_DIM // 2)
        rope_d = lambda v: _rope_slab(v, c32, s32, DIFF_QK_DIM // 2)
    else:
        rope_s = rope_d = lambda v: v
    out = qa_ref.dtype
    for i in range(2):
        sl = slice(i * LANES, (i + 1) * LANES)
        qa_ref[:, sl] = (rope_s(zs[:, sl]) * SWA_QSCALE).astype(out)
        qa_ref[:, GROUP_W + i * LANES:GROUP_W + (i + 1) * LANES] = (rope_d(zd[:, sl]) * DIFF_QSCALE).astype(out)
        kvd_ref[:, sl] = rope_d(zd[:, GROUP_W + i * LANES:GROUP_W + (i + 1) * LANES]).astype(out)
    kvs_ref[:, 0:LANES] = rope_s(zs[:, 2 * LANES:3 * LANES]).astype(out)
    kvs_ref[:, LANES:2 * LANES] = zs[:, 3 * LANES:4 * LANES].astype(out)
    kvd_ref[:, GROUP_W:2 * GROUP_W] = zd[:, 2 * GROUP_W:3 * GROUP_W].astype(out)


def _in_proj(x, mod4, norm_g, w_in_b, l, seq_len, rope_tabs):
    t = x.shape[0]
    tm = ROW_TILE
    rope = rope_tabs is not None
    if rope:
        tiles_per_seq = seq_len // tm
        mod_map = lambda i: (l, 1 + i // tiles_per_seq, 0, 0)
    else:
        mod_map = lambda i: (l, 0, 0, 0)
    in_specs = [
        pl.BlockSpec((tm, D_MODEL), lambda i: (i, 0)),
        pl.BlockSpec((None, None, 1, N_MOD * D_MODEL), mod_map),
        pl.BlockSpec((None, 1, D_MODEL), lambda i: (l, 0, 0)),
        pl.BlockSpec((None, D_MODEL, D_IN), lambda i: (l, 0, 0)),
    ]
    args = [x, mod4, norm_g, w_in_b]
    if rope:
        tab_spec = pl.BlockSpec((tm, LANES), lambda i: (i % tiles_per_seq, 0))
        in_specs += [tab_spec] * 4
        args += list(rope_tabs)
    att = BF16 if rope else F32
    outs = ((4 * GROUP_W, F32), (2 * GROUP_W, F32), (2 * GROUP_W, att), (GROUP_W, att), (2 * GROUP_W, att))
    return pl.pallas_call(
        functools.partial(_in_proj_kernel, rope=rope),
        grid=(t // tm,),
        in_specs=in_specs,
        out_specs=[pl.BlockSpec((tm, w), lambda i: (i, 0)) for w, _ in outs],
        out_shape=[jax.ShapeDtypeStruct((t, w), dt) for w, dt in outs],
        compiler_params=pltpu.CompilerParams(
            dimension_semantics=("arbitrary",), vmem_limit_bytes=VMEM_LIMIT),
        name="in_proj_rope" if rope else "in_proj",
    )(*args)


def _scan_levels(a_slabs, u_slabs, pa_ref, pb_ref, sa_ref, sb_ref, h0, n0, reverse):
    order = list(range(SUBLANES - 1, -1, -1)) if reverse else list(range(SUBLANES))
    acc_a = acc_b = None
    for r in order:
        if acc_a is None:
            acc_a, acc_b = a_slabs[r], u_slabs[r]
        else:
            acc_b = a_slabs[r] * acc_b + u_slabs[r]
            acc_a = a_slabs[r] * acc_a
        pa_ref[r] = acc_a
        pb_ref[r] = acc_b
    pad = n0 // 2
    ident = pad + n0 if reverse else 0
    sa_ref[ident:ident + pad, :] = jnp.ones((pad, GROUP_W), F32)
    sb_ref[ident:ident + pad, :] = jnp.zeros((pad, GROUP_W), F32)
    sign = 1 if reverse else -1
    k = 1
    while k < n0:
        sa_ref[pad:pad + n0, :] = acc_a
        sb_ref[pad:pad + n0, :] = acc_b
        off = pad + sign * k
        acc_b = acc_a * sb_ref[off:off + n0, :] + acc_b
        acc_a = acc_a * sa_ref[off:off + n0, :]
        k *= 2
    sa_ref[pad:pad + n0, :] = acc_a
    sb_ref[pad:pad + n0, :] = acc_b
    off = pad + sign
    carry_in = sa_ref[off:off + n0, :] * h0 + sb_ref[off:off + n0, :]
    end = 0 if reverse else n0 - 1
    final = acc_a[end:end + 1, :] * h0 + acc_b[end:end + 1, :]
    h_slabs = [pa_ref[r] * carry_in + pb_ref[r] for r in range(SUBLANES)]
    return h_slabs, final


def _recurrent_kernel(*refs, seq_len, has_state):
    if has_state:
        (scal_ref, zr_ref, zl_ref, s0_ref, h0_ref, lgl_ref, gng_ref, cw_ref, cb_ref, wbd_ref, lb_ref, lam_ref,
         y_ref,
         d_ref, vec_ref, pad_ref, a_ref, u_ref, hf_ref, hs_ref, pa_ref, pb_ref, sa_ref, sb_ref) = refs
    else:
        (scal_ref, zr_ref, zl_ref, lgl_ref, gng_ref, cw_ref, cb_ref, wbd_ref, lb_ref, lam_ref,
         y_ref, sret_ref, slru_ref,
         d_ref, vec_ref, pad_ref, a_ref, u_ref, hf_ref, hs_ref, pa_ref, pb_ref, sa_ref, sb_ref) = refs
    ch = RET_CHUNK
    n_chunks = seq_len // ch
    n0 = seq_len // SUBLANES
    heads = GROUP_W // HEAD_DIM

    @pl.when(pl.program_id(0) == 0)
    def _():
        n = lax.broadcasted_iota(jnp.int32, (ch, ch), 0)
        m = lax.broadcasted_iota(jnp.int32, (ch, ch), 1)
        dist = (n - m).astype(F32)
        for h in range(heads):
            lg = jnp.where(dist > 0, scal_ref[h], scal_ref[heads + h])
            d_ref[h] = jnp.where(dist == 0, 2.0, jnp.exp(jnp.abs(dist) * lg))
        t = lax.broadcasted_iota(jnp.int32, (ch, GROUP_W), 0).astype(F32)
        lgf = lgl_ref[0:1, :]
        lgb = lgl_ref[1:2, :]
        vec_ref[0] = jnp.exp((t + 1.0) * lgf)
        vec_ref[1] = jnp.exp((ch - t) * lgb)
        vec_ref[2] = jnp.exp((ch - 1.0 - t) * lgf)
        vec_ref[3] = jnp.exp(t * lgb)

    low = _lane((1, LANES)) < HEAD_DIM
    rr = lax.broadcasted_iota(jnp.int32, (LANES, LANES), 0) < HEAD_DIM
    cc = lax.broadcasted_iota(jnp.int32, (LANES, LANES), 1) < HEAD_DIM
    block_diag = (rr == cc).astype(F32)
    g_chunk_f = jnp.exp(ch * lgl_ref[0:1, :])
    g_chunk_b = jnp.exp(ch * lgl_ref[1:2, :])

    def kv(c, p):
        rows = slice(c * ch, (c + 1) * ch)
        k = zr_ref[rows, GROUP_W + p * LANES:GROUP_W + (p + 1) * LANES] * (HEAD_DIM ** -0.5)
        v = zr_ref[rows, 2 * GROUP_W + p * LANES:2 * GROUP_W + (p + 1) * LANES].astype(BF16)
        return k, v

    def state_update(k, v, zeta):
        return _dot_tn((k * zeta).astype(BF16), v) * block_diag

    def pair_state(d, p):
        zero = jnp.zeros((HEAD_DIM, HEAD_DIM), F32)
        top = jnp.concatenate([s0_ref[d, 2 * p], zero], axis=1)
        bottom = jnp.concatenate([zero, s0_ref[d, 2 * p + 1]], axis=1)
        return jnp.concatenate([top, bottom], axis=0)

    def store_state(d, p, s_pair):
        sret_ref[d, 2 * p] = s_pair[0:HEAD_DIM, 0:HEAD_DIM]
        sret_ref[d, 2 * p + 1] = s_pair[HEAD_DIM:LANES, HEAD_DIM:LANES]

    for p in range(2):
        pl_sl = slice(p * LANES, (p + 1) * LANES)
        if has_state:
            s_b = [None] * n_chunks
            s_b[n_chunks - 1] = pair_state(1, p)
            for c in range(n_chunks - 1, 0, -1):
                k, v = kv(c, p)
                s_b[c - 1] = s_b[c] * g_chunk_b[:, pl_sl] + state_update(k, v, vec_ref[3, :, pl_sl])
            s_f = pair_state(0, p)
        for c in range(n_chunks):
            rows = slice(c * ch, (c + 1) * ch)
            q = zr_ref[rows, pl_sl]
            k, v = kv(c, p)
            kb = k.astype(BF16)
            outs = []
            for j in range(2):
                qm = jnp.where(low if j == 0 else ~low, q, 0.0).astype(BF16)
                w = (_dot_nt(qm, kb) * d_ref[2 * p + j]).astype(BF16)
                outs.append(_dot(w, v))
            o = jnp.where(low, outs[0], outs[1])
            if has_state:
                qb = q.astype(BF16)
                o = o + _dot(qb, s_f.astype(BF16)) * vec_ref[0, :, pl_sl]
                o = o + _dot(qb, s_b[c].astype(BF16)) * vec_ref[1, :, pl_sl]
                if c + 1 < n_chunks:
                    s_f = s_f * g_chunk_f[:, pl_sl] + state_update(k, v, vec_ref[2, :, pl_sl])
            else:
                store_state(0, p, state_update(k, v, vec_ref[2, :, pl_sl]))
                store_state(1, p, state_update(k, v, vec_ref[3, :, pl_sl]))
            inv = 1.0 / HEAD_DIM
            mu = jnp.where(low, jnp.sum(jnp.where(low, o, 0.0), -1, keepdims=True),
                           jnp.sum(jnp.where(low, 0.0, o), -1, keepdims=True)) * inv
            dlt = o - mu
            sq = dlt * dlt
            var = jnp.where(low, jnp.sum(jnp.where(low, sq, 0.0), -1, keepdims=True),
                            jnp.sum(jnp.where(low, 0.0, sq), -1, keepdims=True)) * inv
            gate = zr_ref[rows, 3 * GROUP_W + p * LANES:3 * GROUP_W + (p + 1) * LANES]
            y = dlt * lax.rsqrt(var + EPS) * gng_ref[:, pl_sl] * (gate * _sigmoid(gate))
            y_ref[rows, pl_sl] = y.astype(BF16)

    x = zl_ref[:, 0:GROUP_W]
    zero8 = jnp.zeros((SUBLANES, GROUP_W), F32)
    pad_ref[0:SUBLANES, :] = zero8
    pad_ref[SUBLANES:SUBLANES + seq_len, :] = x
    pad_ref[SUBLANES + seq_len:2 * SUBLANES + seq_len, :] = zero8
    cw = cw_ref[...]
    xc = (cw[0:1] * pad_ref[SUBLANES - 2:SUBLANES - 2 + seq_len, :]
          + cw[1:2] * pad_ref[SUBLANES - 1:SUBLANES - 1 + seq_len, :]
          + cw[2:3] * x
          + cw[3:4] * pad_ref[SUBLANES + 1:SUBLANES + 1 + seq_len, :]
          + cb_ref[...])
    pre = _dot(xc.astype(BF16), wbd_ref[...]) + lb_ref[...]
    for d in range(2):
        r_gate = _sigmoid(pre[:, (2 * d) * GROUP_W:(2 * d + 1) * GROUP_W])
        i_gate = _sigmoid(pre[:, (2 * d + 1) * GROUP_W:(2 * d + 2) * GROUP_W])
        log_a = -LRU_C * r_gate * jax.nn.softplus(-lam_ref[d:d + 1, :])
        a_val = jnp.exp(log_a)
        one_minus_a2 = -jnp.tanh(log_a) * (a_val * a_val + 1.0)
        root = jnp.where(one_minus_a2 > 0.0, one_minus_a2 * lax.rsqrt(one_minus_a2), 0.0)
        u_val = root * (i_gate * xc)
        for hh in range(2):
            a_ref[d, hh] = a_val[:, hh * LANES:(hh + 1) * LANES]
            u_ref[d, hh] = u_val[:, hh * LANES:(hh + 1) * LANES]

    def slab(ref, d, r):
        return jnp.concatenate([ref[d, hh, pl.ds(r, n0, stride=SUBLANES), :] for hh in range(2)], axis=1)

    finals = []
    for d in range(2):
        a_slabs = [slab(a_ref, d, r) for r in range(SUBLANES)]
        u_slabs = [slab(u_ref, d, r) for r in range(SUBLANES)]
        h0 = h0_ref[d:d + 1, :] if has_state else jnp.zeros((1, GROUP_W), F32)
        h_slabs, fin = _scan_levels(a_slabs, u_slabs, pa_ref, pb_ref, sa_ref, sb_ref, h0, n0, reverse=(d == 1))
        finals.append(fin)
        for r in range(SUBLANES):
            if d == 0:
                hf_ref[r] = h_slabs[r]
            else:
                h_sum = hf_ref[r] + h_slabs[r]
                for hh in range(2):
                    hs_ref[hh, pl.ds(r, n0, stride=SUBLANES), :] = h_sum[:, hh * LANES:(hh + 1) * LANES]
    h_both = jnp.concatenate([hs_ref[0], hs_ref[1]], axis=1)
    y = h_both * jax.nn.gelu(zl_ref[:, GROUP_W:2 * GROUP_W])
    y_ref[:, GROUP_W:2 * GROUP_W] = y.astype(BF16)
    if not has_state:
        slru_ref[0:1, :] = finals[0]
        slru_ref[1:2, :] = finals[1]


def _recurrent(scal, zr, zl, lp, l, seq_len, state=None):
    t = zr.shape[0]
    b = t // seq_len
    n0 = seq_len // SUBLANES
    has_state = state is not None
    assert has_state or seq_len == RET_CHUNK
    zr3 = zr.reshape(b, seq_len, 4 * GROUP_W)
    zl3 = zl.reshape(b, seq_len, 2 * GROUP_W)
    full = lambda a: pl.BlockSpec(a.shape, lambda i: (0,) * a.ndim)
    in_specs = [
        pl.BlockSpec(memory_space=pltpu.SMEM),
        pl.BlockSpec((None, seq_len, 4 * GROUP_W), lambda i: (i, 0, 0)),
        pl.BlockSpec((None, seq_len, 2 * GROUP_W), lambda i: (i, 0, 0)),
    ]
    args = [scal, zr3, zl3]
    if has_state:
        s0, h0 = state
        heads = GROUP_W // HEAD_DIM
        in_specs += [pl.BlockSpec((None, None, 2, heads, HEAD_DIM, HEAD_DIM), lambda i: (i, l, 0, 0, 0, 0)),
                     pl.BlockSpec((None, None, 2, GROUP_W), lambda i: (i, l, 0, 0))]
        args += [s0, h0]
    params = [lp["lg_lane"], lp["gn_g"], lp["conv_w"], lp["conv_b"], lp["lru_wbd"], lp["lru_bias"], lp["lru_lam"]]
    in_specs += [full(a) for a in params]
    args += params
    out_specs = [pl.BlockSpec((None, seq_len, 2 * GROUP_W), lambda i: (i, 0, 0))]
    out_shape = [jax.ShapeDtypeStruct((b, seq_len, 2 * GROUP_W), BF16)]
    if not has_state:
        heads = GROUP_W // HEAD_DIM
        out_specs += [pl.BlockSpec((None, 2, heads, HEAD_DIM, HEAD_DIM), lambda i: (i, 0, 0, 0, 0)),
                      pl.BlockSpec((None, 2, GROUP_W), lambda i: (i, 0, 0))]
        out_shape += [jax.ShapeDtypeStruct((b, 2, heads, HEAD_DIM, HEAD_DIM), F32),
                      jax.ShapeDtypeStruct((b, 2, GROUP_W), F32)]
    scratch = [
        pltpu.VMEM((GROUP_W // HEAD_DIM, RET_CHUNK, RET_CHUNK), F32),
        pltpu.VMEM((4, RET_CHUNK, GROUP_W), F32),
        pltpu.VMEM((seq_len + 2 * SUBLANES, GROUP_W), F32),
        pltpu.VMEM((2, 2, seq_len, LANES), F32),
        pltpu.VMEM((2, 2, seq_len, LANES), F32),
        pltpu.VMEM((SUBLANES, n0, GROUP_W), F32),
        pltpu.VMEM((2, seq_len, LANES), F32),
        pltpu.VMEM((SUBLANES, n0, GROUP_W), F32),
        pltpu.VMEM((SUBLANES, n0, GROUP_W), F32),
        pltpu.VMEM((2 * n0, GROUP_W), F32),
        pltpu.VMEM((2 * n0, GROUP_W), F32),
    ]
    outs = pl.pallas_call(
        functools.partial(_recurrent_kernel, seq_len=seq_len, has_state=has_state),
        grid=(b,),
        in_specs=in_specs,
        out_specs=out_specs,
        out_shape=out_shape,
        scratch_shapes=scratch,
        compiler_params=pltpu.CompilerParams(
            dimension_semantics=("arbitrary",), vmem_limit_bytes=VMEM_LIMIT),
        name="recurrent_latent" if has_state else "recurrent_ctx",
    )(*args)
    return (outs[0].reshape(t, 2 * GROUP_W),) + tuple(outs[1:])


def _dup_half(x, half):
    sw = pltpu.roll(x, HEAD_DIM, axis=1)
    low = _lane(x.shape) < HEAD_DIM
    return jnp.where(low, x, sw) if half == 0 else jnp.where(low, sw, x)


def _attention_kernel(*refs, seq_len, has_ctx, seqs_per_step):
    n_in = 9 if has_ctx else 5
    scal_ref, ng_ref = refs[0], refs[n_in - 1]
    per_seq = refs[1:n_in - 1] + refs[n_in:]
    per_sequence = [_attention_stages(scal_ref, ng_ref, *[r.at[s] for r in per_seq], seq_len=seq_len,
                                      has_ctx=has_ctx) for s in range(seqs_per_step)]
    stages = [st for seq_stages in per_sequence for st in seq_stages]
    pending = []
    for make_weights, make_out, p in stages:
        pending.append((make_out, p, make_weights(p)))
        if len(pending) > (ATTN_LOOKAHEAD_LATENT if has_ctx else ATTN_LOOKAHEAD_CTX):
            make_out_, p_, w_ = pending.pop(0)
            make_out_(p_, w_)
    for make_out_, p_, w_ in pending:
        make_out_(p_, w_)


def _attention_stages(scal_ref, ng_ref, *refs, seq_len, has_ctx):
    if has_ctx:
        (qa_ref, kvs_ref, kvd_ref, cks_ref, cvs_ref, ckd_ref, cvd_ref, y_ref,
         sk_s, sv_s, dk_s, dv_s, csk_s, csv_s, cdk_s, cdv_s) = refs
    else:
        (qa_ref, kvs_ref, kvd_ref, y_ref, sk_s, sv_s, dk_s, dv_s) = refs
    tq = ATTN_Q_TILE
    heads = GROUP_W // HEAD_DIM
    lane = _lane((1, LANES))
    low = lane < HEAD_DIM
    halves = (low, ~low)

    def build():
        def fill(k_all, v_all, k_dst, v_dst, dup):
            for p in range(2):
                if dup:
                    k_p, v_p = _dup_half(k_all, p), _dup_half(v_all, p)
                else:
                    k_p, v_p = k_all[:, p * LANES:(p + 1) * LANES], v_all[:, p * LANES:(p + 1) * LANES]
                k_dst[p] = k_p.astype(BF16)
                for j in range(2):
                    v_dst[2 * p + j] = jnp.where(halves[j], v_p, 1.0).astype(BF16)

        fill(kvs_ref[:, 0:LANES].astype(F32), kvs_ref[:, LANES:2 * LANES].astype(F32), sk_s, sv_s, True)
        fill(kvd_ref[:, 0:GROUP_W].astype(F32), kvd_ref[:, GROUP_W:2 * GROUP_W].astype(F32), dk_s, dv_s, False)
        if has_ctx:
            fill(cks_ref[...], cvs_ref[...], csk_s, csv_s, True)
            fill(ckd_ref[...], cvd_ref[...], cdk_s, cdv_s, False)

    if seq_len == tq:
        build()
    else:
        pl.when(pl.program_id(1) == 0)(build)

    def weights(q_rows, k_loc, k_ctx, valid, extra2):
        s = _dot_nt(q_rows, k_loc)
        if valid is not None:
            s = jnp.where(valid, s, NEG_BIG)
        m = jnp.max(s, -1, keepdims=True)
        sc = None
        if k_ctx is not None:
            sc = _dot_nt(q_rows, k_ctx)
            m = jnp.maximum(m, jnp.max(sc, -1, keepdims=True))
        if extra2 is not None:
            m = jnp.maximum(m, extra2)
        e = jnp.exp2((s - m).astype(BF16))
        ec = None if sc is None else jnp.exp2((sc - m).astype(BF16))
        extra = None if extra2 is None else jnp.exp2(extra2 - m)
        return e, ec, extra

    def weighted_values(w, v_of_head):
        e, ec, extra = w
        half_rows = e.shape[0] // 2
        res = []
        for j in range(2):
            rs = slice(j * half_rows, (j + 1) * half_rows)
            v_loc, v_ctx = v_of_head(j)
            o = _dot(e[rs], v_loc)
            if ec is not None:
                o = o + _dot(ec[rs], v_ctx)
            den = pltpu.roll(o, HEAD_DIM, axis=1)
            if extra is not None:
                den = den + extra[rs]
            res.append(o / jnp.where(halves[j], den, 1.0))
        return res

    if has_ctx:
        win = 2 * tq
        i = pl.program_id(1)
        blk = jnp.clip(2 * i - 1, 0, (seq_len - win) // SWA_WINDOW)
        start = pl.multiple_of(blk * SWA_WINDOW, SWA_WINDOW)
        rows = pl.ds(start, win)
        qpos = i * tq + (lax.broadcasted_iota(jnp.int32, (2 * tq, win), 0) & (tq - 1))
        kpos = start + lax.broadcasted_iota(jnp.int32, (2 * tq, win), 1)
        valid = jnp.abs(kpos - qpos) <= SWA_WINDOW
    else:
        rows = slice(None)
        valid = None
    first_member = lax.broadcasted_iota(jnp.int32, (2 * tq, 1), 0) < tq
    lam = scal_ref[3 * heads]
    coef = scal_ref[3 * heads + 1]
    grp = lane >> 5

    def swa_weights(p):
        q = qa_ref[:, p * LANES:(p + 1) * LANES]
        q_rows = jnp.concatenate([jnp.where(halves[j], q, 0.0).astype(BF16) for j in range(2)], axis=0)
        sink2 = jnp.where(first_member, scal_ref[2 * heads + 2 * p], scal_ref[2 * heads + 2 * p + 1]) * LOG2E
        return weights(q_rows, sk_s[p, rows, :], csk_s[p] if has_ctx else None, valid, sink2)

    def swa_out(p, w):
        outs = weighted_values(w, lambda j: (sv_s[2 * p + j, rows, :], csv_s[2 * p + j] if has_ctx else None))
        y_ref[:, p * LANES:(p + 1) * LANES] = jnp.where(low, outs[0], outs[1]).astype(BF16)

    def diff_weights(p):
        q = qa_ref[:, GROUP_W + p * LANES:GROUP_W + (p + 1) * LANES]
        q_rows = jnp.concatenate([jnp.where(grp == g, q, 0.0).astype(BF16) for g in range(4)], axis=0)
        return weights(q_rows, dk_s[p], cdk_s[p] if has_ctx else None, None, None)

    def diff_out(p, w):
        parts = weighted_values(w, lambda j: (dv_s[2 * p + j], cdv_s[2 * p + j] if has_ctx else None))
        outs = [r[0:tq] - lam * r[tq:2 * tq] for r in parts]
        o = jnp.where(low, outs[0], outs[1])
        sq = o * o
        ms = jnp.where(low, jnp.sum(jnp.where(low, sq, 0.0), -1, keepdims=True),
                       jnp.sum(jnp.where(low, 0.0, sq), -1, keepdims=True)) * (1.0 / HEAD_DIM)
        y = o * lax.rsqrt(ms + EPS) * ng_ref[:, p * LANES:(p + 1) * LANES] * coef
        y_ref[:, GROUP_W + p * LANES:GROUP_W + (p + 1) * LANES] = y.astype(BF16)

    return [(swa_weights, swa_out, 0), (diff_weights, diff_out, 0),
            (swa_weights, swa_out, 1), (diff_weights, diff_out, 1)]


def _attention(scal, qa, kvs, kvd, norm_g_tiled, l, seq_len, caches=None):
    t = qa.shape[0]
    b = t // seq_len
    tq = ATTN_Q_TILE
    nq = seq_len // tq
    has_ctx = caches is not None
    ns = 1 if has_ctx else CTX_SEQS_PER_STEP
    in_specs = [
        pl.BlockSpec(memory_space=pltpu.SMEM),
        pl.BlockSpec((ns, tq, 2 * GROUP_W), lambda i, j: (i, j, 0)),
        pl.BlockSpec((ns, seq_len, GROUP_W), lambda i, j: (i, 0, 0)),
        pl.BlockSpec((ns, seq_len, 2 * GROUP_W), lambda i, j: (i, 0, 0)),
    ]
    args = [scal, qa.reshape(b, seq_len, 2 * GROUP_W), kvs.reshape(b, seq_len, GROUP_W),
            kvd.reshape(b, seq_len, 2 * GROUP_W)]
    heads = GROUP_W // HEAD_DIM
    scratch = [pltpu.VMEM((ns, 2, seq_len, LANES), BF16), pltpu.VMEM((ns, heads, seq_len, LANES), BF16),
               pltpu.VMEM((ns, 2, seq_len, LANES), BF16), pltpu.VMEM((ns, heads, seq_len, LANES), BF16)]
    if has_ctx:
        past = caches[0].shape[2]
        for a in caches:
            in_specs.append(pl.BlockSpec((ns, None) + a.shape[2:], lambda i, j: (i, l, 0, 0)))
            args.append(a)
        scratch += [pltpu.VMEM((ns, 2, past, LANES), BF16), pltpu.VMEM((ns, heads, past, LANES), BF16),
                    pltpu.VMEM((ns, 2, past, LANES), BF16), pltpu.VMEM((ns, heads, past, LANES), BF16)]
    in_specs.append(pl.BlockSpec(norm_g_tiled.shape, lambda i, j: (0, 0)))
    args.append(norm_g_tiled)
    y = pl.pallas_call(
        functools.partial(_attention_kernel, seq_len=seq_len, has_ctx=has_ctx, seqs_per_step=ns),
        grid=(b // ns, nq),
        in_specs=in_specs,
        out_specs=pl.BlockSpec((ns, tq, 2 * GROUP_W), lambda i, j: (i, j, 0)),
        out_shape=jax.ShapeDtypeStruct((b, seq_len, 2 * GROUP_W), BF16),
        scratch_shapes=scratch,
        compiler_params=pltpu.CompilerParams(
            dimension_semantics=("arbitrary", "arbitrary"), vmem_limit_bytes=VMEM_LIMIT),
        name="attention_latent" if has_ctx else "attention_ctx",
    )(*args)
    return y.reshape(t, 2 * GROUP_W)


def _out_mlp_kernel(x_ref, yr_ref, ya_ref, mod_ref, wo_ref, g_ref, w1_ref, w2_ref, gf_ref, o_ref, *, final):
    g1 = mod_ref[:, 2 * D_MODEL:3 * D_MODEL]
    sh2 = mod_ref[:, 3 * D_MODEL:4 * D_MODEL]
    sc2 = mod_ref[:, 4 * D_MODEL:5 * D_MODEL]
    g2 = mod_ref[:, 5 * D_MODEL:6 * D_MODEL]
    y = _dot(yr_ref[...], wo_ref[0:2 * GROUP_W, :]) + _dot(ya_ref[...], wo_ref[2 * GROUP_W:4 * GROUP_W, :])
    x = x_ref[...] + g1 * y
    h = (_rms(x) * g_ref[...] * (1.0 + sc2) + sh2).astype(BF16)
    acc = None
    for c in range(D_FF // D_MODEL):
        sl = slice(c * D_MODEL, (c + 1) * D_MODEL)
        hid = jnp.square(jnp.maximum(_dot(h, w1_ref[:, sl]), 0.0)).astype(BF16)
        part = _dot(hid, w2_ref[sl, :])
        acc = part if acc is None else acc + part
    x = x + g2 * acc
    if final:
        x = _rms(x) * gf_ref[...]
    o_ref[...] = x


def _out_mlp(x, y_rec, y_att, mod4, w_out_b, norm_g, w_ff1_b, w_ff2_b, final_g, l, seq_len, is_latent, final):
    t = x.shape[0]
    tm = MLP_ROW_TILE
    if is_latent:
        tiles_per_seq = seq_len // tm
        mod_map = lambda i: (l, 1 + i // tiles_per_seq, 0, 0)
    else:
        mod_map = lambda i: (l, 0, 0, 0)
    once = pl.Buffered(1)
    in_specs = [
        pl.BlockSpec((tm, D_MODEL), lambda i: (i, 0)),
        pl.BlockSpec((tm, 2 * GROUP_W), lambda i: (i, 0)),
        pl.BlockSpec((tm, 2 * GROUP_W), lambda i: (i, 0)),
        pl.BlockSpec((None, None, 1, N_MOD * D_MODEL), mod_map),
        pl.BlockSpec((None, D_MODEL, D_MODEL), lambda i: (l, 0, 0), pipeline_mode=once),
        pl.BlockSpec((None, 1, D_MODEL), lambda i: (l, 0, 0)),
        pl.BlockSpec((None, D_MODEL, D_FF), lambda i: (l, 0, 0), pipeline_mode=once),
        pl.BlockSpec((None, D_FF, D_MODEL), lambda i: (l, 0, 0), pipeline_mode=once),
        pl.BlockSpec((1, D_MODEL), lambda i: (0, 0)),
    ]
    return pl.pallas_call(
        functools.partial(_out_mlp_kernel, final=final),
        grid=(t // tm,),
        in_specs=in_specs,
        out_specs=pl.BlockSpec((tm, D_MODEL), lambda i: (i, 0)),
        out_shape=jax.ShapeDtypeStruct((t, D_MODEL), F32),
        compiler_params=pltpu.CompilerParams(
            dimension_semantics=("arbitrary",), vmem_limit_bytes=VMEM_LIMIT),
        name="out_mlp_final" if final else "out_mlp",
    )(x, y_rec, y_att, mod4, w_out_b, norm_g, w_ff1_b, w_ff2_b, final_g)


def _rope_tables(length):
    row = (jnp.arange(length) // GRID_W).astype(F32)
    col = (jnp.arange(length) % GRID_W).astype(F32)
    tabs = []
    for dim in (HEAD_DIM, DIFF_QK_DIM):
        n = dim // 4
        inv = ROPE_BASE ** (-jnp.arange(n, dtype=F32) / n)
        ang = jnp.concatenate([row[:, None] * inv, col[:, None] * inv], -1)
        cos, sin = jnp.cos(ang), jnp.sin(ang)
        reps = LANES // dim
        tabs.append(jnp.tile(jnp.concatenate([cos, cos], -1), (1, reps)))
        tabs.append(jnp.tile(jnp.concatenate([-sin, sin], -1), (1, reps)))
    return tabs


def _block_diag(w):
    n, c, d = w.shape
    eye = jnp.eye(n, dtype=w.dtype)
    return jnp.einsum("ncd,nm->ncmd", w, eye).reshape(n * c, n * d)


def _layer_params(l, ret_decay, ret_gn_g, lru_conv_w, lru_conv_b, lru_w_a, lru_b_a, lru_w_x, lru_b_x, lru_lambda,
                  swa_sink, diff_lambda, diff_norm_g):
    heads = GROUP_W // HEAD_DIM
    log_gamma = jax.nn.log_sigmoid(ret_decay[l].astype(F32))
    lambda_init = 0.8 - 0.6 * math.exp(-0.3 * l)
    lv = diff_lambda[l].astype(F32)
    lam = jnp.exp(jnp.sum(lv[0] * lv[1])) - jnp.exp(jnp.sum(lv[2] * lv[3])) + lambda_init
    scal = jnp.concatenate([log_gamma[0], log_gamma[1], swa_sink[l].astype(F32),
                            jnp.stack([lam, jnp.asarray(1.0 - lambda_init, F32)]),
                            jnp.zeros((2,), F32)])
    wbd = jnp.concatenate([_block_diag(lru_w_a[l, 0]), _block_diag(lru_w_x[l, 0]),
                           _block_diag(lru_w_a[l, 1]), _block_diag(lru_w_x[l, 1])], axis=1).astype(BF16)
    bias = jnp.concatenate([lru_b_a[l, 0], lru_b_x[l, 0], lru_b_a[l, 1], lru_b_x[l, 1]])[None, :]
    return dict(
        scal=scal,
        lg_lane=jnp.repeat(log_gamma, HEAD_DIM, axis=1),
        gn_g=ret_gn_g[l][None, :],
        conv_w=lru_conv_w[l],
        conv_b=lru_conv_b[l][None, :],
        lru_wbd=wbd,
        lru_bias=bias,
        lru_lam=lru_lambda[l],
        diff_ng=jnp.tile(diff_norm_g[l], heads)[None, :],
    )


def kernel(x_prompt, x_sample, c, state_ret, state_lru, cache_swa_k, cache_swa_v, cache_diff_k, cache_diff_v,
           c_ctx, w_ada, b_ada, norm_mix_g, w_in, ret_decay, ret_gn_g, lru_conv_w, lru_conv_b,
           lru_w_a, lru_b_a, lru_w_x, lru_b_x, lru_lambda, swa_sink, diff_lambda, diff_norm_g,
           w_out, norm_mlp_g, w_ff1, w_ff2, final_norm_g):
    batch, seq, _ = x_prompt.shape
    dec_batch, dec_seq, _ = x_sample.shape
    past = cache_swa_k.shape[2]

    cvec = jnp.concatenate([c_ctx[None, :], c, jnp.zeros((MOD_ROWS - 1 - dec_batch, D_MODEL), F32)], axis=0)
    mod4 = _modulation(cvec, w_ada, b_ada).reshape(DEPTH, MOD_ROWS, 1, N_MOD * D_MODEL)

    w_in_b = w_in.astype(BF16)
    w_out_b = w_out.astype(BF16)
    w_ff1_b = w_ff1.astype(BF16)
    w_ff2_b = w_ff2.astype(BF16)
    norm_mix3 = norm_mix_g.reshape(DEPTH, 1, D_MODEL)
    norm_mlp3 = norm_mlp_g.reshape(DEPTH, 1, D_MODEL)
    final_g = final_norm_g[None, :]
    rope_tabs = _rope_tables(dec_seq)
    layers = [_layer_params(l, ret_decay, ret_gn_g, lru_conv_w, lru_conv_b, lru_w_a, lru_b_a, lru_w_x, lru_b_x,
                            lru_lambda, swa_sink, diff_lambda, diff_norm_g) for l in range(DEPTH)]

    heads = GROUP_W // HEAD_DIM

    xp = x_prompt.reshape(batch * seq, D_MODEL)
    new_ret, new_lru, new_swa, new_diff = [], [], [], []
    for l in range(DEPTH):
        lp = layers[l]
        zr, zl, qa, kvs, kvd = _in_proj(xp, mod4, norm_mix3, w_in_b, l, seq, None)
        y_rec, s_ret, s_lru = _recurrent(lp["scal"], zr, zl, lp, l, seq)
        y_att = _attention(lp["scal"], qa, kvs, kvd, lp["diff_ng"], l, seq)
        xp = _out_mlp(xp, y_rec, y_att, mod4, w_out_b, norm_mlp3, w_ff1_b, w_ff2_b, final_g,
                      l, seq, False, l == DEPTH - 1)
        new_ret.append(s_ret)
        new_lru.append(s_lru)
        new_swa.append(kvs)
        new_diff.append(kvd)
    y_prompt = xp.reshape(batch, seq, D_MODEL)
    new_ret = jnp.stack(new_ret, axis=1)
    new_swa = jnp.stack(new_swa, axis=1).reshape(batch, seq, DEPTH, 2, 2, HEAD_DIM)
    new_swa = jnp.transpose(new_swa, (3, 0, 2, 1, 4, 5))
    new_diff = jnp.stack(new_diff, axis=1).reshape(batch, seq, DEPTH, 2, heads, HEAD_DIM)
    new_diff = jnp.transpose(new_diff, (3, 0, 2, 1, 4, 5))

    xs = x_sample.reshape(dec_batch * dec_seq, D_MODEL)
    state = (state_ret, state_lru)
    caches = (cache_swa_k.reshape(dec_batch, DEPTH, past, 2 * HEAD_DIM),
              cache_swa_v.reshape(dec_batch, DEPTH, past, 2 * HEAD_DIM),
              cache_diff_k.reshape(dec_batch, DEPTH, past, GROUP_W),
              cache_diff_v.reshape(dec_batch, DEPTH, past, GROUP_W))
    for l in range(DEPTH):
        lp = layers[l]
        zr, zl, qa, kvs, kvd = _in_proj(xs, mod4, norm_mix3, w_in_b, l, dec_seq, rope_tabs)
        (y_rec,) = _recurrent(lp["scal"], zr, zl, lp, l, dec_seq, state)
        y_att = _attention(lp["scal"], qa, kvs, kvd, lp["diff_ng"], l, dec_seq, caches)
        xs = _out_mlp(xs, y_rec, y_att, mod4, w_out_b, norm_mlp3, w_ff1_b, w_ff2_b, final_g,
                      l, dec_seq, True, l == DEPTH - 1)
    y_sample = xs.reshape(dec_batch, dec_seq, D_MODEL)

    return (y_prompt, y_sample, new_ret, jnp.stack(new_lru, axis=1),
            new_swa[0], new_swa[1], new_diff[0], new_diff[1])
```

```python
import functools
import math

import jax
import jax.numpy as jnp
from jax import lax
from jax.experimental import pallas as pl
from jax.experimental.pallas import tpu as pltpu

F32 = jnp.float32
BF16 = jnp.bfloat16

D_MODEL = 1024
DEPTH = 4
GRID_W = 64
HEAD_DIM = 64
GROUP_W = D_MODEL // 4
RET_CHUNK = 256
LRU_BLOCKS = 4
LRU_C = 8.0
SWA_WINDOW = 128
DIFF_QK_DIM = HEAD_DIM // 2
D_FF = 4 * D_MODEL
ROPE_BASE = 10000.0
EPS = 1e-6
N_MOD = 6
D_IN = 11 * GROUP_W
MOD_ROWS = 16

LANES = 128
SUBLANES = 8
VMEM_LIMIT = 56 * 1024 * 1024
NEG_BIG = -1e30
LOG2E = math.log2(math.e)
SWA_QSCALE = (HEAD_DIM ** -0.5) * LOG2E
DIFF_QSCALE = (DIFF_QK_DIM ** -0.5) * LOG2E

ROW_TILE = 512
MLP_ROW_TILE = 1024
ATTN_Q_TILE = 256
CTX_SEQS_PER_STEP = 8
ATTN_LOOKAHEAD_LATENT = 2
ATTN_LOOKAHEAD_CTX = 1


def _dot(a, b):
    return jnp.dot(a, b, preferred_element_type=F32)


def _dot_nt(a, b):
    return lax.dot_general(a, b, (((1,), (1,)), ((), ())), preferred_element_type=F32)


def _dot_tn(a, b):
    return lax.dot_general(a, b, (((0,), (0,)), ((), ())), preferred_element_type=F32)


def _rms(x):
    return x * lax.rsqrt(jnp.mean(x * x, axis=-1, keepdims=True) + EPS)


def _sigmoid(x):
    return 0.5 * jnp.tanh(0.5 * x) + 0.5


def _lane(shape):
    return lax.broadcasted_iota(jnp.int32, shape, len(shape) - 1)


def _mod_kernel(c_ref, w_ref, b_ref, o_ref):
    s = jax.nn.silu(c_ref[...]).astype(BF16)
    o_ref[...] = _dot(s, w_ref[...].astype(BF16)) + b_ref[...]


def _modulation(cvec, w_ada, b_ada):
    tn = 1536
    n_mod = N_MOD * D_MODEL
    return pl.pallas_call(
        _mod_kernel,
        grid=(DEPTH, n_mod // tn),
        in_specs=[
            pl.BlockSpec((MOD_ROWS, D_MODEL), lambda l, j: (0, 0)),
            pl.BlockSpec((None, D_MODEL, tn), lambda l, j: (l, 0, j)),
            pl.BlockSpec((None, 1, tn), lambda l, j: (l, 0, j)),
        ],
        out_specs=pl.BlockSpec((None, MOD_ROWS, tn), lambda l, j: (l, 0, j)),
        out_shape=jax.ShapeDtypeStruct((DEPTH, MOD_ROWS, n_mod), F32),
        compiler_params=pltpu.CompilerParams(
            dimension_semantics=("arbitrary", "arbitrary"), vmem_limit_bytes=VMEM_LIMIT),
        name="modulation",
    )(cvec, w_ada, b_ada.reshape(DEPTH, 1, n_mod))


def _rope_slab(x, c, s, half):
    first = (_lane(x.shape) & (2 * half - 1)) < half
    swapped = jnp.where(first, pltpu.roll(x, LANES - half, axis=1), pltpu.roll(x, half, axis=1))
    return x * c + swapped * s


def _in_proj_kernel(*refs, rope):
    if rope:
        (x_ref, mod_ref, g_ref, w_ref, c64_ref, s64_ref, c32_ref, s32_ref,
         zr_ref, zl_ref, qa_ref, kvs_ref, kvd_ref) = refs
    else:
        x_ref, mod_ref, g_ref, w_ref, zr_ref, zl_ref, qa_ref, kvs_ref, kvd_ref = refs
    sh1 = mod_ref[:, 0:D_MODEL]
    sc1 = mod_ref[:, D_MODEL:2 * D_MODEL]
    h = (_rms(x_ref[...]) * g_ref[...] * (1.0 + sc1) + sh1).astype(BF16)
    zs = _dot(h, w_ref[:, 6 * GROUP_W:8 * GROUP_W])
    zd = _dot(h, w_ref[:, 8 * GROUP_W:11 * GROUP_W])
    if rope:
        c64, s64, c32, s32 = c64_ref[...], s64_ref[...], c32_ref[...], s32_ref[...]
        rope_s = lambda v: _rope_slab(v, c64, s64, HEAD_DIM // 2)
        rope_d = lambda v: _rope_slab(v, c32, s32, DIFF_QK_DIM // 2)
    else:
        rope_s = rope_d = lambda v: v
    out = qa_ref.dtype
    for i in range(2):
        sl = slice(i * LANES, (i + 1) * LANES)
        qa_ref[:, sl] = (rope_s(zs[:, sl]) * SWA_QSCALE).astype(out)
        qa_ref[:, GROUP_W + i * LANES:GROUP_W + (i + 1) * LANES] = (rope_d(zd[:, sl]) * DIFF_QSCALE).astype(out)
        kvd_ref[:, sl] = rope_d(zd[:, GROUP_W + i * LANES:GROUP_W + (i + 1) * LANES]).astype(out)
    kvs_ref[:, 0:LANES] = rope_s(zs[:, 2 * LANES:3 * LANES]).astype(out)
    kvs_ref[:, LANES:2 * LANES] = zs[:, 3 * LANES:4 * LANES].astype(out)
    kvd_ref[:, GROUP_W:2 * GROUP_W] = zd[:, 2 * GROUP_W:3 * GROUP_W].astype(out)
    zl_ref[...] = _dot(h, w_ref[:, 4 * GROUP_W:6 * GROUP_W])
    zr_ref[...] = _dot(h, w_ref[:, 0:4 * GROUP_W])


def _in_proj(x, mod4, norm_g, w_in_b, l, seq_len, rope_tabs):
    t = x.shape[0]
    tm = ROW_TILE
    rope = rope_tabs is not None
    if rope:
        tiles_per_seq = seq_len // tm
        mod_map = lambda i: (l, 1 + i // tiles_per_seq, 0, 0)
    else:
        mod_map = lambda i: (l, 0, 0, 0)
    in_specs = [
        pl.BlockSpec((tm, D_MODEL), lambda i: (i, 0)),
        pl.BlockSpec((None, None, 1, N_MOD * D_MODEL), mod_map),
        pl.BlockSpec((None, 1, D_MODEL), lambda i: (l, 0, 0)),
        pl.BlockSpec((None, D_MODEL, D_IN), lambda i: (l, 0, 0)),
    ]
    args = [x, mod4, norm_g, w_in_b]
    if rope:
        tab_spec = pl.BlockSpec((tm, LANES), lambda i: (i % tiles_per_seq, 0))
        in_specs += [tab_spec] * 4
        args += list(rope_tabs)
    att = BF16 if rope else F32
    outs = ((4 * GROUP_W, F32), (2 * GROUP_W, F32), (2 * GROUP_W, att), (GROUP_W, att), (2 * GROUP_W, att))
    return pl.pallas_call(
        functools.partial(_in_proj_kernel, rope=rope),
        grid=(t // tm,),
        in_specs=in_specs,
        out_specs=[pl.BlockSpec((tm, w), lambda i: (i, 0)) for w, _ in outs],
        out_shape=[jax.ShapeDtypeStruct((t, w), dt) for w, dt in outs],
        compiler_params=pltpu.CompilerParams(
            dimension_semantics=("arbitrary",), vmem_limit_bytes=VMEM_LIMIT),
        name="in_proj_rope" if rope else "in_proj",
    )(*args)


def _scan_levels(a_slabs, u_slabs, pa_ref, pb_ref, sa_ref, sb_ref, h0, n0, reverse):
    order = list(range(SUBLANES - 1, -1, -1)) if reverse else list(range(SUBLANES))
    acc_a = acc_b = None
    for r in order:
        if acc_a is None:
            acc_a, acc_b = a_slabs[r], u_slabs[r]
        else:
            acc_b = a_slabs[r] * acc_b + u_slabs[r]
            acc_a = a_slabs[r] * acc_a
        pa_ref[r] = acc_a
        pb_ref[r] = acc_b
    pad = n0 // 2
    ident = pad + n0 if reverse else 0
    sa_ref[ident:ident + pad, :] = jnp.ones((pad, GROUP_W), F32)
    sb_ref[ident:ident + pad, :] = jnp.zeros((pad, GROUP_W), F32)
    sign = 1 if reverse else -1
    k = 1
    while k < n0:
        sa_ref[pad:pad + n0, :] = acc_a
        sb_ref[pad:pad + n0, :] = acc_b
        off = pad + sign * k
        acc_b = acc_a * sb_ref[off:off + n0, :] + acc_b
        acc_a = acc_a * sa_ref[off:off + n0, :]
        k *= 2
    sa_ref[pad:pad + n0, :] = acc_a
    sb_ref[pad:pad + n0, :] = acc_b
    off = pad + sign
    carry_in = sa_ref[off:off + n0, :] * h0 + sb_ref[off:off + n0, :]
    end = 0 if reverse else n0 - 1
    final = acc_a[end:end + 1, :] * h0 + acc_b[end:end + 1, :]
    h_slabs = [pa_ref[r] * carry_in + pb_ref[r] for r in range(SUBLANES)]
    return h_slabs, final


def _recurrent_kernel(*refs, seq_len, has_state):
    if has_state:
        (scal_ref, zr_ref, zl_ref, s0_ref, h0_ref, lgl_ref, gng_ref, cw_ref, cb_ref, wbd_ref, lb_ref, lam_ref,
         y_ref,
         d_ref, vec_ref, pad_ref, a_ref, u_ref, hf_ref, hs_ref, pa_ref, pb_ref, sa_ref, sb_ref) = refs
    else:
        (scal_ref, zr_ref, zl_ref, lgl_ref, gng_ref, cw_ref, cb_ref, wbd_ref, lb_ref, lam_ref,
         y_ref, sret_ref, slru_ref,
         d_ref, vec_ref, pad_ref, a_ref, u_ref, hf_ref, hs_ref, pa_ref, pb_ref, sa_ref, sb_ref) = refs
    ch = RET_CHUNK
    n_chunks = seq_len // ch
    n0 = seq_len // SUBLANES
    heads = GROUP_W // HEAD_DIM

    @pl.when(pl.program_id(0) == 0)
    def _():
        n = lax.broadcasted_iota(jnp.int32, (ch, ch), 0)
        m = lax.broadcasted_iota(jnp.int32, (ch, ch), 1)
        dist = (n - m).astype(F32)
        for h in range(heads):
            lg = jnp.where(dist > 0, scal_ref[h], scal_ref[heads + h])
            d_ref[h] = jnp.where(dist == 0, 2.0, jnp.exp(jnp.abs(dist) * lg))
        t = lax.broadcasted_iota(jnp.int32, (ch, GROUP_W), 0).astype(F32)
        lgf = lgl_ref[0:1, :]
        lgb = lgl_ref[1:2, :]
        vec_ref[0] = jnp.exp((t + 1.0) * lgf)
        vec_ref[1] = jnp.exp((ch - t) * lgb)
        vec_ref[2] = jnp.exp((ch - 1.0 - t) * lgf)
        vec_ref[3] = jnp.exp(t * lgb)

    low = _lane((1, LANES)) < HEAD_DIM
    rr = lax.broadcasted_iota(jnp.int32, (LANES, LANES), 0) < HEAD_DIM
    cc = lax.broadcasted_iota(jnp.int32, (LANES, LANES), 1) < HEAD_DIM
    block_diag = (rr == cc).astype(F32)
    g_chunk_f = jnp.exp(ch * lgl_ref[0:1, :])
    g_chunk_b = jnp.exp(ch * lgl_ref[1:2, :])

    def kv(c, p):
        rows = slice(c * ch, (c + 1) * ch)
        k = zr_ref[rows, GROUP_W + p * LANES:GROUP_W + (p + 1) * LANES] * (HEAD_DIM ** -0.5)
        v = zr_ref[rows, 2 * GROUP_W + p * LANES:2 * GROUP_W + (p + 1) * LANES].astype(BF16)
        return k, v

    def state_update(k, v, zeta):
        return _dot_tn((k * zeta).astype(BF16), v) * block_diag

    def pair_state(d, p):
        zero = jnp.zeros((HEAD_DIM, HEAD_DIM), F32)
        top = jnp.concatenate([s0_ref[d, 2 * p], zero], axis=1)
        bottom = jnp.concatenate([zero, s0_ref[d, 2 * p + 1]], axis=1)
        return jnp.concatenate([top, bottom], axis=0)

    def store_state(d, p, s_pair):
        sret_ref[d, 2 * p] = s_pair[0:HEAD_DIM, 0:HEAD_DIM]
        sret_ref[d, 2 * p + 1] = s_pair[HEAD_DIM:LANES, HEAD_DIM:LANES]

    for p in range(2):
        pl_sl = slice(p * LANES, (p + 1) * LANES)
        if has_state:
            s_b = [None] * n_chunks
            s_b[n_chunks - 1] = pair_state(1, p)
            for c in range(n_chunks - 1, 0, -1):
                k, v = kv(c, p)
                s_b[c - 1] = s_b[c] * g_chunk_b[:, pl_sl] + state_update(k, v, vec_ref[3, :, pl_sl])
            s_f = pair_state(0, p)
        for c in range(n_chunks):
            rows = slice(c * ch, (c + 1) * ch)
            q = zr_ref[rows, pl_sl]
            k, v = kv(c, p)
            kb = k.astype(BF16)
            outs = []
            for j in range(2):
                qm = jnp.where(low if j == 0 else ~low, q, 0.0).astype(BF16)
                w = (_dot_nt(qm, kb) * d_ref[2 * p + j]).astype(BF16)
                outs.append(_dot(w, v))
            o = jnp.where(low, outs[0], outs[1])
            if has_state:
                qb = q.astype(BF16)
                o = o + _dot(qb, s_f.astype(BF16)) * vec_ref[0, :, pl_sl]
                o = o + _dot(qb, s_b[c].astype(BF16)) * vec_ref[1, :, pl_sl]
                if c + 1 < n_chunks:
                    s_f = s_f * g_chunk_f[:, pl_sl] + state_update(k, v, vec_ref[2, :, pl_sl])
            else:
                store_state(0, p, state_update(k, v, vec_ref[2, :, pl_sl]))
                store_state(1, p, state_update(k, v, vec_ref[3, :, pl_sl]))
            inv = 1.0 / HEAD_DIM
            mu = jnp.where(low, jnp.sum(jnp.where(low, o, 0.0), -1, keepdims=True),
                           jnp.sum(jnp.where(low, 0.0, o), -1, keepdims=True)) * inv
            dlt = o - mu
            sq = dlt * dlt
            var = jnp.where(low, jnp.sum(jnp.where(low, sq, 0.0), -1, keepdims=True),
                            jnp.sum(jnp.where(low, 0.0, sq), -1, keepdims=True)) * inv
            gate = zr_ref[rows, 3 * GROUP_W + p * LANES:3 * GROUP_W + (p + 1) * LANES]
            y = dlt * lax.rsqrt(var + EPS) * gng_ref[:, pl_sl] * (gate * _sigmoid(gate))
            y_ref[rows, pl_sl] = y.astype(BF16)

    x = zl_ref[:, 0:GROUP_W]
    zero8 = jnp.zeros((SUBLANES, GROUP_W), F32)
    pad_ref[0:SUBLANES, :] = zero8
    pad_ref[SUBLANES:SUBLANES + seq_len, :] = x
    pad_ref[SUBLANES + seq_len:2 * SUBLANES + seq_len, :] = zero8
    cw = cw_ref[...]
    xc = (cw[0:1] * pad_ref[SUBLANES - 2:SUBLANES - 2 + seq_len, :]
          + cw[1:2] * pad_ref[SUBLANES - 1:SUBLANES - 1 + seq_len, :]
          + cw[2:3] * x
          + cw[3:4] * pad_ref[SUBLANES + 1:SUBLANES + 1 + seq_len, :]
          + cb_ref[...])
    pre = _dot(xc.astype(BF16), wbd_ref[...]) + lb_ref[...]
    for d in range(2):
        r_gate = _sigmoid(pre[:, (2 * d) * GROUP_W:(2 * d + 1) * GROUP_W])
        i_gate = _sigmoid(pre[:, (2 * d + 1) * GROUP_W:(2 * d + 2) * GROUP_W])
        log_a = -LRU_C * r_gate * jax.nn.softplus(-lam_ref[d:d + 1, :])
        a_val = jnp.exp(log_a)
        one_minus_a2 = -jnp.tanh(log_a) * (a_val * a_val + 1.0)
        root = jnp.where(one_minus_a2 > 0.0, one_minus_a2 * lax.rsqrt(one_minus_a2), 0.0)
        u_val = root * (i_gate * xc)
        for hh in range(2):
            a_ref[d, hh] = a_val[:, hh * LANES:(hh + 1) * LANES]
            u_ref[d, hh] = u_val[:, hh * LANES:(hh + 1) * LANES]

    def slab(ref, d, r):
        return jnp.concatenate([ref[d, hh, pl.ds(r, n0, stride=SUBLANES), :] for hh in range(2)], axis=1)

    finals = []
    for d in range(2):
        a_slabs = [slab(a_ref, d, r) for r in range(SUBLANES)]
        u_slabs = [slab(u_ref, d, r) for r in range(SUBLANES)]
        h0 = h0_ref[d:d + 1, :] if has_state else jnp.zeros((1, GROUP_W), F32)
        h_slabs, fin = _scan_levels(a_slabs, u_slabs, pa_ref, pb_ref, sa_ref, sb_ref, h0, n0, reverse=(d == 1))
        finals.append(fin)
        for r in range(SUBLANES):
            if d == 0:
                hf_ref[r] = h_slabs[r]
            else:
                h_sum = hf_ref[r] + h_slabs[r]
                for hh in range(2):
                    hs_ref[hh, pl.ds(r, n0, stride=SUBLANES), :] = h_sum[:, hh * LANES:(hh + 1) * LANES]
    h_both = jnp.concatenate([hs_ref[0], hs_ref[1]], axis=1)
    y = h_both * jax.nn.gelu(zl_ref[:, GROUP_W:2 * GROUP_W])
    y_ref[:, GROUP_W:2 * GROUP_W] = y.astype(BF16)
    if not has_state:
        slru_ref[0:1, :] = finals[0]
        slru_ref[1:2, :] = finals[1]


def _recurrent(scal, zr, zl, lp, l, seq_len, state=None):
    t = zr.shape[0]
    b = t // seq_len
    n0 = seq_len // SUBLANES
    has_state = state is not None
    assert has_state or seq_len == RET_CHUNK
    zr3 = zr.reshape(b, seq_len, 4 * GROUP_W)
    zl3 = zl.reshape(b, seq_len, 2 * GROUP_W)
    full = lambda a: pl.BlockSpec(a.shape, lambda i: (0,) * a.ndim)
    in_specs = [
        pl.BlockSpec(memory_space=pltpu.SMEM),
        pl.BlockSpec((None, seq_len, 4 * GROUP_W), lambda i: (i, 0, 0)),
        pl.BlockSpec((None, seq_len, 2 * GROUP_W), lambda i: (i, 0, 0)),
    ]
    args = [scal, zr3, zl3]
    if has_state:
        s0, h0 = state
        heads = GROUP_W // HEAD_DIM
        in_specs += [pl.BlockSpec((None, None, 2, heads, HEAD_DIM, HEAD_DIM), lambda i: (i, l, 0, 0, 0, 0)),
                     pl.BlockSpec((None, None, 2, GROUP_W), lambda i: (i, l, 0, 0))]
        args += [s0, h0]
    params = [lp["lg_lane"], lp["gn_g"], lp["conv_w"], lp["conv_b"], lp["lru_wbd"], lp["lru_bias"], lp["lru_lam"]]
    in_specs += [full(a) for a in params]
    args += params
    out_specs = [pl.BlockSpec((None, seq_len, 2 * GROUP_W), lambda i: (i, 0, 0))]
    out_shape = [jax.ShapeDtypeStruct((b, seq_len, 2 * GROUP_W), BF16)]
    if not has_state:
        heads = GROUP_W // HEAD_DIM
        out_specs += [pl.BlockSpec((None, 2, heads, HEAD_DIM, HEAD_DIM), lambda i: (i, 0, 0, 0, 0)),
                      pl.BlockSpec((None, 2, GROUP_W), lambda i: (i, 0, 0))]
        out_shape += [jax.ShapeDtypeStruct((b, 2, heads, HEAD_DIM, HEAD_DIM), F32),
                      jax.ShapeDtypeStruct((b, 2, GROUP_W), F32)]
    scratch = [
        pltpu.VMEM((GROUP_W // HEAD_DIM, RET_CHUNK, RET_CHUNK), F32),
        pltpu.VMEM((4, RET_CHUNK, GROUP_W), F32),
        pltpu.VMEM((seq_len + 2 * SUBLANES, GROUP_W), F32),
        pltpu.VMEM((2, 2, seq_len, LANES), F32),
        pltpu.VMEM((2, 2, seq_len, LANES), F32),
        pltpu.VMEM((SUBLANES, n0, GROUP_W), F32),
        pltpu.VMEM((2, seq_len, LANES), F32),
        pltpu.VMEM((SUBLANES, n0, GROUP_W), F32),
        pltpu.VMEM((SUBLANES, n0, GROUP_W), F32),
        pltpu.VMEM((2 * n0, GROUP_W), F32),
        pltpu.VMEM((2 * n0, GROUP_W), F32),
    ]
    outs = pl.pallas_call(
        functools.partial(_recurrent_kernel, seq_len=seq_len, has_state=has_state),
        grid=(b,),
        in_specs=in_specs,
        out_specs=out_specs,
        out_shape=out_shape,
        scratch_shapes=scratch,
        compiler_params=pltpu.CompilerParams(
            dimension_semantics=("arbitrary",), vmem_limit_bytes=VMEM_LIMIT),
        name="recurrent_latent" if has_state else "recurrent_ctx",
    )(*args)
    return (outs[0].reshape(t, 2 * GROUP_W),) + tuple(outs[1:])


def _dup_half(x, half):
    sw = pltpu.roll(x, HEAD_DIM, axis=1)
    low = _lane(x.shape) < HEAD_DIM
    return jnp.where(low, x, sw) if half == 0 else jnp.where(low, sw, x)


def _attention_kernel(*refs, seq_len, has_ctx, seqs_per_step):
    n_in = 9 if has_ctx else 5
    scal_ref, ng_ref = refs[0], refs[n_in - 1]
    per_seq = refs[1:n_in - 1] + refs[n_in:]
    per_sequence = [_attention_stages(scal_ref, ng_ref, *[r.at[s] for r in per_seq], seq_len=seq_len,
                                      has_ctx=has_ctx) for s in range(seqs_per_step)]
    stages = [st for seq_stages in per_sequence for st in seq_stages]
    pending = []
    for make_weights, make_out, p in stages:
        pending.append((make_out, p, make_weights(p)))
        if len(pending) > (ATTN_LOOKAHEAD_LATENT if has_ctx else ATTN_LOOKAHEAD_CTX):
            make_out_, p_, w_ = pending.pop(0)
            make_out_(p_, w_)
    for make_out_, p_, w_ in pending:
        make_out_(p_, w_)


def _attention_stages(scal_ref, ng_ref, *refs, seq_len, has_ctx):
    if has_ctx:
        (qa_ref, kvs_ref, kvd_ref, cks_ref, cvs_ref, ckd_ref, cvd_ref, y_ref,
         sk_s, sv_s, dk_s, dv_s, csk_s, csv_s, cdk_s, cdv_s) = refs
    else:
        (qa_ref, kvs_ref, kvd_ref, y_ref, sk_s, sv_s, dk_s, dv_s) = refs
    tq = min(ATTN_Q_TILE, seq_len)
    heads = GROUP_W // HEAD_DIM
    lane = _lane((1, LANES))
    low = lane < HEAD_DIM
    halves = (low, ~low)

    def build():
        def fill(k_all, v_all, k_dst, v_dst, dup):
            for p in range(2):
                if dup:
                    k_p, v_p = _dup_half(k_all, p), _dup_half(v_all, p)
                else:
                    k_p, v_p = k_all[:, p * LANES:(p + 1) * LANES], v_all[:, p * LANES:(p + 1) * LANES]
                k_dst[p] = k_p.astype(BF16)
                for j in range(2):
                    v_dst[2 * p + j] = jnp.where(halves[j], v_p, 1.0).astype(BF16)

        fill(kvs_ref[:, 0:LANES].astype(F32), kvs_ref[:, LANES:2 * LANES].astype(F32), sk_s, sv_s, True)
        fill(kvd_ref[:, 0:GROUP_W].astype(F32), kvd_ref[:, GROUP_W:2 * GROUP_W].astype(F32), dk_s, dv_s, False)
        if has_ctx:
            fill(cks_ref[...], cvs_ref[...], csk_s, csv_s, True)
            fill(ckd_ref[...], cvd_ref[...], cdk_s, cdv_s, False)

    if seq_len == tq:
        build()
    else:
        pl.when(pl.program_id(1) == 0)(build)

    def weights(q_rows, k_loc, k_ctx, valid, extra2):
        s = _dot_nt(q_rows, k_loc)
        if valid is not None:
            s = jnp.where(valid, s, NEG_BIG)
        m = jnp.max(s, -1, keepdims=True)
        sc = None
        if k_ctx is not None:
            sc = _dot_nt(q_rows, k_ctx)
            m = jnp.maximum(m, jnp.max(sc, -1, keepdims=True))
        if extra2 is not None:
            m = jnp.maximum(m, extra2)
        e = jnp.exp2((s - m).astype(BF16))
        ec = None if sc is None else jnp.exp2((sc - m).astype(BF16))
        extra = None if extra2 is None else jnp.exp2(extra2 - m)
        return e, ec, extra

    def weighted_values(w, v_of_head):
        e, ec, extra = w
        half_rows = e.shape[0] // 2
        res = []
        for j in range(2):
            rs = slice(j * half_rows, (j + 1) * half_rows)
            v_loc, v_ctx = v_of_head(j)
            o = _dot(e[rs], v_loc)
            if ec is not None:
                o = o + _dot(ec[rs], v_ctx)
            den = pltpu.roll(o, HEAD_DIM, axis=1)
            if extra is not None:
                den = den + extra[rs]
            res.append(o / jnp.where(halves[j], den, 1.0))
        return res

    if has_ctx:
        win = tq + 2 * SWA_WINDOW
        i = pl.program_id(1)
        blk = jnp.clip(i * (tq // SWA_WINDOW) - 1, 0, (seq_len - win) // SWA_WINDOW)
        start = pl.multiple_of(blk * SWA_WINDOW, SWA_WINDOW)
        rows = pl.ds(start, win)
        qpos = i * tq + (lax.broadcasted_iota(jnp.int32, (2 * tq, win), 0) & (tq - 1))
        kpos = start + lax.broadcasted_iota(jnp.int32, (2 * tq, win), 1)
        valid = jnp.abs(kpos - qpos) <= SWA_WINDOW
    else:
        rows = slice(None)
        valid = None
    first_member = lax.broadcasted_iota(jnp.int32, (2 * tq, 1), 0) < tq
    lam = scal_ref[3 * heads]
    coef = scal_ref[3 * heads + 1]
    grp = lane >> 5

    def swa_weights(p):
        q = qa_ref[:, p * LANES:(p + 1) * LANES]
        q_rows = jnp.concatenate([jnp.where(halves[j], q, 0.0).astype(BF16) for j in range(2)], axis=0)
        sink2 = jnp.where(first_member, scal_ref[2 * heads + 2 * p], scal_ref[2 * heads + 2 * p + 1]) * LOG2E
        return weights(q_rows, sk_s[p, rows, :], csk_s[p] if has_ctx else None, valid, sink2)

    def swa_out(p, w):
        outs = weighted_values(w, lambda j: (sv_s[2 * p + j, rows, :], csv_s[2 * p + j] if has_ctx else None))
        y_ref[:, p * LANES:(p + 1) * LANES] = jnp.where(low, outs[0], outs[1]).astype(BF16)

    def diff_weights(p):
        q = qa_ref[:, GROUP_W + p * LANES:GROUP_W + (p + 1) * LANES]
        q_rows = jnp.concatenate([jnp.where(grp == g, q, 0.0).astype(BF16) for g in range(4)], axis=0)
        return weights(q_rows, dk_s[p], cdk_s[p] if has_ctx else None, None, None)

    def diff_out(p, w):
        parts = weighted_values(w, lambda j: (dv_s[2 * p + j], cdv_s[2 * p + j] if has_ctx else None))
        outs = [r[0:tq] - lam * r[tq:2 * tq] for r in parts]
        o = jnp.where(low, outs[0], outs[1])
        sq = o * o
        ms = jnp.where(low, jnp.sum(jnp.where(low, sq, 0.0), -1, keepdims=True),
                       jnp.sum(jnp.where(low, 0.0, sq), -1, keepdims=True)) * (1.0 / HEAD_DIM)
        y = o * lax.rsqrt(ms + EPS) * ng_ref[:, p * LANES:(p + 1) * LANES] * coef
        y_ref[:, GROUP_W + p * LANES:GROUP_W + (p + 1) * LANES] = y.astype(BF16)

    return [(swa_weights, swa_out, 0), (diff_weights, diff_out, 0),
            (swa_weights, swa_out, 1), (diff_weights, diff_out, 1)]


def _attention(scal, qa, kvs, kvd, norm_g_tiled, l, seq_len, caches=None):
    t = qa.shape[0]
    b = t // seq_len
    tq = min(ATTN_Q_TILE, seq_len)
    nq = seq_len // tq
    has_ctx = caches is not None
    ns = 1 if has_ctx else CTX_SEQS_PER_STEP
    in_specs = [
        pl.BlockSpec(memory_space=pltpu.SMEM),
        pl.BlockSpec((ns, tq, 2 * GROUP_W), lambda i, j: (i, j, 0)),
        pl.BlockSpec((ns, seq_len, GROUP_W), lambda i, j: (i, 0, 0)),
        pl.BlockSpec((ns, seq_len, 2 * GROUP_W), lambda i, j: (i, 0, 0)),
    ]
    args = [scal, qa.reshape(b, seq_len, 2 * GROUP_W), kvs.reshape(b, seq_len, GROUP_W),
            kvd.reshape(b, seq_len, 2 * GROUP_W)]
    heads = GROUP_W // HEAD_DIM
    scratch = [pltpu.VMEM((ns, 2, seq_len, LANES), BF16), pltpu.VMEM((ns, heads, seq_len, LANES), BF16),
               pltpu.VMEM((ns, 2, seq_len, LANES), BF16), pltpu.VMEM((ns, heads, seq_len, LANES), BF16)]
    if has_ctx:
        past = caches[0].shape[2]
        for a in caches:
            in_specs.append(pl.BlockSpec((ns, None) + a.shape[2:], lambda i, j: (i, l, 0, 0)))
            args.append(a)
        scratch += [pltpu.VMEM((ns, 2, past, LANES), BF16), pltpu.VMEM((ns, heads, past, LANES), BF16),
                    pltpu.VMEM((ns, 2, past, LANES), BF16), pltpu.VMEM((ns, heads, past, LANES), BF16)]
    in_specs.append(pl.BlockSpec(norm_g_tiled.shape, lambda i, j: (0, 0)))
    args.append(norm_g_tiled)
    y = pl.pallas_call(
        functools.partial(_attention_kernel, seq_len=seq_len, has_ctx=has_ctx, seqs_per_step=ns),
        grid=(b // ns, nq),
        in_specs=in_specs,
        out_specs=pl.BlockSpec((ns, tq, 2 * GROUP_W), lambda i, j: (i, j, 0)),
        out_shape=jax.ShapeDtypeStruct((b, seq_len, 2 * GROUP_W), BF16),
        scratch_shapes=scratch,
        compiler_params=pltpu.CompilerParams(
            dimension_semantics=("arbitrary", "arbitrary"), vmem_limit_bytes=VMEM_LIMIT),
        name="attention_latent" if has_ctx else "attention_ctx",
    )(*args)
    return y.reshape(t, 2 * GROUP_W)


def _out_mlp_kernel(x_ref, yr_ref, ya_ref, mod_ref, wo_ref, g_ref, w1_ref, w2_ref, gf_ref, o_ref, *, final):
    g1 = mod_ref[:, 2 * D_MODEL:3 * D_MODEL]
    sh2 = mod_ref[:, 3 * D_MODEL:4 * D_MODEL]
    sc2 = mod_ref[:, 4 * D_MODEL:5 * D_MODEL]
    g2 = mod_ref[:, 5 * D_MODEL:6 * D_MODEL]
    y = _dot(yr_ref[...], wo_ref[0:2 * GROUP_W, :]) + _dot(ya_ref[...], wo_ref[2 * GROUP_W:4 * GROUP_W, :])
    x = x_ref[...] + g1 * y
    h = (_rms(x) * g_ref[...] * (1.0 + sc2) + sh2).astype(BF16)
    acc = None
    for c in range(D_FF // D_MODEL):
        sl = slice(c * D_MODEL, (c + 1) * D_MODEL)
        hid = jnp.square(jnp.maximum(_dot(h, w1_ref[:, sl]), 0.0)).astype(BF16)
        part = _dot(hid, w2_ref[sl, :])
        acc = part if acc is None else acc + part
    x = x + g2 * acc
    if final:
        x = _rms(x) * gf_ref[...]
    o_ref[...] = x


def _out_mlp(x, y_rec, y_att, mod4, w_out_b, norm_g, w_ff1_b, w_ff2_b, final_g, l, seq_len, is_latent, final):
    t = x.shape[0]
    tm = MLP_ROW_TILE
    if is_latent:
        tiles_per_seq = seq_len // tm
        mod_map = lambda i: (l, 1 + i // tiles_per_seq, 0, 0)
    else:
        mod_map = lambda i: (l, 0, 0, 0)
    once = pl.Buffered(1)
    in_specs = [
        pl.BlockSpec((tm, D_MODEL), lambda i: (i, 0)),
        pl.BlockSpec((tm, 2 * GROUP_W), lambda i: (i, 0)),
        pl.BlockSpec((tm, 2 * GROUP_W), lambda i: (i, 0)),
        pl.BlockSpec((None, None, 1, N_MOD * D_MODEL), mod_map),
        pl.BlockSpec((None, D_MODEL, D_MODEL), lambda i: (l, 0, 0), pipeline_mode=once),
        pl.BlockSpec((None, 1, D_MODEL), lambda i: (l, 0, 0)),
        pl.BlockSpec((None, D_MODEL, D_FF), lambda i: (l, 0, 0), pipeline_mode=once),
        pl.BlockSpec((None, D_FF, D_MODEL), lambda i: (l, 0, 0), pipeline_mode=once),
        pl.BlockSpec((1, D_MODEL), lambda i: (0, 0)),
    ]
    return pl.pallas_call(
        functools.partial(_out_mlp_kernel, final=final),
        grid=(t // tm,),
        in_specs=in_specs,
        out_specs=pl.BlockSpec((tm, D_MODEL), lambda i: (i, 0)),
        out_shape=jax.ShapeDtypeStruct((t, D_MODEL), F32),
        compiler_params=pltpu.CompilerParams(
            dimension_semantics=("arbitrary",), vmem_limit_bytes=VMEM_LIMIT),
        name="out_mlp_final" if final else "out_mlp",
    )(x, y_rec, y_att, mod4, w_out_b, norm_g, w_ff1_b, w_ff2_b, final_g)


def _rope_tables(length):
    row = (jnp.arange(length) // GRID_W).astype(F32)
    col = (jnp.arange(length) % GRID_W).astype(F32)
    tabs = []
    for dim in (HEAD_DIM, DIFF_QK_DIM):
        n = dim // 4
        inv = ROPE_BASE ** (-jnp.arange(n, dtype=F32) / n)
        ang = jnp.concatenate([row[:, None] * inv, col[:, None] * inv], -1)
        cos, sin = jnp.cos(ang), jnp.sin(ang)
        reps = LANES // dim
        tabs.append(jnp.tile(jnp.concatenate([cos, cos], -1), (1, reps)))
        tabs.append(jnp.tile(jnp.concatenate([-sin, sin], -1), (1, reps)))
    return tabs


def _block_diag(w):
    n, c, d = w.shape
    eye = jnp.eye(n, dtype=w.dtype)
    return jnp.einsum("ncd,nm->ncmd", w, eye).reshape(n * c, n * d)


def _layer_params(l, ret_decay, ret_gn_g, lru_conv_w, lru_conv_b, lru_w_a, lru_b_a, lru_w_x, lru_b_x, lru_lambda,
                  swa_sink, diff_lambda, diff_norm_g):
    heads = GROUP_W // HEAD_DIM
    log_gamma = jax.nn.log_sigmoid(ret_decay[l].astype(F32))
    lambda_init = 0.8 - 0.6 * math.exp(-0.3 * l)
    lv = diff_lambda[l].astype(F32)
    lam = jnp.exp(jnp.sum(lv[0] * lv[1])) - jnp.exp(jnp.sum(lv[2] * lv[3])) + lambda_init
    scal = jnp.concatenate([log_gamma[0], log_gamma[1], swa_sink[l].astype(F32),
                            jnp.stack([lam, jnp.asarray(1.0 - lambda_init, F32)]),
                            jnp.zeros((2,), F32)])
    wbd = jnp.concatenate([_block_diag(lru_w_a[l, 0]), _block_diag(lru_w_x[l, 0]),
                           _block_diag(lru_w_a[l, 1]), _block_diag(lru_w_x[l, 1])], axis=1).astype(BF16)
    bias = jnp.concatenate([lru_b_a[l, 0], lru_b_x[l, 0], lru_b_a[l, 1], lru_b_x[l, 1]])[None, :]
    return dict(
        scal=scal,
        lg_lane=jnp.repeat(log_gamma, HEAD_DIM, axis=1),
        gn_g=ret_gn_g[l][None, :],
        conv_w=lru_conv_w[l],
        conv_b=lru_conv_b[l][None, :],
        lru_wbd=wbd,
        lru_bias=bias,
        lru_lam=lru_lambda[l],
        diff_ng=jnp.tile(diff_norm_g[l], heads)[None, :],
    )


def kernel(x_prompt, x_sample, c, state_ret, state_lru, cache_swa_k, cache_swa_v, cache_diff_k, cache_diff_v,
           c_ctx, w_ada, b_ada, norm_mix_g, w_in, ret_decay, ret_gn_g, lru_conv_w, lru_conv_b,
           lru_w_a, lru_b_a, lru_w_x, lru_b_x, lru_lambda, swa_sink, diff_lambda, diff_norm_g,
           w_out, norm_mlp_g, w_ff1, w_ff2, final_norm_g):
    batch, seq, _ = x_prompt.shape
    dec_batch, dec_seq, _ = x_sample.shape
    past = cache_swa_k.shape[2]

    cvec = jnp.concatenate([c_ctx[None, :], c, jnp.zeros((MOD_ROWS - 1 - dec_batch, D_MODEL), F32)], axis=0)
    mod4 = _modulation(cvec, w_ada, b_ada).reshape(DEPTH, MOD_ROWS, 1, N_MOD * D_MODEL)

    w_in_b = w_in.astype(BF16)
    w_out_b = w_out.astype(BF16)
    w_ff1_b = w_ff1.astype(BF16)
    w_ff2_b = w_ff2.astype(BF16)
    norm_mix3 = norm_mix_g.reshape(DEPTH, 1, D_MODEL)
    norm_mlp3 = norm_mlp_g.reshape(DEPTH, 1, D_MODEL)
    final_g = final_norm_g[None, :]
    rope_tabs = _rope_tables(dec_seq)
    layers = [_layer_params(l, ret_decay, ret_gn_g, lru_conv_w, lru_conv_b, lru_w_a, lru_b_a, lru_w_x, lru_b_x,
                            lru_lambda, swa_sink, diff_lambda, diff_norm_g) for l in range(DEPTH)]

    heads = GROUP_W // HEAD_DIM

    xp = x_prompt.reshape(batch * seq, D_MODEL)
    new_ret, new_lru, new_swa, new_diff = [], [], [], []
    for l in range(DEPTH):
        lp = layers[l]
        zr, zl, qa, kvs, kvd = _in_proj(xp, mod4, norm_mix3, w_in_b, l, seq, None)
        y_rec, s_ret, s_lru = _recurrent(lp["scal"], zr, zl, lp, l, seq)
        y_att = _attention(lp["scal"], qa, kvs, kvd, lp["diff_ng"], l, seq)
        xp = _out_mlp(xp, y_rec, y_att, mod4, w_out_b, norm_mlp3, w_ff1_b, w_ff2_b, final_g,
                      l, seq, False, l == DEPTH - 1)
        new_ret.append(s_ret)
        new_lru.append(s_lru)
        new_swa.append(kvs)
        new_diff.append(kvd)
    y_prompt = xp.reshape(batch, seq, D_MODEL)
    new_ret = jnp.stack(new_ret, axis=1)
    new_swa = jnp.stack(new_swa, axis=1).reshape(batch, seq, DEPTH, 2, 2, HEAD_DIM)
    new_swa = jnp.transpose(new_swa, (3, 0, 2, 1, 4, 5))
    new_diff = jnp.stack(new_diff, axis=1).reshape(batch, seq, DEPTH, 2, heads, HEAD_DIM)
    new_diff = jnp.transpose(new_diff, (3, 0, 2, 1, 4, 5))

    xs = x_sample.reshape(dec_batch * dec_seq, D_MODEL)
    state = (state_ret, state_lru)
    caches = (cache_swa_k.reshape(dec_batch, DEPTH, past, 2 * HEAD_DIM),
              cache_swa_v.reshape(dec_batch, DEPTH, past, 2 * HEAD_DIM),
              cache_diff_k.reshape(dec_batch, DEPTH, past, GROUP_W),
              cache_diff_v.reshape(dec_batch, DEPTH, past, GROUP_W))
    for l in range(DEPTH):
        lp = layers[l]
        zr, zl, qa, kvs, kvd = _in_proj(xs, mod4, norm_mix3, w_in_b, l, dec_seq, rope_tabs)
        (y_rec,) = _recurrent(lp["scal"], zr, zl, lp, l, dec_seq, state)
        y_att = _attention(lp["scal"], qa, kvs, kvd, lp["diff_ng"], l, dec_seq, caches)
        xs = _out_mlp(xs, y_rec, y_att, mod4, w_out_b, norm_mlp3, w_ff1_b, w_ff2_b, final_g,
                      l, dec_seq, True, l == DEPTH - 1)
    y_sample = xs.reshape(dec_batch, dec_seq, D_MODEL)

    return (y_prompt, y_sample, new_ret, jnp.stack(new_lru, axis=1),
            new_swa[0], new_swa[1], new_diff[0], new_diff[1])
```

```python
import functools
import math

import jax
import jax.numpy as jnp
from jax import lax
from jax.experimental import pallas as pl
from jax.experimental.pallas import tpu as pltpu

F32 = jnp.float32
BF16 = jnp.bfloat16

D_MODEL = 1024
DEPTH = 4
GRID_W = 64
HEAD_DIM = 64
GROUP_W = D_MODEL // 4
RET_CHUNK = 256
LRU_BLOCKS = 4
LRU_C = 8.0
SWA_WINDOW = 128
DIFF_QK_DIM = HEAD_DIM // 2
D_FF = 4 * D_MODEL
ROPE_BASE = 10000.0
EPS = 1e-6
N_MOD = 6
D_IN = 11 * GROUP_W
MOD_ROWS = 16

LANES = 128
SUBLANES = 8
VMEM_LIMIT = 56 * 1024 * 1024
NEG_BIG = -1e30
LOG2E = math.log2(math.e)
SWA_QSCALE = (HEAD_DIM ** -0.5) * LOG2E
DIFF_QSCALE = (DIFF_QK_DIM ** -0.5) * LOG2E

ROW_TILE = 512
MLP_ROW_TILE = 1024
ATTN_Q_TILE = 256
LATENT_SEQS_PER_STEP = 4
CTX_SEQS_PER_STEP = 8
ATTN_LOOKAHEAD_LATENT = 2
ATTN_LOOKAHEAD_CTX = 1


def _dot(a, b):
    return jnp.dot(a, b, preferred_element_type=F32)


def _dot_nt(a, b):
    return lax.dot_general(a, b, (((1,), (1,)), ((), ())), preferred_element_type=F32)


def _dot_tn(a, b):
    return lax.dot_general(a, b, (((0,), (0,)), ((), ())), preferred_element_type=F32)


def _rms(x):
    return x * lax.rsqrt(jnp.mean(x * x, axis=-1, keepdims=True) + EPS)


def _sigmoid(x):
    return 0.5 * jnp.tanh(0.5 * x) + 0.5


def _lane(shape):
    return lax.broadcasted_iota(jnp.int32, shape, len(shape) - 1)


def _mod_kernel(c_ref, w_ref, b_ref, o_ref):
    s = jax.nn.silu(c_ref[...]).astype(BF16)
    o_ref[...] = _dot(s, w_ref[...].astype(BF16)) + b_ref[...]


def _modulation(cvec, w_ada, b_ada):
    tn = 1536
    n_mod = N_MOD * D_MODEL
    return pl.pallas_call(
        _mod_kernel,
        grid=(DEPTH, n_mod // tn),
        in_specs=[
            pl.BlockSpec((MOD_ROWS, D_MODEL), lambda l, j: (0, 0)),
            pl.BlockSpec((None, D_MODEL, tn), lambda l, j: (l, 0, j)),
            pl.BlockSpec((None, 1, tn), lambda l, j: (l, 0, j)),
        ],
        out_specs=pl.BlockSpec((None, MOD_ROWS, tn), lambda l, j: (l, 0, j)),
        out_shape=jax.ShapeDtypeStruct((DEPTH, MOD_ROWS, n_mod), F32),
        compiler_params=pltpu.CompilerParams(
            dimension_semantics=("arbitrary", "arbitrary"), vmem_limit_bytes=VMEM_LIMIT),
        name="modulation",
    )(cvec, w_ada, b_ada.reshape(DEPTH, 1, n_mod))


def _rope_slab(x, c, s, half):
    first = (_lane(x.shape) & (2 * half - 1)) < half
    swapped = jnp.where(first, pltpu.roll(x, LANES - half, axis=1), pltpu.roll(x, half, axis=1))
    return x * c + swapped * s


def _in_proj_kernel(*refs, rope):
    if rope:
        (x_ref, mod_ref, g_ref, w_ref, c64_ref, s64_ref, c32_ref, s32_ref,
         zr_ref, zl_ref, qa_ref, kvs_ref, kvd_ref) = refs
    else:
        x_ref, mod_ref, g_ref, w_ref, zr_ref, zl_ref, qa_ref, kvs_ref, kvd_ref = refs
    sh1 = mod_ref[:, 0:D_MODEL]
    sc1 = mod_ref[:, D_MODEL:2 * D_MODEL]
    h = (_rms(x_ref[...]) * g_ref[...] * (1.0 + sc1) + sh1).astype(BF16)
    zs = _dot(h, w_ref[:, 6 * GROUP_W:8 * GROUP_W])
    zd = _dot(h, w_ref[:, 8 * GROUP_W:11 * GROUP_W])
    if rope:
        c64, s64, c32, s32 = c64_ref[...], s64_ref[...], c32_ref[...], s32_ref[...]
        rope_s = lambda v: _rope_slab(v, c64, s64, HEAD_DIM // 2)
        rope_d = lambda v: _rope_slab(v, c32, s32, DIFF_QK_DIM // 2)
    else:
        rope_s = rope_d = lambda v: v
    out = qa_ref.dtype
    for i in range(2):
        sl = slice(i * LANES, (i + 1) * LANES)
        qa_ref[:, sl] = (rope_s(zs[:, sl]) * SWA_QSCALE).astype(out)
        qa_ref[:, GROUP_W + i * LANES:GROUP_W + (i + 1) * LANES] = (rope_d(zd[:, sl]) * DIFF_QSCALE).astype(out)
        kvd_ref[:, sl] = rope_d(zd[:, GROUP_W + i * LANES:GROUP_W + (i + 1) * LANES]).astype(out)
    kvs_ref[:, 0:LANES] = rope_s(zs[:, 2 * LANES:3 * LANES]).astype(out)
    kvs_ref[:, LANES:2 * LANES] = zs[:, 3 * LANES:4 * LANES].astype(out)
    kvd_ref[:, GROUP_W:2 * GROUP_W] = zd[:, 2 * GROUP_W:3 * GROUP_W].astype(out)
    zl_ref[...] = _dot(h, w_ref[:, 4 * GROUP_W:6 * GROUP_W])
    zr_ref[...] = _dot(h, w_ref[:, 0:4 * GROUP_W])


def _in_proj(x, mod4, norm_g, w_in_b, l, seq_len, rope_tabs):
    t = x.shape[0]
    tm = ROW_TILE
    rope = rope_tabs is not None
    if rope:
        tiles_per_seq = seq_len // tm
        mod_map = lambda i: (l, 1 + i // tiles_per_seq, 0, 0)
    else:
        mod_map = lambda i: (l, 0, 0, 0)
    in_specs = [
        pl.BlockSpec((tm, D_MODEL), lambda i: (i, 0)),
        pl.BlockSpec((None, None, 1, N_MOD * D_MODEL), mod_map),
        pl.BlockSpec((None, 1, D_MODEL), lambda i: (l, 0, 0)),
        pl.BlockSpec((None, D_MODEL, D_IN), lambda i: (l, 0, 0)),
    ]
    args = [x, mod4, norm_g, w_in_b]
    if rope:
        tab_spec = pl.BlockSpec((tm, LANES), lambda i: (i % tiles_per_seq, 0))
        in_specs += [tab_spec] * 4
        args += list(rope_tabs)
    att = BF16 if rope else F32
    outs = ((4 * GROUP_W, F32), (2 * GROUP_W, F32), (2 * GROUP_W, att), (GROUP_W, att), (2 * GROUP_W, att))
    return pl.pallas_call(
        functools.partial(_in_proj_kernel, rope=rope),
        grid=(t // tm,),
        in_specs=in_specs,
        out_specs=[pl.BlockSpec((tm, w), lambda i: (i, 0)) for w, _ in outs],
        out_shape=[jax.ShapeDtypeStruct((t, w), dt) for w, dt in outs],
        compiler_params=pltpu.CompilerParams(
            dimension_semantics=("arbitrary",), vmem_limit_bytes=VMEM_LIMIT),
        name="in_proj_rope" if rope else "in_proj",
    )(*args)


def _scan_levels(a_slabs, u_slabs, pa_ref, pb_ref, sa_ref, sb_ref, h0, n0, reverse):
    order = list(range(SUBLANES - 1, -1, -1)) if reverse else list(range(SUBLANES))
    acc_a = acc_b = None
    for r in order:
        if acc_a is None:
            acc_a, acc_b = a_slabs[r], u_slabs[r]
        else:
            acc_b = a_slabs[r] * acc_b + u_slabs[r]
            acc_a = a_slabs[r] * acc_a
        pa_ref[r] = acc_a
        pb_ref[r] = acc_b
    pad = n0 // 2
    ident = pad + n0 if reverse else 0
    sa_ref[ident:ident + pad, :] = jnp.ones((pad, GROUP_W), F32)
    sb_ref[ident:ident + pad, :] = jnp.zeros((pad, GROUP_W), F32)
    sign = 1 if reverse else -1
    k = 1
    while k < n0:
        sa_ref[pad:pad + n0, :] = acc_a
        sb_ref[pad:pad + n0, :] = acc_b
        off = pad + sign * k
        acc_b = acc_a * sb_ref[off:off + n0, :] + acc_b
        acc_a = acc_a * sa_ref[off:off + n0, :]
        k *= 2
    sa_ref[pad:pad + n0, :] = acc_a
    sb_ref[pad:pad + n0, :] = acc_b
    off = pad + sign
    carry_in = sa_ref[off:off + n0, :] * h0 + sb_ref[off:off + n0, :]
    end = 0 if reverse else n0 - 1
    final = acc_a[end:end + 1, :] * h0 + acc_b[end:end + 1, :]
    h_slabs = [pa_ref[r] * carry_in + pb_ref[r] for r in range(SUBLANES)]
    return h_slabs, final


def _recurrent_kernel(*refs, seq_len, has_state):
    if has_state:
        (scal_ref, zr_ref, zl_ref, s0_ref, h0_ref, lgl_ref, gng_ref, cw_ref, cb_ref, wbd_ref, lb_ref, lam_ref,
         y_ref,
         d_ref, vec_ref, pad_ref, a_ref, u_ref, hf_ref, hs_ref, pa_ref, pb_ref, sa_ref, sb_ref) = refs
    else:
        (scal_ref, zr_ref, zl_ref, lgl_ref, gng_ref, cw_ref, cb_ref, wbd_ref, lb_ref, lam_ref,
         y_ref, sret_ref, slru_ref,
         d_ref, vec_ref, pad_ref, a_ref, u_ref, hf_ref, hs_ref, pa_ref, pb_ref, sa_ref, sb_ref) = refs
    ch = RET_CHUNK
    n_chunks = seq_len // ch
    n0 = seq_len // SUBLANES
    heads = GROUP_W // HEAD_DIM

    @pl.when(pl.program_id(0) == 0)
    def _():
        n = lax.broadcasted_iota(jnp.int32, (ch, ch), 0)
        m = lax.broadcasted_iota(jnp.int32, (ch, ch), 1)
        dist = (n - m).astype(F32)
        for h in range(heads):
            lg = jnp.where(dist > 0, scal_ref[h], scal_ref[heads + h])
            d_ref[h] = jnp.where(dist == 0, 2.0, jnp.exp(jnp.abs(dist) * lg))
        t = lax.broadcasted_iota(jnp.int32, (ch, GROUP_W), 0).astype(F32)
        lgf = lgl_ref[0:1, :]
        lgb = lgl_ref[1:2, :]
        vec_ref[0] = jnp.exp((t + 1.0) * lgf)
        vec_ref[1] = jnp.exp((ch - t) * lgb)
        vec_ref[2] = jnp.exp((ch - 1.0 - t) * lgf)
        vec_ref[3] = jnp.exp(t * lgb)

    low = _lane((1, LANES)) < HEAD_DIM
    rr = lax.broadcasted_iota(jnp.int32, (LANES, LANES), 0) < HEAD_DIM
    cc = lax.broadcasted_iota(jnp.int32, (LANES, LANES), 1) < HEAD_DIM
    block_diag = (rr == cc).astype(F32)
    g_chunk_f = jnp.exp(ch * lgl_ref[0:1, :])
    g_chunk_b = jnp.exp(ch * lgl_ref[1:2, :])

    def kv(c, p):
        rows = slice(c * ch, (c + 1) * ch)
        k = zr_ref[rows, GROUP_W + p * LANES:GROUP_W + (p + 1) * LANES] * (HEAD_DIM ** -0.5)
        v = zr_ref[rows, 2 * GROUP_W + p * LANES:2 * GROUP_W + (p + 1) * LANES].astype(BF16)
        return k, v

    def state_update(k, v, zeta):
        return _dot_tn((k * zeta).astype(BF16), v) * block_diag

    def pair_state(d, p):
        zero = jnp.zeros((HEAD_DIM, HEAD_DIM), F32)
        top = jnp.concatenate([s0_ref[d, 2 * p], zero], axis=1)
        bottom = jnp.concatenate([zero, s0_ref[d, 2 * p + 1]], axis=1)
        return jnp.concatenate([top, bottom], axis=0)

    def store_state(d, p, s_pair):
        sret_ref[d, 2 * p] = s_pair[0:HEAD_DIM, 0:HEAD_DIM]
        sret_ref[d, 2 * p + 1] = s_pair[HEAD_DIM:LANES, HEAD_DIM:LANES]

    for p in range(2):
        pl_sl = slice(p * LANES, (p + 1) * LANES)
        if has_state:
            s_b = [None] * n_chunks
            s_b[n_chunks - 1] = pair_state(1, p)
            for c in range(n_chunks - 1, 0, -1):
                k, v = kv(c, p)
                s_b[c - 1] = s_b[c] * g_chunk_b[:, pl_sl] + state_update(k, v, vec_ref[3, :, pl_sl])
            s_f = pair_state(0, p)
        for c in range(n_chunks):
            rows = slice(c * ch, (c + 1) * ch)
            q = zr_ref[rows, pl_sl]
            k, v = kv(c, p)
            kb = k.astype(BF16)
            outs = []
            for j in range(2):
                qm = jnp.where(low if j == 0 else ~low, q, 0.0).astype(BF16)
                w = (_dot_nt(qm, kb) * d_ref[2 * p + j]).astype(BF16)
                outs.append(_dot(w, v))
            o = jnp.where(low, outs[0], outs[1])
            if has_state:
                qb = q.astype(BF16)
                o = o + _dot(qb, s_f.astype(BF16)) * vec_ref[0, :, pl_sl]
                o = o + _dot(qb, s_b[c].astype(BF16)) * vec_ref[1, :, pl_sl]
                if c + 1 < n_chunks:
                    s_f = s_f * g_chunk_f[:, pl_sl] + state_update(k, v, vec_ref[2, :, pl_sl])
            else:
                store_state(0, p, state_update(k, v, vec_ref[2, :, pl_sl]))
                store_state(1, p, state_update(k, v, vec_ref[3, :, pl_sl]))
            inv = 1.0 / HEAD_DIM
            mu = jnp.where(low, jnp.sum(jnp.where(low, o, 0.0), -1, keepdims=True),
                           jnp.sum(jnp.where(low, 0.0, o), -1, keepdims=True)) * inv
            dlt = o - mu
            sq = dlt * dlt
            var = jnp.where(low, jnp.sum(jnp.where(low, sq, 0.0), -1, keepdims=True),
                            jnp.sum(jnp.where(low, 0.0, sq), -1, keepdims=True)) * inv
            gate = zr_ref[rows, 3 * GROUP_W + p * LANES:3 * GROUP_W + (p + 1) * LANES]
            y = dlt * lax.rsqrt(var + EPS) * gng_ref[:, pl_sl] * (gate * _sigmoid(gate))
            y_ref[rows, pl_sl] = y.astype(BF16)

    x = zl_ref[:, 0:GROUP_W]
    zero8 = jnp.zeros((SUBLANES, GROUP_W), F32)
    pad_ref[0:SUBLANES, :] = zero8
    pad_ref[SUBLANES:SUBLANES + seq_len, :] = x
    pad_ref[SUBLANES + seq_len:2 * SUBLANES + seq_len, :] = zero8
    cw = cw_ref[...]
    xc = (cw[0:1] * pad_ref[SUBLANES - 2:SUBLANES - 2 + seq_len, :]
          + cw[1:2] * pad_ref[SUBLANES - 1:SUBLANES - 1 + seq_len, :]
          + cw[2:3] * x
          + cw[3:4] * pad_ref[SUBLANES + 1:SUBLANES + 1 + seq_len, :]
          + cb_ref[...])
    pre = _dot(xc.astype(BF16), wbd_ref[...]) + lb_ref[...]
    for d in range(2):
        r_gate = _sigmoid(pre[:, (2 * d) * GROUP_W:(2 * d + 1) * GROUP_W])
        i_gate = _sigmoid(pre[:, (2 * d + 1) * GROUP_W:(2 * d + 2) * GROUP_W])
        log_a = -LRU_C * r_gate * jax.nn.softplus(-lam_ref[d:d + 1, :])
        a_val = jnp.exp(log_a)
        one_minus_a2 = -jnp.tanh(log_a) * (a_val * a_val + 1.0)
        root = jnp.where(one_minus_a2 > 0.0, one_minus_a2 * lax.rsqrt(one_minus_a2), 0.0)
        u_val = root * (i_gate * xc)
        for hh in range(2):
            a_ref[d, hh] = a_val[:, hh * LANES:(hh + 1) * LANES]
            u_ref[d, hh] = u_val[:, hh * LANES:(hh + 1) * LANES]

    def slab(ref, d, r):
        return jnp.concatenate([ref[d, hh, pl.ds(r, n0, stride=SUBLANES), :] for hh in range(2)], axis=1)

    finals = []
    for d in range(2):
        a_slabs = [slab(a_ref, d, r) for r in range(SUBLANES)]
        u_slabs = [slab(u_ref, d, r) for r in range(SUBLANES)]
        h0 = h0_ref[d:d + 1, :] if has_state else jnp.zeros((1, GROUP_W), F32)
        h_slabs, fin = _scan_levels(a_slabs, u_slabs, pa_ref, pb_ref, sa_ref, sb_ref, h0, n0, reverse=(d == 1))
        finals.append(fin)
        for r in range(SUBLANES):
            if d == 0:
                hf_ref[r] = h_slabs[r]
            else:
                h_sum = hf_ref[r] + h_slabs[r]
                for hh in range(2):
                    hs_ref[hh, pl.ds(r, n0, stride=SUBLANES), :] = h_sum[:, hh * LANES:(hh + 1) * LANES]
    h_both = jnp.concatenate([hs_ref[0], hs_ref[1]], axis=1)
    y = h_both * jax.nn.gelu(zl_ref[:, GROUP_W:2 * GROUP_W])
    y_ref[:, GROUP_W:2 * GROUP_W] = y.astype(BF16)
    if not has_state:
        slru_ref[0:1, :] = finals[0]
        slru_ref[1:2, :] = finals[1]


def _recurrent(scal, zr, zl, lp, l, seq_len, state=None):
    t = zr.shape[0]
    b = t // seq_len
    n0 = seq_len // SUBLANES
    has_state = state is not None
    assert has_state or seq_len == RET_CHUNK
    zr3 = zr.reshape(b, seq_len, 4 * GROUP_W)
    zl3 = zl.reshape(b, seq_len, 2 * GROUP_W)
    full = lambda a: pl.BlockSpec(a.shape, lambda i: (0,) * a.ndim)
    in_specs = [
        pl.BlockSpec(memory_space=pltpu.SMEM),
        pl.BlockSpec((None, seq_len, 4 * GROUP_W), lambda i: (i, 0, 0)),
        pl.BlockSpec((None, seq_len, 2 * GROUP_W), lambda i: (i, 0, 0)),
    ]
    args = [scal, zr3, zl3]
    if has_state:
        s0, h0 = state
        heads = GROUP_W // HEAD_DIM
        in_specs += [pl.BlockSpec((None, None, 2, heads, HEAD_DIM, HEAD_DIM), lambda i: (i, l, 0, 0, 0, 0)),
                     pl.BlockSpec((None, None, 2, GROUP_W), lambda i: (i, l, 0, 0))]
        args += [s0, h0]
    params = [lp["lg_lane"], lp["gn_g"], lp["conv_w"], lp["conv_b"], lp["lru_wbd"], lp["lru_bias"], lp["lru_lam"]]
    in_specs += [full(a) for a in params]
    args += params
    out_specs = [pl.BlockSpec((None, seq_len, 2 * GROUP_W), lambda i: (i, 0, 0))]
    out_shape = [jax.ShapeDtypeStruct((b, seq_len, 2 * GROUP_W), BF16)]
    if not has_state:
        heads = GROUP_W // HEAD_DIM
        out_specs += [pl.BlockSpec((None, 2, heads, HEAD_DIM, HEAD_DIM), lambda i: (i, 0, 0, 0, 0)),
                      pl.BlockSpec((None, 2, GROUP_W), lambda i: (i, 0, 0))]
        out_shape += [jax.ShapeDtypeStruct((b, 2, heads, HEAD_DIM, HEAD_DIM), F32),
                      jax.ShapeDtypeStruct((b, 2, GROUP_W), F32)]
    scratch = [
        pltpu.VMEM((GROUP_W // HEAD_DIM, RET_CHUNK, RET_CHUNK), F32),
        pltpu.VMEM((4, RET_CHUNK, GROUP_W), F32),
        pltpu.VMEM((seq_len + 2 * SUBLANES, GROUP_W), F32),
        pltpu.VMEM((2, 2, seq_len, LANES), F32),
        pltpu.VMEM((2, 2, seq_len, LANES), F32),
        pltpu.VMEM((SUBLANES, n0, GROUP_W), F32),
        pltpu.VMEM((2, seq_len, LANES), F32),
        pltpu.VMEM((SUBLANES, n0, GROUP_W), F32),
        pltpu.VMEM((SUBLANES, n0, GROUP_W), F32),
        pltpu.VMEM((2 * n0, GROUP_W), F32),
        pltpu.VMEM((2 * n0, GROUP_W), F32),
    ]
    outs = pl.pallas_call(
        functools.partial(_recurrent_kernel, seq_len=seq_len, has_state=has_state),
        grid=(b,),
        in_specs=in_specs,
        out_specs=out_specs,
        out_shape=out_shape,
        scratch_shapes=scratch,
        compiler_params=pltpu.CompilerParams(
            dimension_semantics=("arbitrary",), vmem_limit_bytes=VMEM_LIMIT),
        name="recurrent_latent" if has_state else "recurrent_ctx",
    )(*args)
    return (outs[0].reshape(t, 2 * GROUP_W),) + tuple(outs[1:])


def _dup_half(x, half):
    sw = pltpu.roll(x, HEAD_DIM, axis=1)
    low = _lane(x.shape) < HEAD_DIM
    return jnp.where(low, x, sw) if half == 0 else jnp.where(low, sw, x)


def _attention_kernel(*refs, seq_len, has_ctx, seqs_per_step):
    n_in = 9 if has_ctx else 5
    scal_ref, ng_ref = refs[0], refs[n_in - 1]
    per_seq = refs[1:n_in - 1] + refs[n_in:]
    per_sequence = [_attention_stages(scal_ref, ng_ref, *[r.at[s] for r in per_seq], seq_len=seq_len,
                                      has_ctx=has_ctx) for s in range(seqs_per_step)]
    stages = [st for seq_stages in per_sequence for st in seq_stages]
    pending = []
    for make_weights, make_out, p in stages:
        pending.append((make_out, p, make_weights(p)))
        if len(pending) > (ATTN_LOOKAHEAD_LATENT if has_ctx else ATTN_LOOKAHEAD_CTX):
            make_out_, p_, w_ = pending.pop(0)
            make_out_(p_, w_)
    for make_out_, p_, w_ in pending:
        make_out_(p_, w_)


def _attention_stages(scal_ref, ng_ref, *refs, seq_len, has_ctx):
    if has_ctx:
        (qa_ref, kvs_ref, kvd_ref, cks_ref, cvs_ref, ckd_ref, cvd_ref, y_ref,
         sk_s, sv_s, dk_s, dv_s, csk_s, csv_s, cdk_s, cdv_s) = refs
    else:
        (qa_ref, kvs_ref, kvd_ref, y_ref, sk_s, sv_s, dk_s, dv_s) = refs
    tq = min(ATTN_Q_TILE, seq_len)
    heads = GROUP_W // HEAD_DIM
    lane = _lane((1, LANES))
    low = lane < HEAD_DIM
    halves = (low, ~low)

    def build():
        def fill(k_all, v_all, k_dst, v_dst, dup):
            for p in range(2):
                if dup:
                    k_p, v_p = _dup_half(k_all, p), _dup_half(v_all, p)
                else:
                    k_p, v_p = k_all[:, p * LANES:(p + 1) * LANES], v_all[:, p * LANES:(p + 1) * LANES]
                k_dst[p] = k_p.astype(BF16)
                for j in range(2):
                    v_dst[2 * p + j] = jnp.where(halves[j], v_p, 1.0).astype(BF16)

        fill(kvs_ref[:, 0:LANES].astype(F32), kvs_ref[:, LANES:2 * LANES].astype(F32), sk_s, sv_s, True)
        fill(kvd_ref[:, 0:GROUP_W].astype(F32), kvd_ref[:, GROUP_W:2 * GROUP_W].astype(F32), dk_s, dv_s, False)
        if has_ctx:
            fill(cks_ref[...], cvs_ref[...], csk_s, csv_s, True)
            fill(ckd_ref[...], cvd_ref[...], cdk_s, cdv_s, False)

    if seq_len == tq:
        build()
    else:
        pl.when(pl.program_id(1) == 0)(build)

    def weights(q_rows, k_loc, k_ctx, valid, extra2):
        s = _dot_nt(q_rows, k_loc)
        if valid is not None:
            s = jnp.where(valid, s, NEG_BIG)
        m = jnp.max(s, -1, keepdims=True)
        sc = None
        if k_ctx is not None:
            sc = _dot_nt(q_rows, k_ctx)
            m = jnp.maximum(m, jnp.max(sc, -1, keepdims=True))
        if extra2 is not None:
            m = jnp.maximum(m, extra2)
        e = jnp.exp2((s - m).astype(BF16))
        ec = None if sc is None else jnp.exp2((sc - m).astype(BF16))
        extra = None if extra2 is None else jnp.exp2(extra2 - m)
        return e, ec, extra

    def weighted_values(w, v_of_head):
        e, ec, extra = w
        half_rows = e.shape[0] // 2
        res = []
        for j in range(2):
            rs = slice(j * half_rows, (j + 1) * half_rows)
            v_loc, v_ctx = v_of_head(j)
            o = _dot(e[rs], v_loc)
            if ec is not None:
                o = o + _dot(ec[rs], v_ctx)
            den = pltpu.roll(o, HEAD_DIM, axis=1)
            if extra is not None:
                den = den + extra[rs]
            res.append(o / jnp.where(halves[j], den, 1.0))
        return res

    if has_ctx:
        win = tq + 2 * SWA_WINDOW
        i = pl.program_id(1)
        blk = jnp.clip(i * (tq // SWA_WINDOW) - 1, 0, (seq_len - win) // SWA_WINDOW)
        start = pl.multiple_of(blk * SWA_WINDOW, SWA_WINDOW)
        rows = pl.ds(start, win)
        qpos = i * tq + (lax.broadcasted_iota(jnp.int32, (2 * tq, win), 0) & (tq - 1))
        kpos = start + lax.broadcasted_iota(jnp.int32, (2 * tq, win), 1)
        valid = jnp.abs(kpos - qpos) <= SWA_WINDOW
    else:
        rows = slice(None)
        valid = None
    first_member = lax.broadcasted_iota(jnp.int32, (2 * tq, 1), 0) < tq
    lam = scal_ref[3 * heads]
    coef = scal_ref[3 * heads + 1]
    grp = lane >> 5

    def swa_weights(p):
        q = qa_ref[:, p * LANES:(p + 1) * LANES]
        q_rows = jnp.concatenate([jnp.where(halves[j], q, 0.0).astype(BF16) for j in range(2)], axis=0)
        sink2 = jnp.where(first_member, scal_ref[2 * heads + 2 * p], scal_ref[2 * heads + 2 * p + 1]) * LOG2E
        return weights(q_rows, sk_s[p, rows, :], csk_s[p] if has_ctx else None, valid, sink2)

    def swa_out(p, w):
        outs = weighted_values(w, lambda j: (sv_s[2 * p + j, rows, :], csv_s[2 * p + j] if has_ctx else None))
        y_ref[:, p * LANES:(p + 1) * LANES] = jnp.where(low, outs[0], outs[1]).astype(BF16)

    def diff_weights(p):
        q = qa_ref[:, GROUP_W + p * LANES:GROUP_W + (p + 1) * LANES]
        q_rows = jnp.concatenate([jnp.where(grp == g, q, 0.0).astype(BF16) for g in range(4)], axis=0)
        return weights(q_rows, dk_s[p], cdk_s[p] if has_ctx else None, None, None)

    def diff_out(p, w):
        parts = weighted_values(w, lambda j: (dv_s[2 * p + j], cdv_s[2 * p + j] if has_ctx else None))
        outs = [r[0:tq] - lam * r[tq:2 * tq] for r in parts]
        o = jnp.where(low, outs[0], outs[1])
        sq = o * o
        ms = jnp.where(low, jnp.sum(jnp.where(low, sq, 0.0), -1, keepdims=True),
                       jnp.sum(jnp.where(low, 0.0, sq), -1, keepdims=True)) * (1.0 / HEAD_DIM)
        y = o * lax.rsqrt(ms + EPS) * ng_ref[:, p * LANES:(p + 1) * LANES] * coef
        y_ref[:, GROUP_W + p * LANES:GROUP_W + (p + 1) * LANES] = y.astype(BF16)

    return [(swa_weights, swa_out, 0), (diff_weights, diff_out, 0),
            (swa_weights, swa_out, 1), (diff_weights, diff_out, 1)]


def _attention(scal, qa, kvs, kvd, norm_g_tiled, l, seq_len, caches=None):
    t = qa.shape[0]
    b = t // seq_len
    tq = min(ATTN_Q_TILE, seq_len)
    nq = seq_len // tq
    has_ctx = caches is not None
    ns = LATENT_SEQS_PER_STEP if has_ctx else CTX_SEQS_PER_STEP
    in_specs = [
        pl.BlockSpec(memory_space=pltpu.SMEM),
        pl.BlockSpec((ns, tq, 2 * GROUP_W), lambda i, j: (i, j, 0)),
        pl.BlockSpec((ns, seq_len, GROUP_W), lambda i, j: (i, 0, 0)),
        pl.BlockSpec((ns, seq_len, 2 * GROUP_W), lambda i, j: (i, 0, 0)),
    ]
    args = [scal, qa.reshape(b, seq_len, 2 * GROUP_W), kvs.reshape(b, seq_len, GROUP_W),
            kvd.reshape(b, seq_len, 2 * GROUP_W)]
    heads = GROUP_W // HEAD_DIM
    scratch = [pltpu.VMEM((ns, 2, seq_len, LANES), BF16), pltpu.VMEM((ns, heads, seq_len, LANES), BF16),
               pltpu.VMEM((ns, 2, seq_len, LANES), BF16), pltpu.VMEM((ns, heads, seq_len, LANES), BF16)]
    if has_ctx:
        past = caches[0].shape[2]
        for a in caches:
            in_specs.append(pl.BlockSpec((ns, None) + a.shape[2:], lambda i, j: (i, l, 0, 0)))
            args.append(a)
        scratch += [pltpu.VMEM((ns, 2, past, LANES), BF16), pltpu.VMEM((ns, heads, past, LANES), BF16),
                    pltpu.VMEM((ns, 2, past, LANES), BF16), pltpu.VMEM((ns, heads, past, LANES), BF16)]
    in_specs.append(pl.BlockSpec(norm_g_tiled.shape, lambda i, j: (0, 0)))
    args.append(norm_g_tiled)
    y = pl.pallas_call(
        functools.partial(_attention_kernel, seq_len=seq_len, has_ctx=has_ctx, seqs_per_step=ns),
        grid=(b // ns, nq),
        in_specs=in_specs,
        out_specs=pl.BlockSpec((ns, tq, 2 * GROUP_W), lambda i, j: (i, j, 0)),
        out_shape=jax.ShapeDtypeStruct((b, seq_len, 2 * GROUP_W), BF16),
        scratch_shapes=scratch,
        compiler_params=pltpu.CompilerParams(
            dimension_semantics=("arbitrary", "arbitrary"), vmem_limit_bytes=VMEM_LIMIT),
        name="attention_latent" if has_ctx else "attention_ctx",
    )(*args)
    return y.reshape(t, 2 * GROUP_W)


def _out_mlp_kernel(x_ref, yr_ref, ya_ref, mod_ref, wo_ref, g_ref, w1_ref, w2_ref, gf_ref, o_ref, *, final):
    g1 = mod_ref[:, 2 * D_MODEL:3 * D_MODEL]
    sh2 = mod_ref[:, 3 * D_MODEL:4 * D_MODEL]
    sc2 = mod_ref[:, 4 * D_MODEL:5 * D_MODEL]
    g2 = mod_ref[:, 5 * D_MODEL:6 * D_MODEL]
    y = _dot(yr_ref[...], wo_ref[0:2 * GROUP_W, :]) + _dot(ya_ref[...], wo_ref[2 * GROUP_W:4 * GROUP_W, :])
    x = x_ref[...] + g1 * y
    h = (_rms(x) * g_ref[...] * (1.0 + sc2) + sh2).astype(BF16)
    acc = None
    for c in range(D_FF // D_MODEL):
        sl = slice(c * D_MODEL, (c + 1) * D_MODEL)
        hid = jnp.square(jnp.maximum(_dot(h, w1_ref[:, sl]), 0.0)).astype(BF16)
        part = _dot(hid, w2_ref[sl, :])
        acc = part if acc is None else acc + part
    x = x + g2 * acc
    if final:
        x = _rms(x) * gf_ref[...]
    o_ref[...] = x


def _out_mlp(x, y_rec, y_att, mod4, w_out_b, norm_g, w_ff1_b, w_ff2_b, final_g, l, seq_len, is_latent, final):
    t = x.shape[0]
    tm = MLP_ROW_TILE
    if is_latent:
        tiles_per_seq = seq_len // tm
        mod_map = lambda i: (l, 1 + i // tiles_per_seq, 0, 0)
    else:
        mod_map = lambda i: (l, 0, 0, 0)
    once = pl.Buffered(1)
    in_specs = [
        pl.BlockSpec((tm, D_MODEL), lambda i: (i, 0)),
        pl.BlockSpec((tm, 2 * GROUP_W), lambda i: (i, 0)),
        pl.BlockSpec((tm, 2 * GROUP_W), lambda i: (i, 0)),
        pl.BlockSpec((None, None, 1, N_MOD * D_MODEL), mod_map),
        pl.BlockSpec((None, D_MODEL, D_MODEL), lambda i: (l, 0, 0), pipeline_mode=once),
        pl.BlockSpec((None, 1, D_MODEL), lambda i: (l, 0, 0)),
        pl.BlockSpec((None, D_MODEL, D_FF), lambda i: (l, 0, 0), pipeline_mode=once),
        pl.BlockSpec((None, D_FF, D_MODEL), lambda i: (l, 0, 0), pipeline_mode=once),
        pl.BlockSpec((1, D_MODEL), lambda i: (0, 0)),
    ]
    return pl.pallas_call(
        functools.partial(_out_mlp_kernel, final=final),
        grid=(t // tm,),
        in_specs=in_specs,
        out_specs=pl.BlockSpec((tm, D_MODEL), lambda i: (i, 0)),
        out_shape=jax.ShapeDtypeStruct((t, D_MODEL), F32),
        compiler_params=pltpu.CompilerParams(
            dimension_semantics=("arbitrary",), vmem_limit_bytes=VMEM_LIMIT),
        name="out_mlp_final" if final else "out_mlp",
    )(x, y_rec, y_att, mod4, w_out_b, norm_g, w_ff1_b, w_ff2_b, final_g)


def _rope_tables(length):
    row = (jnp.arange(length) // GRID_W).astype(F32)
    col = (jnp.arange(length) % GRID_W).astype(F32)
    tabs = []
    for dim in (HEAD_DIM, DIFF_QK_DIM):
        n = dim // 4
        inv = ROPE_BASE ** (-jnp.arange(n, dtype=F32) / n)
        ang = jnp.concatenate([row[:, None] * inv, col[:, None] * inv], -1)
        cos, sin = jnp.cos(ang), jnp.sin(ang)
        reps = LANES // dim
        tabs.append(jnp.tile(jnp.concatenate([cos, cos], -1), (1, reps)))
        tabs.append(jnp.tile(jnp.concatenate([-sin, sin], -1), (1, reps)))
    return tabs


def _block_diag(w):
    n, c, d = w.shape
    eye = jnp.eye(n, dtype=w.dtype)
    return jnp.einsum("ncd,nm->ncmd", w, eye).reshape(n * c, n * d)


def _layer_params(l, ret_decay, ret_gn_g, lru_conv_w, lru_conv_b, lru_w_a, lru_b_a, lru_w_x, lru_b_x, lru_lambda,
                  swa_sink, diff_lambda, diff_norm_g):
    heads = GROUP_W // HEAD_DIM
    log_gamma = jax.nn.log_sigmoid(ret_decay[l].astype(F32))
    lambda_init = 0.8 - 0.6 * math.exp(-0.3 * l)
    lv = diff_lambda[l].astype(F32)
    lam = jnp.exp(jnp.sum(lv[0] * lv[1])) - jnp.exp(jnp.sum(lv[2] * lv[3])) + lambda_init
    scal = jnp.concatenate([log_gamma[0], log_gamma[1], swa_sink[l].astype(F32),
                            jnp.stack([lam, jnp.asarray(1.0 - lambda_init, F32)]),
                            jnp.zeros((2,), F32)])
    wbd = jnp.concatenate([_block_diag(lru_w_a[l, 0]), _block_diag(lru_w_x[l, 0]),
                           _block_diag(lru_w_a[l, 1]), _block_diag(lru_w_x[l, 1])], axis=1).astype(BF16)
    bias = jnp.concatenate([lru_b_a[l, 0], lru_b_x[l, 0], lru_b_a[l, 1], lru_b_x[l, 1]])[None, :]
    return dict(
        scal=scal,
        lg_lane=jnp.repeat(log_gamma, HEAD_DIM, axis=1),
        gn_g=ret_gn_g[l][None, :],
        conv_w=lru_conv_w[l],
        conv_b=lru_conv_b[l][None, :],
        lru_wbd=wbd,
        lru_bias=bias,
        lru_lam=lru_lambda[l],
        diff_ng=jnp.tile(diff_norm_g[l], heads)[None, :],
    )


def kernel(x_prompt, x_sample, c, state_ret, state_lru, cache_swa_k, cache_swa_v, cache_diff_k, cache_diff_v,
           c_ctx, w_ada, b_ada, norm_mix_g, w_in, ret_decay, ret_gn_g, lru_conv_w, lru_conv_b,
           lru_w_a, lru_b_a, lru_w_x, lru_b_x, lru_lambda, swa_sink, diff_lambda, diff_norm_g,
           w_out, norm_mlp_g, w_ff1, w_ff2, final_norm_g):
    batch, seq, _ = x_prompt.shape
    dec_batch, dec_seq, _ = x_sample.shape
    past = cache_swa_k.shape[2]

    cvec = jnp.concatenate([c_ctx[None, :], c, jnp.zeros((MOD_ROWS - 1 - dec_batch, D_MODEL), F32)], axis=0)
    mod4 = _modulation(cvec, w_ada, b_ada).reshape(DEPTH, MOD_ROWS, 1, N_MOD * D_MODEL)

    w_in_b = w_in.astype(BF16)
    w_out_b = w_out.astype(BF16)
    w_ff1_b = w_ff1.astype(BF16)
    w_ff2_b = w_ff2.astype(BF16)
    norm_mix3 = norm_mix_g.reshape(DEPTH, 1, D_MODEL)
    norm_mlp3 = norm_mlp_g.reshape(DEPTH, 1, D_MODEL)
    final_g = final_norm_g[None, :]
    rope_tabs = _rope_tables(dec_seq)
    layers = [_layer_params(l, ret_decay, ret_gn_g, lru_conv_w, lru_conv_b, lru_w_a, lru_b_a, lru_w_x, lru_b_x,
                            lru_lambda, swa_sink, diff_lambda, diff_norm_g) for l in range(DEPTH)]

    heads = GROUP_W // HEAD_DIM

    xp = x_prompt.reshape(batch * seq, D_MODEL)
    new_ret, new_lru, new_swa, new_diff = [], [], [], []
    for l in range(DEPTH):
        lp = layers[l]
        zr, zl, qa, kvs, kvd = _in_proj(xp, mod4, norm_mix3, w_in_b, l, seq, None)
        y_rec, s_ret, s_lru = _recurrent(lp["scal"], zr, zl, lp, l, seq)
        y_att = _attention(lp["scal"], qa, kvs, kvd, lp["diff_ng"], l, seq)
        xp = _out_mlp(xp, y_rec, y_att, mod4, w_out_b, norm_mlp3, w_ff1_b, w_ff2_b, final_g,
                      l, seq, False, l == DEPTH - 1)
        new_ret.append(s_ret)
        new_lru.append(s_lru)
        new_swa.append(kvs)
        new_diff.append(kvd)
    y_prompt = xp.reshape(batch, seq, D_MODEL)
    new_ret = jnp.stack(new_ret, axis=1)
    new_swa = jnp.stack(new_swa, axis=1).reshape(batch, seq, DEPTH, 2, 2, HEAD_DIM)
    new_swa = jnp.transpose(new_swa, (3, 0, 2, 1, 4, 5))
    new_diff = jnp.stack(new_diff, axis=1).reshape(batch, seq, DEPTH, 2, heads, HEAD_DIM)
    new_diff = jnp.transpose(new_diff, (3, 0, 2, 1, 4, 5))

    xs = x_sample.reshape(dec_batch * dec_seq, D_MODEL)
    state = (state_ret, state_lru)
    caches = (cache_swa_k.reshape(dec_batch, DEPTH, past, 2 * HEAD_DIM),
              cache_swa_v.reshape(dec_batch, DEPTH, past, 2 * HEAD_DIM),
              cache_diff_k.reshape(dec_batch, DEPTH, past, GROUP_W),
              cache_diff_v.reshape(dec_batch, DEPTH, past, GROUP_W))
    for l in range(DEPTH):
        lp = layers[l]
        zr, zl, qa, kvs, kvd = _in_proj(xs, mod4, norm_mix3, w_in_b, l, dec_seq, rope_tabs)
        (y_rec,) = _recurrent(lp["scal"], zr, zl, lp, l, dec_seq, state)
        y_att = _attention(lp["scal"], qa, kvs, kvd, lp["diff_ng"], l, dec_seq, caches)
        xs = _out_mlp(xs, y_rec, y_att, mod4, w_out_b, norm_mlp3, w_ff1_b, w_ff2_b, final_g,
                      l, dec_seq, True, l == DEPTH - 1)
    y_sample = xs.reshape(dec_batch, dec_seq, D_MODEL)

    return (y_prompt, y_sample, new_ret, jnp.stack(new_lru, axis=1),
            new_swa[0], new_swa[1], new_diff[0], new_diff[1])
```

```python
import functools
import math

import jax
import jax.numpy as jnp
from jax import lax
from jax.experimental import pallas as pl
from jax.experimental.pallas import tpu as pltpu

F32 = jnp.float32
BF16 = jnp.bfloat16

D_MODEL = 1024
DEPTH = 4
GRID_W = 64
HEAD_DIM = 64
GROUP_W = D_MODEL // 4
RET_CHUNK = 256
LRU_C = 8.0
SWA_WINDOW = 128
DIFF_QK_DIM = HEAD_DIM // 2
D_FF = 4 * D_MODEL
ROPE_BASE = 10000.0
EPS = 1e-6
N_MOD = 6
D_IN = 11 * GROUP_W
MOD_ROWS = 16

LANES = 128
SUBLANES = 8
VMEM_LIMIT = 56 * 1024 * 1024
NEG_BIG = -1e30
LOG2E = math.log2(math.e)
SWA_QSCALE = (HEAD_DIM ** -0.5) * LOG2E
DIFF_QSCALE = (DIFF_QK_DIM ** -0.5) * LOG2E

ROW_TILE = 1024
ATTN_Q_TILE = 256
LATENT_SEQS_PER_STEP = 4
CTX_SEQS_PER_STEP = 8
ATTN_LOOKAHEAD_LATENT = 2
ATTN_LOOKAHEAD_CTX = 1


def _dot(a, b):
    return jnp.dot(a, b, preferred_element_type=F32)


def _dot_nt(a, b):
    return lax.dot_general(a, b, (((1,), (1,)), ((), ())), preferred_element_type=F32)


def _dot_tn(a, b):
    return lax.dot_general(a, b, (((0,), (0,)), ((), ())), preferred_element_type=F32)


def _rms(x):
    return x * lax.rsqrt(jnp.mean(x * x, axis=-1, keepdims=True) + EPS)


def _sigmoid(x):
    return 0.5 * jnp.tanh(0.5 * x) + 0.5


def _lane(shape):
    return lax.broadcasted_iota(jnp.int32, shape, len(shape) - 1)


def _mod_kernel(c_ref, w_ref, b_ref, o_ref):
    s = jax.nn.silu(c_ref[...]).astype(BF16)
    o_ref[...] = _dot(s, w_ref[...].astype(BF16)) + b_ref[...]


def _modulation(cvec, w_ada, b_ada):
    tn = 1536
    n_mod = N_MOD * D_MODEL
    return pl.pallas_call(
        _mod_kernel,
        grid=(DEPTH, n_mod // tn),
        in_specs=[
            pl.BlockSpec((MOD_ROWS, D_MODEL), lambda l, j: (0, 0)),
            pl.BlockSpec((None, D_MODEL, tn), lambda l, j: (l, 0, j)),
            pl.BlockSpec((None, 1, tn), lambda l, j: (l, 0, j)),
        ],
        out_specs=pl.BlockSpec((None, MOD_ROWS, tn), lambda l, j: (l, 0, j)),
        out_shape=jax.ShapeDtypeStruct((DEPTH, MOD_ROWS, n_mod), F32),
        compiler_params=pltpu.CompilerParams(
            dimension_semantics=("arbitrary", "arbitrary"), vmem_limit_bytes=VMEM_LIMIT),
        name="modulation",
    )(cvec, w_ada, b_ada.reshape(DEPTH, 1, n_mod))


def _rope_slab(x, c, s, half):
    first = (_lane(x.shape) & (2 * half - 1)) < half
    swapped = jnp.where(first, pltpu.roll(x, LANES - half, axis=1), pltpu.roll(x, half, axis=1))
    return x * c + swapped * s


def _in_proj_kernel(*refs, rope):
    if rope:
        (x_ref, mod_ref, g_ref, w_ref, c64_ref, s64_ref, c32_ref, s32_ref,
         zr_ref, zl_ref, qa_ref, kvs_ref, kvd_ref) = refs
    else:
        x_ref, mod_ref, g_ref, w_ref, zr_ref, zl_ref, qa_ref, kvs_ref, kvd_ref = refs
    sh1 = mod_ref[:, 0:D_MODEL]
    sc1 = mod_ref[:, D_MODEL:2 * D_MODEL]
    h = (_rms(x_ref[...]) * g_ref[...] * (1.0 + sc1) + sh1).astype(BF16)
    zs = _dot(h, w_ref[:, 6 * GROUP_W:8 * GROUP_W])
    zd = _dot(h, w_ref[:, 8 * GROUP_W:11 * GROUP_W])
    if rope:
        c64, s64, c32, s32 = c64_ref[...], s64_ref[...], c32_ref[...], s32_ref[...]
        rope_s = lambda v: _rope_slab(v, c64, s64, HEAD_DIM // 2)
        rope_d = lambda v: _rope_slab(v, c32, s32, DIFF_QK_DIM // 2)
    else:
        rope_s = rope_d = lambda v: v
    out = qa_ref.dtype
    for i in range(2):
        sl = slice(i * LANES, (i + 1) * LANES)
        qa_ref[:, sl] = (rope_s(zs[:, sl]) * SWA_QSCALE).astype(out)
        qa_ref[:, GROUP_W + i * LANES:GROUP_W + (i + 1) * LANES] = (rope_d(zd[:, sl]) * DIFF_QSCALE).astype(out)
        kvd_ref[:, sl] = rope_d(zd[:, GROUP_W + i * LANES:GROUP_W + (i + 1) * LANES]).astype(out)
    kvs_ref[:, 0:LANES] = rope_s(zs[:, 2 * LANES:3 * LANES]).astype(out)
    kvs_ref[:, LANES:2 * LANES] = zs[:, 3 * LANES:4 * LANES].astype(out)
    kvd_ref[:, GROUP_W:2 * GROUP_W] = zd[:, 2 * GROUP_W:3 * GROUP_W].astype(out)
    zl_ref[...] = _dot(h, w_ref[:, 4 * GROUP_W:6 * GROUP_W])
    zr_ref[...] = _dot(h, w_ref[:, 0:4 * GROUP_W])


def _in_proj(x, mod4, norm_g, w_in_b, l, seq_len, rope_tabs):
    t = x.shape[0]
    tm = ROW_TILE
    rope = rope_tabs is not None
    if rope:
        tiles_per_seq = seq_len // tm
        mod_map = lambda i: (l, 1 + i // tiles_per_seq, 0, 0)
    else:
        mod_map = lambda i: (l, 0, 0, 0)
    in_specs = [
        pl.BlockSpec((tm, D_MODEL), lambda i: (i, 0)),
        pl.BlockSpec((None, None, 1, N_MOD * D_MODEL), mod_map),
        pl.BlockSpec((None, 1, D_MODEL), lambda i: (l, 0, 0)),
        pl.BlockSpec((None, D_MODEL, D_IN), lambda i: (l, 0, 0)),
    ]
    args = [x, mod4, norm_g, w_in_b]
    if rope:
        tab_spec = pl.BlockSpec((tm, LANES), lambda i: (i % tiles_per_seq, 0))
        in_specs += [tab_spec] * 4
        args += list(rope_tabs)
    att = BF16 if rope else F32
    outs = ((4 * GROUP_W, F32), (2 * GROUP_W, F32), (2 * GROUP_W, att), (GROUP_W, att), (2 * GROUP_W, att))
    return pl.pallas_call(
        functools.partial(_in_proj_kernel, rope=rope),
        grid=(t // tm,),
        in_specs=in_specs,
        out_specs=[pl.BlockSpec((tm, w), lambda i: (i, 0)) for w, _ in outs],
        out_shape=[jax.ShapeDtypeStruct((t, w), dt) for w, dt in outs],
        compiler_params=pltpu.CompilerParams(
            dimension_semantics=("arbitrary",), vmem_limit_bytes=VMEM_LIMIT),
        name="in_proj_rope" if rope else "in_proj",
    )(*args)


def _scan_levels(a_slabs, u_slabs, pa_ref, pb_ref, sa_ref, sb_ref, h0, n0, reverse):
    order = list(range(SUBLANES - 1, -1, -1)) if reverse else list(range(SUBLANES))
    acc_a = acc_b = None
    for r in order:
        if acc_a is None:
            acc_a, acc_b = a_slabs[r], u_slabs[r]
        else:
            acc_b = a_slabs[r] * acc_b + u_slabs[r]
            acc_a = a_slabs[r] * acc_a
        pa_ref[r] = acc_a
        pb_ref[r] = acc_b
    pad = n0 // 2
    ident = pad + n0 if reverse else 0
    sa_ref[ident:ident + pad, :] = jnp.ones((pad, GROUP_W), F32)
    sb_ref[ident:ident + pad, :] = jnp.zeros((pad, GROUP_W), F32)
    sign = 1 if reverse else -1
    k = 1
    while k < n0:
        sa_ref[pad:pad + n0, :] = acc_a
        sb_ref[pad:pad + n0, :] = acc_b
        off = pad + sign * k
        acc_b = acc_a * sb_ref[off:off + n0, :] + acc_b
        acc_a = acc_a * sa_ref[off:off + n0, :]
        k *= 2
    sa_ref[pad:pad + n0, :] = acc_a
    sb_ref[pad:pad + n0, :] = acc_b
    off = pad + sign
    carry_in = sa_ref[off:off + n0, :] * h0 + sb_ref[off:off + n0, :]
    end = 0 if reverse else n0 - 1
    final = acc_a[end:end + 1, :] * h0 + acc_b[end:end + 1, :]
    h_slabs = [pa_ref[r] * carry_in + pb_ref[r] for r in range(SUBLANES)]
    return h_slabs, final


def _recurrent_kernel(*refs, seq_len, has_state):
    if has_state:
        (scal_ref, zr_ref, zl_ref, s0_ref, h0_ref, lgl_ref, gng_ref, cw_ref, cb_ref, wbd_ref, lb_ref, lam_ref,
         y_ref,
         d_ref, vec_ref, pad_ref, a_ref, u_ref, hf_ref, hs_ref, pa_ref, pb_ref, sa_ref, sb_ref) = refs
    else:
        (scal_ref, zr_ref, zl_ref, lgl_ref, gng_ref, cw_ref, cb_ref, wbd_ref, lb_ref, lam_ref,
         y_ref, sret_ref, slru_ref,
         d_ref, vec_ref, pad_ref, a_ref, u_ref, hf_ref, hs_ref, pa_ref, pb_ref, sa_ref, sb_ref) = refs
    ch = RET_CHUNK
    n_chunks = seq_len // ch
    n0 = seq_len // SUBLANES
    heads = GROUP_W // HEAD_DIM

    @pl.when(pl.program_id(0) == 0)
    def _():
        n = lax.broadcasted_iota(jnp.int32, (ch, ch), 0)
        m = lax.broadcasted_iota(jnp.int32, (ch, ch), 1)
        dist = (n - m).astype(F32)
        for h in range(heads):
            lg = jnp.where(dist > 0, scal_ref[h], scal_ref[heads + h])
            d_ref[h] = jnp.where(dist == 0, 2.0, jnp.exp(jnp.abs(dist) * lg))
        t = lax.broadcasted_iota(jnp.int32, (ch, GROUP_W), 0).astype(F32)
        lgf = lgl_ref[0:1, :]
        lgb = lgl_ref[1:2, :]
        vec_ref[0] = jnp.exp((t + 1.0) * lgf)
        vec_ref[1] = jnp.exp((ch - t) * lgb)
        vec_ref[2] = jnp.exp((ch - 1.0 - t) * lgf)
        vec_ref[3] = jnp.exp(t * lgb)

    low = _lane((1, LANES)) < HEAD_DIM
    rr = lax.broadcasted_iota(jnp.int32, (LANES, LANES), 0) < HEAD_DIM
    cc = lax.broadcasted_iota(jnp.int32, (LANES, LANES), 1) < HEAD_DIM
    block_diag = (rr == cc).astype(F32)
    g_chunk_f = jnp.exp(ch * lgl_ref[0:1, :])
    g_chunk_b = jnp.exp(ch * lgl_ref[1:2, :])

    def kv(c, p):
        rows = slice(c * ch, (c + 1) * ch)
        k = zr_ref[rows, GROUP_W + p * LANES:GROUP_W + (p + 1) * LANES] * (HEAD_DIM ** -0.5)
        v = zr_ref[rows, 2 * GROUP_W + p * LANES:2 * GROUP_W + (p + 1) * LANES].astype(BF16)
        return k, v

    def state_update(k, v, zeta):
        return _dot_tn((k * zeta).astype(BF16), v) * block_diag

    def pair_state(d, p):
        zero = jnp.zeros((HEAD_DIM, HEAD_DIM), F32)
        top = jnp.concatenate([s0_ref[d, 2 * p], zero], axis=1)
        bottom = jnp.concatenate([zero, s0_ref[d, 2 * p + 1]], axis=1)
        return jnp.concatenate([top, bottom], axis=0)

    def store_state(d, p, s_pair):
        sret_ref[d, 2 * p] = s_pair[0:HEAD_DIM, 0:HEAD_DIM]
        sret_ref[d, 2 * p + 1] = s_pair[HEAD_DIM:LANES, HEAD_DIM:LANES]

    for p in range(2):
        pl_sl = slice(p * LANES, (p + 1) * LANES)
        if has_state:
            s_b = [None] * n_chunks
            s_b[n_chunks - 1] = pair_state(1, p)
            for c in range(n_chunks - 1, 0, -1):
                k, v = kv(c, p)
                s_b[c - 1] = s_b[c] * g_chunk_b[:, pl_sl] + state_update(k, v, vec_ref[3, :, pl_sl])
            s_f = pair_state(0, p)
        for c in range(n_chunks):
            rows = slice(c * ch, (c + 1) * ch)
            q = zr_ref[rows, pl_sl]
            k, v = kv(c, p)
            kb = k.astype(BF16)
            outs = []
            for j in range(2):
                qm = jnp.where(low if j == 0 else ~low, q, 0.0).astype(BF16)
                w = (_dot_nt(qm, kb) * d_ref[2 * p + j]).astype(BF16)
                outs.append(_dot(w, v))
            o = jnp.where(low, outs[0], outs[1])
            if has_state:
                qb = q.astype(BF16)
                o = o + _dot(qb, s_f.astype(BF16)) * vec_ref[0, :, pl_sl]
                o = o + _dot(qb, s_b[c].astype(BF16)) * vec_ref[1, :, pl_sl]
                if c + 1 < n_chunks:
                    s_f = s_f * g_chunk_f[:, pl_sl] + state_update(k, v, vec_ref[2, :, pl_sl])
            else:
                store_state(0, p, state_update(k, v, vec_ref[2, :, pl_sl]))
                store_state(1, p, state_update(k, v, vec_ref[3, :, pl_sl]))
            inv = 1.0 / HEAD_DIM
            mu = jnp.where(low, jnp.sum(jnp.where(low, o, 0.0), -1, keepdims=True),
                           jnp.sum(jnp.where(low, 0.0, o), -1, keepdims=True)) * inv
            dlt = o - mu
            sq = dlt * dlt
            var = jnp.where(low, jnp.sum(jnp.where(low, sq, 0.0), -1, keepdims=True),
                            jnp.sum(jnp.where(low, 0.0, sq), -1, keepdims=True)) * inv
            gate = zr_ref[rows, 3 * GROUP_W + p * LANES:3 * GROUP_W + (p + 1) * LANES]
            y = dlt * lax.rsqrt(var + EPS) * gng_ref[:, pl_sl] * (gate * _sigmoid(gate))
            y_ref[rows, pl_sl] = y.astype(BF16)

    x = zl_ref[:, 0:GROUP_W]
    zero8 = jnp.zeros((SUBLANES, GROUP_W), F32)
    pad_ref[0:SUBLANES, :] = zero8
    pad_ref[SUBLANES:SUBLANES + seq_len, :] = x
    pad_ref[SUBLANES + seq_len:2 * SUBLANES + seq_len, :] = zero8
    cw = cw_ref[...]
    xc = (cw[0:1] * pad_ref[SUBLANES - 2:SUBLANES - 2 + seq_len, :]
          + cw[1:2] * pad_ref[SUBLANES - 1:SUBLANES - 1 + seq_len, :]
          + cw[2:3] * x
          + cw[3:4] * pad_ref[SUBLANES + 1:SUBLANES + 1 + seq_len, :]
          + cb_ref[...])
    pre = _dot(xc.astype(BF16), wbd_ref[...]) + lb_ref[...]
    for d in range(2):
        r_gate = _sigmoid(pre[:, (2 * d) * GROUP_W:(2 * d + 1) * GROUP_W])
        i_gate = _sigmoid(pre[:, (2 * d + 1) * GROUP_W:(2 * d + 2) * GROUP_W])
        log_a = -LRU_C * r_gate * jax.nn.softplus(-lam_ref[d:d + 1, :])
        a_val = jnp.exp(log_a)
        one_minus_a2 = -jnp.tanh(log_a) * (a_val * a_val + 1.0)
        root = jnp.where(one_minus_a2 > 0.0, one_minus_a2 * lax.rsqrt(one_minus_a2), 0.0)
        u_val = root * (i_gate * xc)
        for hh in range(2):
            a_ref[d, hh] = a_val[:, hh * LANES:(hh + 1) * LANES]
            u_ref[d, hh] = u_val[:, hh * LANES:(hh + 1) * LANES]

    def slab(ref, d, r):
        return jnp.concatenate([ref[d, hh, pl.ds(r, n0, stride=SUBLANES), :] for hh in range(2)], axis=1)

    finals = []
    for d in range(2):
        a_slabs = [slab(a_ref, d, r) for r in range(SUBLANES)]
        u_slabs = [slab(u_ref, d, r) for r in range(SUBLANES)]
        h0 = h0_ref[d:d + 1, :] if has_state else jnp.zeros((1, GROUP_W), F32)
        h_slabs, fin = _scan_levels(a_slabs, u_slabs, pa_ref, pb_ref, sa_ref, sb_ref, h0, n0, reverse=(d == 1))
        finals.append(fin)
        for r in range(SUBLANES):
            if d == 0:
                hf_ref[r] = h_slabs[r]
            else:
                h_sum = hf_ref[r] + h_slabs[r]
                for hh in range(2):
                    hs_ref[hh, pl.ds(r, n0, stride=SUBLANES), :] = h_sum[:, hh * LANES:(hh + 1) * LANES]
    h_both = jnp.concatenate([hs_ref[0], hs_ref[1]], axis=1)
    y = h_both * jax.nn.gelu(zl_ref[:, GROUP_W:2 * GROUP_W])
    y_ref[:, GROUP_W:2 * GROUP_W] = y.astype(BF16)
    if not has_state:
        slru_ref[0:1, :] = finals[0]
        slru_ref[1:2, :] = finals[1]


def _recurrent(scal, zr, zl, lp, l, seq_len, state=None):
    t = zr.shape[0]
    b = t // seq_len
    n0 = seq_len // SUBLANES
    has_state = state is not None
    assert has_state or seq_len == RET_CHUNK
    zr3 = zr.reshape(b, seq_len, 4 * GROUP_W)
    zl3 = zl.reshape(b, seq_len, 2 * GROUP_W)
    full = lambda a: pl.BlockSpec(a.shape, lambda i: (0,) * a.ndim)
    in_specs = [
        pl.BlockSpec(memory_space=pltpu.SMEM),
        pl.BlockSpec((None, seq_len, 4 * GROUP_W), lambda i: (i, 0, 0)),
        pl.BlockSpec((None, seq_len, 2 * GROUP_W), lambda i: (i, 0, 0)),
    ]
    args = [scal, zr3, zl3]
    if has_state:
        s0, h0 = state
        heads = GROUP_W // HEAD_DIM
        in_specs += [pl.BlockSpec((None, None, 2, heads, HEAD_DIM, HEAD_DIM), lambda i: (i, l, 0, 0, 0, 0)),
                     pl.BlockSpec((None, None, 2, GROUP_W), lambda i: (i, l, 0, 0))]
        args += [s0, h0]
    params = [lp["lg_lane"], lp["gn_g"], lp["conv_w"], lp["conv_b"], lp["lru_wbd"], lp["lru_bias"], lp["lru_lam"]]
    in_specs += [full(a) for a in params]
    args += params
    out_specs = [pl.BlockSpec((None, seq_len, 2 * GROUP_W), lambda i: (i, 0, 0))]
    out_shape = [jax.ShapeDtypeStruct((b, seq_len, 2 * GROUP_W), BF16)]
    if not has_state:
        heads = GROUP_W // HEAD_DIM
        out_specs += [pl.BlockSpec((None, 2, heads, HEAD_DIM, HEAD_DIM), lambda i: (i, 0, 0, 0, 0)),
                      pl.BlockSpec((None, 2, GROUP_W), lambda i: (i, 0, 0))]
        out_shape += [jax.ShapeDtypeStruct((b, 2, heads, HEAD_DIM, HEAD_DIM), F32),
                      jax.ShapeDtypeStruct((b, 2, GROUP_W), F32)]
    scratch = [
        pltpu.VMEM((GROUP_W // HEAD_DIM, RET_CHUNK, RET_CHUNK), F32),
        pltpu.VMEM((4, RET_CHUNK, GROUP_W), F32),
        pltpu.VMEM((seq_len + 2 * SUBLANES, GROUP_W), F32),
        pltpu.VMEM((2, 2, seq_len, LANES), F32),
        pltpu.VMEM((2, 2, seq_len, LANES), F32),
        pltpu.VMEM((SUBLANES, n0, GROUP_W), F32),
        pltpu.VMEM((2, seq_len, LANES), F32),
        pltpu.VMEM((SUBLANES, n0, GROUP_W), F32),
        pltpu.VMEM((SUBLANES, n0, GROUP_W), F32),
        pltpu.VMEM((2 * n0, GROUP_W), F32),
        pltpu.VMEM((2 * n0, GROUP_W), F32),
    ]
    outs = pl.pallas_call(
        functools.partial(_recurrent_kernel, seq_len=seq_len, has_state=has_state),
        grid=(b,),
        in_specs=in_specs,
        out_specs=out_specs,
        out_shape=out_shape,
        scratch_shapes=scratch,
        compiler_params=pltpu.CompilerParams(
            dimension_semantics=("arbitrary",), vmem_limit_bytes=VMEM_LIMIT),
        name="recurrent_latent" if has_state else "recurrent_ctx",
    )(*args)
    return (outs[0].reshape(t, 2 * GROUP_W),) + tuple(outs[1:])


def _dup_half(x, half):
    sw = pltpu.roll(x, HEAD_DIM, axis=1)
    low = _lane(x.shape) < HEAD_DIM
    return jnp.where(low, x, sw) if half == 0 else jnp.where(low, sw, x)


def _attention_kernel(*refs, seq_len, has_ctx, seqs_per_step):
    n_in = 9 if has_ctx else 5
    scal_ref, ng_ref = refs[0], refs[n_in - 1]
    per_seq = refs[1:n_in - 1] + refs[n_in:]
    per_sequence = [_attention_stages(scal_ref, ng_ref, *[r.at[s] for r in per_seq], seq_len=seq_len,
                                      has_ctx=has_ctx) for s in range(seqs_per_step)]
    stages = [st for seq_stages in per_sequence for st in seq_stages]
    pending = []
    for make_weights, make_out, p in stages:
        pending.append((make_out, p, make_weights(p)))
        if len(pending) > (ATTN_LOOKAHEAD_LATENT if has_ctx else ATTN_LOOKAHEAD_CTX):
            make_out_, p_, w_ = pending.pop(0)
            make_out_(p_, w_)
    for make_out_, p_, w_ in pending:
        make_out_(p_, w_)


def _attention_stages(scal_ref, ng_ref, *refs, seq_len, has_ctx):
    if has_ctx:
        (qa_ref, kvs_ref, kvd_ref, cks_ref, cvs_ref, ckd_ref, cvd_ref, y_ref,
         sk_s, sv_s, dk_s, dv_s, csk_s, csv_s, cdk_s, cdv_s) = refs
    else:
        (qa_ref, kvs_ref, kvd_ref, y_ref, sk_s, sv_s, dk_s, dv_s) = refs
    tq = min(ATTN_Q_TILE, seq_len)
    heads = GROUP_W // HEAD_DIM
    lane = _lane((1, LANES))
    low = lane < HEAD_DIM
    halves = (low, ~low)

    def build():
        def fill(k_all, v_all, k_dst, v_dst, dup):
            for p in range(2):
                if dup:
                    k_p, v_p = _dup_half(k_all, p), _dup_half(v_all, p)
                else:
                    k_p, v_p = k_all[:, p * LANES:(p + 1) * LANES], v_all[:, p * LANES:(p + 1) * LANES]
                k_dst[p] = k_p.astype(BF16)
                for j in range(2):
                    v_dst[2 * p + j] = jnp.where(halves[j], v_p, 1.0).astype(BF16)

        fill(kvs_ref[:, 0:LANES].astype(F32), kvs_ref[:, LANES:2 * LANES].astype(F32), sk_s, sv_s, True)
        fill(kvd_ref[:, 0:GROUP_W].astype(F32), kvd_ref[:, GROUP_W:2 * GROUP_W].astype(F32), dk_s, dv_s, False)
        if has_ctx:
            fill(cks_ref[...], cvs_ref[...], csk_s, csv_s, True)
            fill(ckd_ref[...], cvd_ref[...], cdk_s, cdv_s, False)

    if seq_len == tq:
        build()
    else:
        pl.when(pl.program_id(1) == 0)(build)

    def weights(q_rows, k_loc, k_ctx, valid, extra2):
        s = _dot_nt(q_rows, k_loc)
        if valid is not None:
            s = jnp.where(valid, s, NEG_BIG)
        m = jnp.max(s, -1, keepdims=True)
        sc = None
        if k_ctx is not None:
            sc = _dot_nt(q_rows, k_ctx)
            m = jnp.maximum(m, jnp.max(sc, -1, keepdims=True))
        if extra2 is not None:
            m = jnp.maximum(m, extra2)
        e = jnp.exp2((s - m).astype(BF16))
        ec = None if sc is None else jnp.exp2((sc - m).astype(BF16))
        extra = None if extra2 is None else jnp.exp2(extra2 - m)
        return e, ec, extra

    def weighted_values(w, v_of_head):
        e, ec, extra = w
        half_rows = e.shape[0] // 2
        res = []
        for j in range(2):
            rs = slice(j * half_rows, (j + 1) * half_rows)
            v_loc, v_ctx = v_of_head(j)
            o = _dot(e[rs], v_loc)
            if ec is not None:
                o = o + _dot(ec[rs], v_ctx)
            den = pltpu.roll(o, HEAD_DIM, axis=1)
            if extra is not None:
                den = den + extra[rs]
            res.append(o / jnp.where(halves[j], den, 1.0))
        return res

    if has_ctx:
        win = tq + 2 * SWA_WINDOW
        i = pl.program_id(1)
        blk = jnp.clip(i * (tq // SWA_WINDOW) - 1, 0, (seq_len - win) // SWA_WINDOW)
        start = pl.multiple_of(blk * SWA_WINDOW, SWA_WINDOW)
        rows = pl.ds(start, win)
        qpos = i * tq + (lax.broadcasted_iota(jnp.int32, (2 * tq, win), 0) & (tq - 1))
        kpos = start + lax.broadcasted_iota(jnp.int32, (2 * tq, win), 1)
        valid = jnp.abs(kpos - qpos) <= SWA_WINDOW
    else:
        rows = slice(None)
        valid = None
    first_member = lax.broadcasted_iota(jnp.int32, (2 * tq, 1), 0) < tq
    lam = scal_ref[3 * heads]
    coef = scal_ref[3 * heads + 1]
    grp = lane >> 5

    def swa_weights(p):
        q = qa_ref[:, p * LANES:(p + 1) * LANES]
        q_rows = jnp.concatenate([jnp.where(halves[j], q, 0.0).astype(BF16) for j in range(2)], axis=0)
        sink2 = jnp.where(first_member, scal_ref[2 * heads + 2 * p], scal_ref[2 * heads + 2 * p + 1]) * LOG2E
        return weights(q_rows, sk_s[p, rows, :], csk_s[p] if has_ctx else None, valid, sink2)

    def swa_out(p, w):
        outs = weighted_values(w, lambda j: (sv_s[2 * p + j, rows, :], csv_s[2 * p + j] if has_ctx else None))
        y_ref[:, p * LANES:(p + 1) * LANES] = jnp.where(low, outs[0], outs[1]).astype(BF16)

    def diff_weights(p):
        q = qa_ref[:, GROUP_W + p * LANES:GROUP_W + (p + 1) * LANES]
        q_rows = jnp.concatenate([jnp.where(grp == g, q, 0.0).astype(BF16) for g in range(4)], axis=0)
        return weights(q_rows, dk_s[p], cdk_s[p] if has_ctx else None, None, None)

    def diff_out(p, w):
        parts = weighted_values(w, lambda j: (dv_s[2 * p + j], cdv_s[2 * p + j] if has_ctx else None))
        outs = [r[0:tq] - lam * r[tq:2 * tq] for r in parts]
        o = jnp.where(low, outs[0], outs[1])
        sq = o * o
        ms = jnp.where(low, jnp.sum(jnp.where(low, sq, 0.0), -1, keepdims=True),
                       jnp.sum(jnp.where(low, 0.0, sq), -1, keepdims=True)) * (1.0 / HEAD_DIM)
        y = o * lax.rsqrt(ms + EPS) * ng_ref[:, p * LANES:(p + 1) * LANES] * coef
        y_ref[:, GROUP_W + p * LANES:GROUP_W + (p + 1) * LANES] = y.astype(BF16)

    return [(swa_weights, swa_out, 0), (diff_weights, diff_out, 0),
            (swa_weights, swa_out, 1), (diff_weights, diff_out, 1)]


def _attention(scal, qa, kvs, kvd, norm_g_tiled, l, seq_len, caches=None):
    t = qa.shape[0]
    b = t // seq_len
    tq = min(ATTN_Q_TILE, seq_len)
    nq = seq_len // tq
    has_ctx = caches is not None
    ns = LATENT_SEQS_PER_STEP if has_ctx else CTX_SEQS_PER_STEP
    in_specs = [
        pl.BlockSpec(memory_space=pltpu.SMEM),
        pl.BlockSpec((ns, tq, 2 * GROUP_W), lambda i, j: (i, j, 0)),
        pl.BlockSpec((ns, seq_len, GROUP_W), lambda i, j: (i, 0, 0)),
        pl.BlockSpec((ns, seq_len, 2 * GROUP_W), lambda i, j: (i, 0, 0)),
    ]
    args = [scal, qa.reshape(b, seq_len, 2 * GROUP_W), kvs.reshape(b, seq_len, GROUP_W),
            kvd.reshape(b, seq_len, 2 * GROUP_W)]
    heads = GROUP_W // HEAD_DIM
    scratch = [pltpu.VMEM((ns, 2, seq_len, LANES), BF16), pltpu.VMEM((ns, heads, seq_len, LANES), BF16),
               pltpu.VMEM((ns, 2, seq_len, LANES), BF16), pltpu.VMEM((ns, heads, seq_len, LANES), BF16)]
    if has_ctx:
        past = caches[0].shape[2]
        for a in caches:
            in_specs.append(pl.BlockSpec((ns, None) + a.shape[2:], lambda i, j: (i, l, 0, 0)))
            args.append(a)
        scratch += [pltpu.VMEM((ns, 2, past, LANES), BF16), pltpu.VMEM((ns, heads, past, LANES), BF16),
                    pltpu.VMEM((ns, 2, past, LANES), BF16), pltpu.VMEM((ns, heads, past, LANES), BF16)]
    in_specs.append(pl.BlockSpec(norm_g_tiled.shape, lambda i, j: (0, 0)))
    args.append(norm_g_tiled)
    y = pl.pallas_call(
        functools.partial(_attention_kernel, seq_len=seq_len, has_ctx=has_ctx, seqs_per_step=ns),
        grid=(b // ns, nq),
        in_specs=in_specs,
        out_specs=pl.BlockSpec((ns, tq, 2 * GROUP_W), lambda i, j: (i, j, 0)),
        out_shape=jax.ShapeDtypeStruct((b, seq_len, 2 * GROUP_W), BF16),
        scratch_shapes=scratch,
        compiler_params=pltpu.CompilerParams(
            dimension_semantics=("arbitrary", "arbitrary"), vmem_limit_bytes=VMEM_LIMIT),
        name="attention_latent" if has_ctx else "attention_ctx",
    )(*args)
    return y.reshape(t, 2 * GROUP_W)


def _out_mlp_kernel(x_ref, yr_ref, ya_ref, mod_ref, wo_ref, g_ref, w1_ref, w2_ref, gf_ref, o_ref, *, final):
    g1 = mod_ref[:, 2 * D_MODEL:3 * D_MODEL]
    sh2 = mod_ref[:, 3 * D_MODEL:4 * D_MODEL]
    sc2 = mod_ref[:, 4 * D_MODEL:5 * D_MODEL]
    g2 = mod_ref[:, 5 * D_MODEL:6 * D_MODEL]
    y = _dot(yr_ref[...], wo_ref[0:2 * GROUP_W, :]) + _dot(ya_ref[...], wo_ref[2 * GROUP_W:4 * GROUP_W, :])
    x = x_ref[...] + g1 * y
    h = (_rms(x) * g_ref[...] * (1.0 + sc2) + sh2).astype(BF16)
    acc = None
    for c in range(D_FF // D_MODEL):
        sl = slice(c * D_MODEL, (c + 1) * D_MODEL)
        hid = jnp.square(jnp.maximum(_dot(h, w1_ref[:, sl]), 0.0)).astype(BF16)
        part = _dot(hid, w2_ref[sl, :])
        acc = part if acc is None else acc + part
    x = x + g2 * acc
    if final:
        x = _rms(x) * gf_ref[...]
    o_ref[...] = x


def _out_mlp(x, y_rec, y_att, mod4, w_out_b, norm_g, w_ff1_b, w_ff2_b, final_g, l, seq_len, is_latent, final):
    t = x.shape[0]
    tm = ROW_TILE
    if is_latent:
        tiles_per_seq = seq_len // tm
        mod_map = lambda i: (l, 1 + i // tiles_per_seq, 0, 0)
    else:
        mod_map = lambda i: (l, 0, 0, 0)
    once = pl.Buffered(1)
    in_specs = [
        pl.BlockSpec((tm, D_MODEL), lambda i: (i, 0)),
        pl.BlockSpec((tm, 2 * GROUP_W), lambda i: (i, 0)),
        pl.BlockSpec((tm, 2 * GROUP_W), lambda i: (i, 0)),
        pl.BlockSpec((None, None, 1, N_MOD * D_MODEL), mod_map),
        pl.BlockSpec((None, D_MODEL, D_MODEL), lambda i: (l, 0, 0), pipeline_mode=once),
        pl.BlockSpec((None, 1, D_MODEL), lambda i: (l, 0, 0)),
        pl.BlockSpec((None, D_MODEL, D_FF), lambda i: (l, 0, 0), pipeline_mode=once),
        pl.BlockSpec((None, D_FF, D_MODEL), lambda i: (l, 0, 0), pipeline_mode=once),
        pl.BlockSpec((1, D_MODEL), lambda i: (0, 0)),
    ]
    return pl.pallas_call(
        functools.partial(_out_mlp_kernel, final=final),
        grid=(t // tm,),
        in_specs=in_specs,
        out_specs=pl.BlockSpec((tm, D_MODEL), lambda i: (i, 0)),
        out_shape=jax.ShapeDtypeStruct((t, D_MODEL), F32),
        compiler_params=pltpu.CompilerParams(
            dimension_semantics=("arbitrary",), vmem_limit_bytes=VMEM_LIMIT),
        name="out_mlp_final" if final else "out_mlp",
    )(x, y_rec, y_att, mod4, w_out_b, norm_g, w_ff1_b, w_ff2_b, final_g)


def _rope_tables(length):
    row = (jnp.arange(length) // GRID_W).astype(F32)
    col = (jnp.arange(length) % GRID_W).astype(F32)
    tabs = []
    for dim in (HEAD_DIM, DIFF_QK_DIM):
        n = dim // 4
        inv = ROPE_BASE ** (-jnp.arange(n, dtype=F32) / n)
        ang = jnp.concatenate([row[:, None] * inv, col[:, None] * inv], -1)
        cos, sin = jnp.cos(ang), jnp.sin(ang)
        reps = LANES // dim
        tabs.append(jnp.tile(jnp.concatenate([cos, cos], -1), (1, reps)))
        tabs.append(jnp.tile(jnp.concatenate([-sin, sin], -1), (1, reps)))
    return tabs


def _block_diag(w):
    n, c, d = w.shape
    eye = jnp.eye(n, dtype=w.dtype)
    return jnp.einsum("ncd,nm->ncmd", w, eye).reshape(n * c, n * d)


def _layer_params(l, ret_decay, ret_gn_g, lru_conv_w, lru_conv_b, lru_w_a, lru_b_a, lru_w_x, lru_b_x, lru_lambda,
                  swa_sink, diff_lambda, diff_norm_g):
    heads = GROUP_W // HEAD_DIM
    log_gamma = jax.nn.log_sigmoid(ret_decay[l].astype(F32))
    lambda_init = 0.8 - 0.6 * math.exp(-0.3 * l)
    lv = diff_lambda[l].astype(F32)
    lam = jnp.exp(jnp.sum(lv[0] * lv[1])) - jnp.exp(jnp.sum(lv[2] * lv[3])) + lambda_init
    scal = jnp.concatenate([log_gamma[0], log_gamma[1], swa_sink[l].astype(F32),
                            jnp.stack([lam, jnp.asarray(1.0 - lambda_init, F32)]),
                            jnp.zeros((2,), F32)])
    wbd = jnp.concatenate([_block_diag(lru_w_a[l, 0]), _block_diag(lru_w_x[l, 0]),
                           _block_diag(lru_w_a[l, 1]), _block_diag(lru_w_x[l, 1])], axis=1).astype(BF16)
    bias = jnp.concatenate([lru_b_a[l, 0], lru_b_x[l, 0], lru_b_a[l, 1], lru_b_x[l, 1]])[None, :]
    return dict(
        scal=scal,
        lg_lane=jnp.repeat(log_gamma, HEAD_DIM, axis=1),
        gn_g=ret_gn_g[l][None, :],
        conv_w=lru_conv_w[l],
        conv_b=lru_conv_b[l][None, :],
        lru_wbd=wbd,
        lru_bias=bias,
        lru_lam=lru_lambda[l],
        diff_ng=jnp.tile(diff_norm_g[l], heads)[None, :],
    )


def kernel(x_prompt, x_sample, c, state_ret, state_lru, cache_swa_k, cache_swa_v, cache_diff_k, cache_diff_v,
           c_ctx, w_ada, b_ada, norm_mix_g, w_in, ret_decay, ret_gn_g, lru_conv_w, lru_conv_b,
           lru_w_a, lru_b_a, lru_w_x, lru_b_x, lru_lambda, swa_sink, diff_lambda, diff_norm_g,
           w_out, norm_mlp_g, w_ff1, w_ff2, final_norm_g):
    batch, seq, _ = x_prompt.shape
    dec_batch, dec_seq, _ = x_sample.shape
    past = cache_swa_k.shape[2]

    cvec = jnp.concatenate([c_ctx[None, :], c, jnp.zeros((MOD_ROWS - 1 - dec_batch, D_MODEL), F32)], axis=0)
    mod4 = _modulation(cvec, w_ada, b_ada).reshape(DEPTH, MOD_ROWS, 1, N_MOD * D_MODEL)

    w_in_b = w_in.astype(BF16)
    w_out_b = w_out.astype(BF16)
    w_ff1_b = w_ff1.astype(BF16)
    w_ff2_b = w_ff2.astype(BF16)
    norm_mix3 = norm_mix_g.reshape(DEPTH, 1, D_MODEL)
    norm_mlp3 = norm_mlp_g.reshape(DEPTH, 1, D_MODEL)
    final_g = final_norm_g[None, :]
    rope_tabs = _rope_tables(dec_seq)
    layers = [_layer_params(l, ret_decay, ret_gn_g, lru_conv_w, lru_conv_b, lru_w_a, lru_b_a, lru_w_x, lru_b_x,
                            lru_lambda, swa_sink, diff_lambda, diff_norm_g) for l in range(DEPTH)]

    heads = GROUP_W // HEAD_DIM

    xp = x_prompt.reshape(batch * seq, D_MODEL)
    new_ret, new_lru, new_swa, new_diff = [], [], [], []
    for l in range(DEPTH):
        lp = layers[l]
        zr, zl, qa, kvs, kvd = _in_proj(xp, mod4, norm_mix3, w_in_b, l, seq, None)
        y_rec, s_ret, s_lru = _recurrent(lp["scal"], zr, zl, lp, l, seq)
        y_att = _attention(lp["scal"], qa, kvs, kvd, lp["diff_ng"], l, seq)
        xp = _out_mlp(xp, y_rec, y_att, mod4, w_out_b, norm_mlp3, w_ff1_b, w_ff2_b, final_g,
                      l, seq, False, l == DEPTH - 1)
        new_ret.append(s_ret)
        new_lru.append(s_lru)
        new_swa.append(kvs)
        new_diff.append(kvd)
    y_prompt = xp.reshape(batch, seq, D_MODEL)
    new_ret = jnp.stack(new_ret, axis=1)
    new_swa = jnp.stack(new_swa, axis=1).reshape(batch, seq, DEPTH, 2, 2, HEAD_DIM)
    new_swa = jnp.transpose(new_swa, (3, 0, 2, 1, 4, 5))
    new_diff = jnp.stack(new_diff, axis=1).reshape(batch, seq, DEPTH, 2, heads, HEAD_DIM)
    new_diff = jnp.transpose(new_diff, (3, 0, 2, 1, 4, 5))

    xs = x_sample.reshape(dec_batch * dec_seq, D_MODEL)
    state = (state_ret, state_lru)
    caches = (cache_swa_k.reshape(dec_batch, DEPTH, past, 2 * HEAD_DIM),
              cache_swa_v.reshape(dec_batch, DEPTH, past, 2 * HEAD_DIM),
              cache_diff_k.reshape(dec_batch, DEPTH, past, GROUP_W),
              cache_diff_v.reshape(dec_batch, DEPTH, past, GROUP_W))
    for l in range(DEPTH):
        lp = layers[l]
        zr, zl, qa, kvs, kvd = _in_proj(xs, mod4, norm_mix3, w_in_b, l, dec_seq, rope_tabs)
        (y_rec,) = _recurrent(lp["scal"], zr, zl, lp, l, dec_seq, state)
        y_att = _attention(lp["scal"], qa, kvs, kvd, lp["diff_ng"], l, dec_seq, caches)
        xs = _out_mlp(xs, y_rec, y_att, mod4, w_out_b, norm_mlp3, w_ff1_b, w_ff2_b, final_g,
                      l, dec_seq, True, l == DEPTH - 1)
    y_sample = xs.reshape(dec_batch, dec_seq, D_MODEL)

    return (y_prompt, y_sample, new_ret, jnp.stack(new_lru, axis=1),
            new_swa[0], new_swa[1], new_diff[0], new_diff[1])
```

```python
import functools
import math

import jax
import jax.numpy as jnp
from jax import lax
from jax.experimental import pallas as pl
from jax.experimental.pallas import tpu as pltpu

F32 = jnp.float32
BF16 = jnp.bfloat16

D_MODEL = 1024
DEPTH = 4
GRID_W = 64
HEAD_DIM = 64
GROUP_W = D_MODEL // 4
RET_CHUNK = 256
LRU_C = 8.0
SWA_WINDOW = 128
DIFF_QK_DIM = HEAD_DIM // 2
D_FF = 4 * D_MODEL
ROPE_BASE = 10000.0
EPS = 1e-6
N_MOD = 6
D_IN = 11 * GROUP_W
MOD_ROWS = 16

LANES = 128
SUBLANES = 8
VMEM_LIMIT = 56 * 1024 * 1024
NEG_BIG = -1e30
LOG2E = math.log2(math.e)
SWA_QSCALE = (HEAD_DIM ** -0.5) * LOG2E
DIFF_QSCALE = (DIFF_QK_DIM ** -0.5) * LOG2E

ROW_TILE = 1024
ATTN_Q_TILE = 256
LATENT_SEQS_PER_STEP = 4
CTX_SEQS_PER_STEP = 8
ATTN_LOOKAHEAD_LATENT = 2
ATTN_LOOKAHEAD_CTX = 1


def _dot(a, b):
    return jnp.dot(a, b, preferred_element_type=F32)


def _dot_nt(a, b):
    return lax.dot_general(a, b, (((1,), (1,)), ((), ())), preferred_element_type=F32)


def _dot_tn(a, b):
    return lax.dot_general(a, b, (((0,), (0,)), ((), ())), preferred_element_type=F32)


def _rms(x):
    return x * lax.rsqrt(jnp.mean(x * x, axis=-1, keepdims=True) + EPS)


def _sigmoid(x):
    return 0.5 * jnp.tanh(0.5 * x) + 0.5


def _lane(shape):
    return lax.broadcasted_iota(jnp.int32, shape, len(shape) - 1)


def _mod_kernel(c_ref, w_ref, b_ref, o_ref):
    s = jax.nn.silu(c_ref[...]).astype(BF16)
    o_ref[...] = _dot(s, w_ref[...].astype(BF16)) + b_ref[...]


def _modulation(cvec, w_ada, b_ada):
    tn = 1536
    n_mod = N_MOD * D_MODEL
    return pl.pallas_call(
        _mod_kernel,
        grid=(DEPTH, n_mod // tn),
        in_specs=[
            pl.BlockSpec((MOD_ROWS, D_MODEL), lambda l, j: (0, 0)),
            pl.BlockSpec((None, D_MODEL, tn), lambda l, j: (l, 0, j)),
            pl.BlockSpec((None, 1, tn), lambda l, j: (l, 0, j)),
        ],
        out_specs=pl.BlockSpec((None, MOD_ROWS, tn), lambda l, j: (l, 0, j)),
        out_shape=jax.ShapeDtypeStruct((DEPTH, MOD_ROWS, n_mod), F32),
        compiler_params=pltpu.CompilerParams(
            dimension_semantics=("arbitrary", "arbitrary"), vmem_limit_bytes=VMEM_LIMIT),
        name="modulation",
    )(cvec, w_ada, b_ada.reshape(DEPTH, 1, n_mod))


def _rope_slab(x, c, s, half):
    first = (_lane(x.shape) & (2 * half - 1)) < half
    swapped = jnp.where(first, pltpu.roll(x, LANES - half, axis=1), pltpu.roll(x, half, axis=1))
    return x * c + swapped * s


def _in_proj_kernel(*refs, rope):
    if rope:
        (x_ref, mod_ref, g_ref, w_ref, c64_ref, s64_ref, c32_ref, s32_ref,
         zr_ref, zl_ref, qa_ref, kvs_ref, kvd_ref) = refs
    else:
        x_ref, mod_ref, g_ref, w_ref, zr_ref, zl_ref, qa_ref, kvs_ref, kvd_ref = refs
    sh1 = mod_ref[:, 0:D_MODEL]
    sc1 = mod_ref[:, D_MODEL:2 * D_MODEL]
    h = (_rms(x_ref[...]) * g_ref[...] * (1.0 + sc1) + sh1).astype(BF16)
    zs = _dot(h, w_ref[:, 6 * GROUP_W:8 * GROUP_W])
    zd = _dot(h, w_ref[:, 8 * GROUP_W:11 * GROUP_W])
    if rope:
        c64, s64, c32, s32 = c64_ref[...], s64_ref[...], c32_ref[...], s32_ref[...]
        rope_s = lambda v: _rope_slab(v, c64, s64, HEAD_DIM // 2)
        rope_d = lambda v: _rope_slab(v, c32, s32, DIFF_QK_DIM // 2)
    else:
        rope_s = rope_d = lambda v: v
    out = qa_ref.dtype
    for i in range(2):
        sl = slice(i * LANES, (i + 1) * LANES)
        qa_ref[:, sl] = (rope_s(zs[:, sl]) * SWA_QSCALE).astype(out)
        qa_ref[:, GROUP_W + i * LANES:GROUP_W + (i + 1) * LANES] = (rope_d(zd[:, sl]) * DIFF_QSCALE).astype(out)
        kvd_ref[:, sl] = rope_d(zd[:, GROUP_W + i * LANES:GROUP_W + (i + 1) * LANES]).astype(out)
    kvs_ref[:, 0:LANES] = rope_s(zs[:, 2 * LANES:3 * LANES]).astype(out)
    kvs_ref[:, LANES:2 * LANES] = zs[:, 3 * LANES:4 * LANES].astype(out)
    kvd_ref[:, GROUP_W:2 * GROUP_W] = zd[:, 2 * GROUP_W:3 * GROUP_W].astype(out)
    zl_ref[...] = _dot(h, w_ref[:, 4 * GROUP_W:6 * GROUP_W])
    zr_ref[...] = _dot(h, w_ref[:, 0:4 * GROUP_W])


def _in_proj(x, mod4, norm_g, w_in_b, l, seq_len, rope_tabs):
    t = x.shape[0]
    tm = ROW_TILE
    rope = rope_tabs is not None
    if rope:
        tiles_per_seq = seq_len // tm
        mod_map = lambda i: (l, 1 + i // tiles_per_seq, 0, 0)
    else:
        mod_map = lambda i: (l, 0, 0, 0)
    in_specs = [
        pl.BlockSpec((tm, D_MODEL), lambda i: (i, 0)),
        pl.BlockSpec((None, None, 1, N_MOD * D_MODEL), mod_map),
        pl.BlockSpec((None, 1, D_MODEL), lambda i: (l, 0, 0)),
        pl.BlockSpec((None, D_MODEL, D_IN), lambda i: (l, 0, 0)),
    ]
    args = [x, mod4, norm_g, w_in_b]
    if rope:
        tab_spec = pl.BlockSpec((tm, LANES), lambda i: (i % tiles_per_seq, 0))
        in_specs += [tab_spec] * 4
        args += list(rope_tabs)
    att = BF16 if rope else F32
    outs = ((4 * GROUP_W, F32), (2 * GROUP_W, F32), (2 * GROUP_W, att), (GROUP_W, att), (2 * GROUP_W, att))
    return pl.pallas_call(
        functools.partial(_in_proj_kernel, rope=rope),
        grid=(t // tm,),
        in_specs=in_specs,
        out_specs=[pl.BlockSpec((tm, w), lambda i: (i, 0)) for w, _ in outs],
        out_shape=[jax.ShapeDtypeStruct((t, w), dt) for w, dt in outs],
        compiler_params=pltpu.CompilerParams(
            dimension_semantics=("arbitrary",), vmem_limit_bytes=VMEM_LIMIT),
        name="in_proj_rope" if rope else "in_proj",
    )(*args)


def _scan_levels(a_slabs, u_slabs, pa_ref, pb_ref, sa_ref, sb_ref, h0, n0, reverse):
    order = list(range(SUBLANES - 1, -1, -1)) if reverse else list(range(SUBLANES))
    acc_a = acc_b = None
    for r in order:
        if acc_a is None:
            acc_a, acc_b = a_slabs[r], u_slabs[r]
        else:
            acc_b = a_slabs[r] * acc_b + u_slabs[r]
            acc_a = a_slabs[r] * acc_a
        pa_ref[r] = acc_a
        pb_ref[r] = acc_b
    pad = n0 // 2
    ident = pad + n0 if reverse else 0
    sa_ref[ident:ident + pad, :] = jnp.ones((pad, GROUP_W), F32)
    sb_ref[ident:ident + pad, :] = jnp.zeros((pad, GROUP_W), F32)
    sign = 1 if reverse else -1
    k = 1
    while k < n0:
        sa_ref[pad:pad + n0, :] = acc_a
        sb_ref[pad:pad + n0, :] = acc_b
        off = pad + sign * k
        acc_b = acc_a * sb_ref[off:off + n0, :] + acc_b
        acc_a = acc_a * sa_ref[off:off + n0, :]
        k *= 2
    sa_ref[pad:pad + n0, :] = acc_a
    sb_ref[pad:pad + n0, :] = acc_b
    off = pad + sign
    carry_in = sa_ref[off:off + n0, :] * h0 + sb_ref[off:off + n0, :]
    end = 0 if reverse else n0 - 1
    final = acc_a[end:end + 1, :] * h0 + acc_b[end:end + 1, :]
    h_slabs = [pa_ref[r] * carry_in + pb_ref[r] for r in range(SUBLANES)]
    return h_slabs, final


def _recurrent_kernel(*refs, seq_len, has_state):
    if has_state:
        (scal_ref, zr_ref, zl_ref, s0_ref, h0_ref, lgl_ref, gng_ref, cw_ref, cb_ref, wbd_ref, lb_ref, lam_ref,
         y_ref,
         d_ref, vec_ref, pad_ref, a_ref, u_ref, hf_ref, hs_ref, pa_ref, pb_ref, sa_ref, sb_ref) = refs
    else:
        (scal_ref, zr_ref, zl_ref, lgl_ref, gng_ref, cw_ref, cb_ref, wbd_ref, lb_ref, lam_ref,
         y_ref, sret_ref, slru_ref,
         d_ref, vec_ref, pad_ref, a_ref, u_ref, hf_ref, hs_ref, pa_ref, pb_ref, sa_ref, sb_ref) = refs
    ch = RET_CHUNK
    n_chunks = seq_len // ch
    n0 = seq_len // SUBLANES
    heads = GROUP_W // HEAD_DIM

    @pl.when(pl.program_id(0) == 0)
    def _():
        n = lax.broadcasted_iota(jnp.int32, (ch, ch), 0)
        m = lax.broadcasted_iota(jnp.int32, (ch, ch), 1)
        dist = (n - m).astype(F32)
        for h in range(heads):
            lg = jnp.where(dist > 0, scal_ref[h], scal_ref[heads + h])
            d_ref[h] = jnp.where(dist == 0, 2.0, jnp.exp(jnp.abs(dist) * lg))
        t = lax.broadcasted_iota(jnp.int32, (ch, GROUP_W), 0).astype(F32)
        lgf = lgl_ref[0:1, :]
        lgb = lgl_ref[1:2, :]
        vec_ref[0] = jnp.exp((t + 1.0) * lgf)
        vec_ref[1] = jnp.exp((ch - t) * lgb)
        vec_ref[2] = jnp.exp((ch - 1.0 - t) * lgf)
        vec_ref[3] = jnp.exp(t * lgb)

    low = _lane((1, LANES)) < HEAD_DIM
    rr = lax.broadcasted_iota(jnp.int32, (LANES, LANES), 0) < HEAD_DIM
    cc = lax.broadcasted_iota(jnp.int32, (LANES, LANES), 1) < HEAD_DIM
    block_diag = (rr == cc).astype(F32)
    g_chunk_f = jnp.exp(ch * lgl_ref[0:1, :])
    g_chunk_b = jnp.exp(ch * lgl_ref[1:2, :])

    def kv(c, p):
        rows = slice(c * ch, (c + 1) * ch)
        k = zr_ref[rows, GROUP_W + p * LANES:GROUP_W + (p + 1) * LANES] * (HEAD_DIM ** -0.5)
        v = zr_ref[rows, 2 * GROUP_W + p * LANES:2 * GROUP_W + (p + 1) * LANES].astype(BF16)
        return k, v

    def state_update(k, v, zeta):
        return _dot_tn((k * zeta).astype(BF16), v) * block_diag

    def pair_state(d, p):
        zero = jnp.zeros((HEAD_DIM, HEAD_DIM), F32)
        top = jnp.concatenate([s0_ref[d, 2 * p], zero], axis=1)
        bottom = jnp.concatenate([zero, s0_ref[d, 2 * p + 1]], axis=1)
        return jnp.concatenate([top, bottom], axis=0)

    def store_state(d, p, s_pair):
        sret_ref[d, 2 * p] = s_pair[0:HEAD_DIM, 0:HEAD_DIM]
        sret_ref[d, 2 * p + 1] = s_pair[HEAD_DIM:LANES, HEAD_DIM:LANES]

    for p in range(2):
        pl_sl = slice(p * LANES, (p + 1) * LANES)
        if has_state:
            s_b = [None] * n_chunks
            s_b[n_chunks - 1] = pair_state(1, p)
            for c in range(n_chunks - 1, 0, -1):
                k, v = kv(c, p)
                s_b[c - 1] = s_b[c] * g_chunk_b[:, pl_sl] + state_update(k, v, vec_ref[3, :, pl_sl])
            s_f = pair_state(0, p)
        for c in range(n_chunks):
            rows = slice(c * ch, (c + 1) * ch)
            q = zr_ref[rows, pl_sl]
            k, v = kv(c, p)
            kb = k.astype(BF16)
            outs = []
            for j in range(2):
                qm = jnp.where(low if j == 0 else ~low, q, 0.0).astype(BF16)
                w = (_dot_nt(qm, kb) * d_ref[2 * p + j]).astype(BF16)
                outs.append(_dot(w, v))
            o = jnp.where(low, outs[0], outs[1])
            if has_state:
                qb = q.astype(BF16)
                o = o + _dot(qb, s_f.astype(BF16)) * vec_ref[0, :, pl_sl]
                o = o + _dot(qb, s_b[c].astype(BF16)) * vec_ref[1, :, pl_sl]
                if c + 1 < n_chunks:
                    s_f = s_f * g_chunk_f[:, pl_sl] + state_update(k, v, vec_ref[2, :, pl_sl])
            else:
                store_state(0, p, state_update(k, v, vec_ref[2, :, pl_sl]))
                store_state(1, p, state_update(k, v, vec_ref[3, :, pl_sl]))
            inv = 1.0 / HEAD_DIM
            mu = jnp.where(low, jnp.sum(jnp.where(low, o, 0.0), -1, keepdims=True),
                           jnp.sum(jnp.where(low, 0.0, o), -1, keepdims=True)) * inv
            dlt = o - mu
            sq = dlt * dlt
            var = jnp.where(low, jnp.sum(jnp.where(low, sq, 0.0), -1, keepdims=True),
                            jnp.sum(jnp.where(low, 0.0, sq), -1, keepdims=True)) * inv
            gate = zr_ref[rows, 3 * GROUP_W + p * LANES:3 * GROUP_W + (p + 1) * LANES]
            y = dlt * lax.rsqrt(var + EPS) * gng_ref[:, pl_sl] * (gate * _sigmoid(gate))
            y_ref[rows, pl_sl] = y.astype(BF16)

    x = zl_ref[:, 0:GROUP_W]
    zero8 = jnp.zeros((SUBLANES, GROUP_W), F32)
    pad_ref[0:SUBLANES, :] = zero8
    pad_ref[SUBLANES:SUBLANES + seq_len, :] = x
    pad_ref[SUBLANES + seq_len:2 * SUBLANES + seq_len, :] = zero8
    cw = cw_ref[...]
    xc = (cw[0:1] * pad_ref[SUBLANES - 2:SUBLANES - 2 + seq_len, :]
          + cw[1:2] * pad_ref[SUBLANES - 1:SUBLANES - 1 + seq_len, :]
          + cw[2:3] * x
          + cw[3:4] * pad_ref[SUBLANES + 1:SUBLANES + 1 + seq_len, :]
          + cb_ref[...])
    pre = _dot(xc.astype(BF16), wbd_ref[...]) + lb_ref[...]
    for d in range(2):
        r_gate = _sigmoid(pre[:, (2 * d) * GROUP_W:(2 * d + 1) * GROUP_W])
        i_gate = _sigmoid(pre[:, (2 * d + 1) * GROUP_W:(2 * d + 2) * GROUP_W])
        log_a = -LRU_C * r_gate * jax.nn.softplus(-lam_ref[d:d + 1, :])
        a_val = jnp.exp(log_a)
        one_minus_a2 = -jnp.tanh(log_a) * (a_val * a_val + 1.0)
        root = jnp.where(one_minus_a2 > 0.0, one_minus_a2 * lax.rsqrt(one_minus_a2), 0.0)
        u_val = root * (i_gate * xc)
        for hh in range(2):
            a_ref[d, hh] = a_val[:, hh * LANES:(hh + 1) * LANES]
            u_ref[d, hh] = u_val[:, hh * LANES:(hh + 1) * LANES]

    def slab(ref, d, r):
        return jnp.concatenate([ref[d, hh, pl.ds(r, n0, stride=SUBLANES), :] for hh in range(2)], axis=1)

    finals = []
    for d in range(2):
        a_slabs = [slab(a_ref, d, r) for r in range(SUBLANES)]
        u_slabs = [slab(u_ref, d, r) for r in range(SUBLANES)]
        h0 = h0_ref[d:d + 1, :] if has_state else jnp.zeros((1, GROUP_W), F32)
        h_slabs, fin = _scan_levels(a_slabs, u_slabs, pa_ref, pb_ref, sa_ref, sb_ref, h0, n0, reverse=(d == 1))
        finals.append(fin)
        for r in range(SUBLANES):
            if d == 0:
                hf_ref[r] = h_slabs[r]
            else:
                h_sum = hf_ref[r] + h_slabs[r]
                for hh in range(2):
                    hs_ref[hh, pl.ds(r, n0, stride=SUBLANES), :] = h_sum[:, hh * LANES:(hh + 1) * LANES]
    h_both = jnp.concatenate([hs_ref[0], hs_ref[1]], axis=1)
    y = h_both * jax.nn.gelu(zl_ref[:, GROUP_W:2 * GROUP_W])
    y_ref[:, GROUP_W:2 * GROUP_W] = y.astype(BF16)
    if not has_state:
        slru_ref[0:1, :] = finals[0]
        slru_ref[1:2, :] = finals[1]


def _recurrent(scal, zr, zl, lp, l, seq_len, state=None):
    t = zr.shape[0]
    b = t // seq_len
    n0 = seq_len // SUBLANES
    has_state = state is not None
    assert has_state or seq_len == RET_CHUNK
    zr3 = zr.reshape(b, seq_len, 4 * GROUP_W)
    zl3 = zl.reshape(b, seq_len, 2 * GROUP_W)
    full = lambda a: pl.BlockSpec(a.shape, lambda i: (0,) * a.ndim)
    in_specs = [
        pl.BlockSpec(memory_space=pltpu.SMEM),
        pl.BlockSpec((None, seq_len, 4 * GROUP_W), lambda i: (i, 0, 0)),
        pl.BlockSpec((None, seq_len, 2 * GROUP_W), lambda i: (i, 0, 0)),
    ]
    args = [scal, zr3, zl3]
    if has_state:
        s0, h0 = state
        heads = GROUP_W // HEAD_DIM
        in_specs += [pl.BlockSpec((None, None, 2, heads, HEAD_DIM, HEAD_DIM), lambda i: (i, l, 0, 0, 0, 0)),
                     pl.BlockSpec((None, None, 2, GROUP_W), lambda i: (i, l, 0, 0))]
        args += [s0, h0]
    params = [lp["lg_lane"], lp["gn_g"], lp["conv_w"], lp["conv_b"], lp["lru_wbd"], lp["lru_bias"], lp["lru_lam"]]
    in_specs += [full(a) for a in params]
    args += params
    out_specs = [pl.BlockSpec((None, seq_len, 2 * GROUP_W), lambda i: (i, 0, 0))]
    out_shape = [jax.ShapeDtypeStruct((b, seq_len, 2 * GROUP_W), BF16)]
    if not has_state:
        heads = GROUP_W // HEAD_DIM
        out_specs += [pl.BlockSpec((None, 2, heads, HEAD_DIM, HEAD_DIM), lambda i: (i, 0, 0, 0, 0)),
                      pl.BlockSpec((None, 2, GROUP_W), lambda i: (i, 0, 0))]
        out_shape += [jax.ShapeDtypeStruct((b, 2, heads, HEAD_DIM, HEAD_DIM), F32),
                      jax.ShapeDtypeStruct((b, 2, GROUP_W), F32)]
    scratch = [
        pltpu.VMEM((GROUP_W // HEAD_DIM, RET_CHUNK, RET_CHUNK), F32),
        pltpu.VMEM((4, RET_CHUNK, GROUP_W), F32),
        pltpu.VMEM((seq_len + 2 * SUBLANES, GROUP_W), F32),
        pltpu.VMEM((2, 2, seq_len, LANES), F32),
        pltpu.VMEM((2, 2, seq_len, LANES), F32),
        pltpu.VMEM((SUBLANES, n0, GROUP_W), F32),
        pltpu.VMEM((2, seq_len, LANES), F32),
        pltpu.VMEM((SUBLANES, n0, GROUP_W), F32),
        pltpu.VMEM((SUBLANES, n0, GROUP_W), F32),
        pltpu.VMEM((2 * n0, GROUP_W), F32),
        pltpu.VMEM((2 * n0, GROUP_W), F32),
    ]
    outs = pl.pallas_call(
        functools.partial(_recurrent_kernel, seq_len=seq_len, has_state=has_state),
        grid=(b,),
        in_specs=in_specs,
        out_specs=out_specs,
        out_shape=out_shape,
        scratch_shapes=scratch,
        compiler_params=pltpu.CompilerParams(
            dimension_semantics=("arbitrary",), vmem_limit_bytes=VMEM_LIMIT),
        name="recurrent_latent" if has_state else "recurrent_ctx",
    )(*args)
    return (outs[0].reshape(t, 2 * GROUP_W),) + tuple(outs[1:])


def _dup_half(x, half):
    sw = pltpu.roll(x, HEAD_DIM, axis=1)
    low = _lane(x.shape) < HEAD_DIM
    return jnp.where(low, x, sw) if half == 0 else jnp.where(low, sw, x)


def _attention_kernel(*refs, seq_len, has_ctx, seqs_per_step):
    n_in = 9 if has_ctx else 5
    scal_ref, ng_ref = refs[0], refs[n_in - 1]
    per_seq = refs[1:n_in - 1] + refs[n_in:]
    per_sequence = [_attention_stages(scal_ref, ng_ref, *[r.at[s] for r in per_seq], seq_len=seq_len,
                                      has_ctx=has_ctx) for s in range(seqs_per_step)]
    stages = [st for seq_stages in per_sequence for st in seq_stages]
    pending = []
    for make_weights, make_out, p in stages:
        pending.append((make_out, p, make_weights(p)))
        if len(pending) > (ATTN_LOOKAHEAD_LATENT if has_ctx else ATTN_LOOKAHEAD_CTX):
            make_out_, p_, w_ = pending.pop(0)
            make_out_(p_, w_)
    for make_out_, p_, w_ in pending:
        make_out_(p_, w_)


def _attention_stages(scal_ref, ng_ref, *refs, seq_len, has_ctx):
    if has_ctx:
        (qa_ref, kvs_ref, kvd_ref, cks_ref, cvs_ref, ckd_ref, cvd_ref, y_ref,
         sk_s, sv_s, dk_s, dv_s, csk_s, csv_s, cdk_s, cdv_s) = refs
    else:
        (qa_ref, kvs_ref, kvd_ref, y_ref, sk_s, sv_s, dk_s, dv_s) = refs
    tq = min(ATTN_Q_TILE, seq_len)
    heads = GROUP_W // HEAD_DIM
    lane = _lane((1, LANES))
    low = lane < HEAD_DIM
    halves = (low, ~low)

    def build():
        def fill(k_all, v_all, k_dst, v_dst, dup):
            for p in range(2):
                if dup:
                    k_p, v_p = _dup_half(k_all, p), _dup_half(v_all, p)
                else:
                    k_p, v_p = k_all[:, p * LANES:(p + 1) * LANES], v_all[:, p * LANES:(p + 1) * LANES]
                k_dst[p] = k_p.astype(BF16)
                for j in range(2):
                    v_dst[2 * p + j] = jnp.where(halves[j], v_p, 1.0).astype(BF16)

        fill(kvs_ref[:, 0:LANES].astype(F32), kvs_ref[:, LANES:2 * LANES].astype(F32), sk_s, sv_s, True)
        fill(kvd_ref[:, 0:GROUP_W].astype(F32), kvd_ref[:, GROUP_W:2 * GROUP_W].astype(F32), dk_s, dv_s, False)
        if has_ctx:
            fill(cks_ref[...], cvs_ref[...], csk_s, csv_s, True)
            fill(ckd_ref[...], cvd_ref[...], cdk_s, cdv_s, False)

    if seq_len == tq:
        build()
    else:
        pl.when(pl.program_id(1) == 0)(build)

    def weights(q_rows, k_loc, k_ctx, valid, extra2):
        s = _dot_nt(q_rows, k_loc)
        if valid is not None:
            s = jnp.where(valid, s, NEG_BIG)
        m = jnp.max(s, -1, keepdims=True)
        sc = None
        if k_ctx is not None:
            sc = _dot_nt(q_rows, k_ctx)
            m = jnp.maximum(m, jnp.max(sc, -1, keepdims=True))
        if extra2 is not None:
            m = jnp.maximum(m, extra2)
        e = jnp.exp2((s - m).astype(BF16))
        ec = None if sc is None else jnp.exp2((sc - m).astype(BF16))
        extra = None if extra2 is None else jnp.exp2(extra2 - m)
        return e, ec, extra

    def weighted_values(w, v_of_head):
        e, ec, extra = w
        half_rows = e.shape[0] // 2
        res = []
        for j in range(2):
            rs = slice(j * half_rows, (j + 1) * half_rows)
            v_loc, v_ctx = v_of_head(j)
            o = _dot(e[rs], v_loc)
            if ec is not None:
                o = o + _dot(ec[rs], v_ctx)
            den = pltpu.roll(o, HEAD_DIM, axis=1)
            if extra is not None:
                den = den + extra[rs]
            res.append(o / jnp.where(halves[j], den, 1.0))
        return res

    if has_ctx:
        win = tq + 2 * SWA_WINDOW
        i = pl.program_id(1)
        blk = jnp.clip(i * (tq // SWA_WINDOW) - 1, 0, (seq_len - win) // SWA_WINDOW)
        start = pl.multiple_of(blk * SWA_WINDOW, SWA_WINDOW)
        rows = pl.ds(start, win)
        qpos = i * tq + (lax.broadcasted_iota(jnp.int32, (2 * tq, win), 0) & (tq - 1))
        kpos = start + lax.broadcasted_iota(jnp.int32, (2 * tq, win), 1)
        valid = jnp.abs(kpos - qpos) <= SWA_WINDOW
    else:
        rows = slice(None)
        valid = None
    first_member = lax.broadcasted_iota(jnp.int32, (2 * tq, 1), 0) < tq
    lam = scal_ref[3 * heads]
    coef = scal_ref[3 * heads + 1]
    grp = lane >> 5

    def swa_weights(p):
        q = qa_ref[:, p * LANES:(p + 1) * LANES]
        q_rows = jnp.concatenate([jnp.where(halves[j], q, 0.0).astype(BF16) for j in range(2)], axis=0)
        sink2 = jnp.where(first_member, scal_ref[2 * heads + 2 * p], scal_ref[2 * heads + 2 * p + 1]) * LOG2E
        return weights(q_rows, sk_s[p, rows, :], csk_s[p] if has_ctx else None, valid, sink2)

    def swa_out(p, w):
        outs = weighted_values(w, lambda j: (sv_s[2 * p + j, rows, :], csv_s[2 * p + j] if has_ctx else None))
        y_ref[:, p * LANES:(p + 1) * LANES] = jnp.where(low, outs[0], outs[1]).astype(BF16)

    first_member_out = {}

    def diff_weights(pj):
        p, j = pj
        q = qa_ref[:, GROUP_W + p * LANES:GROUP_W + (p + 1) * LANES]
        q_rows = jnp.concatenate([jnp.where(grp == 2 * j + c, q, 0.0).astype(BF16) for c in range(2)], axis=0)
        return weights(q_rows, dk_s[p], cdk_s[p] if has_ctx else None, None, None)

    def diff_out(pj, w):
        p, j = pj
        e, ec, _ = w
        o = _dot(e, dv_s[2 * p + j])
        if ec is not None:
            o = o + _dot(ec, cdv_s[2 * p + j])
        r = o / jnp.where(halves[j], pltpu.roll(o, HEAD_DIM, axis=1), 1.0)
        out = r[0:tq] - lam * r[tq:2 * tq]
        if j == 0:
            first_member_out[p] = out
            return
        o = jnp.where(low, first_member_out.pop(p), out)
        sq = o * o
        ms = jnp.where(low, jnp.sum(jnp.where(low, sq, 0.0), -1, keepdims=True),
                       jnp.sum(jnp.where(low, 0.0, sq), -1, keepdims=True)) * (1.0 / HEAD_DIM)
        y = o * lax.rsqrt(ms + EPS) * ng_ref[:, p * LANES:(p + 1) * LANES] * coef
        y_ref[:, GROUP_W + p * LANES:GROUP_W + (p + 1) * LANES] = y.astype(BF16)

    return [(swa_weights, swa_out, 0), (diff_weights, diff_out, (0, 0)), (diff_weights, diff_out, (0, 1)),
            (swa_weights, swa_out, 1), (diff_weights, diff_out, (1, 0)), (diff_weights, diff_out, (1, 1))]


def _attention(scal, qa, kvs, kvd, norm_g_tiled, l, seq_len, caches=None):
    t = qa.shape[0]
    b = t // seq_len
    tq = min(ATTN_Q_TILE, seq_len)
    nq = seq_len // tq
    has_ctx = caches is not None
    ns = LATENT_SEQS_PER_STEP if has_ctx else CTX_SEQS_PER_STEP
    in_specs = [
        pl.BlockSpec(memory_space=pltpu.SMEM),
        pl.BlockSpec((ns, tq, 2 * GROUP_W), lambda i, j: (i, j, 0)),
        pl.BlockSpec((ns, seq_len, GROUP_W), lambda i, j: (i, 0, 0)),
        pl.BlockSpec((ns, seq_len, 2 * GROUP_W), lambda i, j: (i, 0, 0)),
    ]
    args = [scal, qa.reshape(b, seq_len, 2 * GROUP_W), kvs.reshape(b, seq_len, GROUP_W),
            kvd.reshape(b, seq_len, 2 * GROUP_W)]
    heads = GROUP_W // HEAD_DIM
    scratch = [pltpu.VMEM((ns, 2, seq_len, LANES), BF16), pltpu.VMEM((ns, heads, seq_len, LANES), BF16),
               pltpu.VMEM((ns, 2, seq_len, LANES), BF16), pltpu.VMEM((ns, heads, seq_len, LANES), BF16)]
    if has_ctx:
        past = caches[0].shape[2]
        for a in caches:
            in_specs.append(pl.BlockSpec((ns, None) + a.shape[2:], lambda i, j: (i, l, 0, 0)))
            args.append(a)
        scratch += [pltpu.VMEM((ns, 2, past, LANES), BF16), pltpu.VMEM((ns, heads, past, LANES), BF16),
                    pltpu.VMEM((ns, 2, past, LANES), BF16), pltpu.VMEM((ns, heads, past, LANES), BF16)]
    in_specs.append(pl.BlockSpec(norm_g_tiled.shape, lambda i, j: (0, 0)))
    args.append(norm_g_tiled)
    y = pl.pallas_call(
        functools.partial(_attention_kernel, seq_len=seq_len, has_ctx=has_ctx, seqs_per_step=ns),
        grid=(b // ns, nq),
        in_specs=in_specs,
        out_specs=pl.BlockSpec((ns, tq, 2 * GROUP_W), lambda i, j: (i, j, 0)),
        out_shape=jax.ShapeDtypeStruct((b, seq_len, 2 * GROUP_W), BF16),
        scratch_shapes=scratch,
        compiler_params=pltpu.CompilerParams(
            dimension_semantics=("arbitrary", "arbitrary"), vmem_limit_bytes=VMEM_LIMIT),
        name="attention_latent" if has_ctx else "attention_ctx",
    )(*args)
    return y.reshape(t, 2 * GROUP_W)


def _out_mlp_kernel(x_ref, yr_ref, ya_ref, mod_ref, wo_ref, g_ref, w1_ref, w2_ref, gf_ref, o_ref, *, final):
    g1 = mod_ref[:, 2 * D_MODEL:3 * D_MODEL]
    sh2 = mod_ref[:, 3 * D_MODEL:4 * D_MODEL]
    sc2 = mod_ref[:, 4 * D_MODEL:5 * D_MODEL]
    g2 = mod_ref[:, 5 * D_MODEL:6 * D_MODEL]
    y = _dot(yr_ref[...], wo_ref[0:2 * GROUP_W, :]) + _dot(ya_ref[...], wo_ref[2 * GROUP_W:4 * GROUP_W, :])
    x = x_ref[...] + g1 * y
    h = (_rms(x) * g_ref[...] * (1.0 + sc2) + sh2).astype(BF16)
    acc = None
    for c in range(D_FF // D_MODEL):
        sl = slice(c * D_MODEL, (c + 1) * D_MODEL)
        hid = jnp.square(jnp.maximum(_dot(h, w1_ref[:, sl]), 0.0)).astype(BF16)
        part = _dot(hid, w2_ref[sl, :])
        acc = part if acc is None else acc + part
    x = x + g2 * acc
    if final:
        x = _rms(x) * gf_ref[...]
    o_ref[...] = x


def _out_mlp(x, y_rec, y_att, mod4, w_out_b, norm_g, w_ff1_b, w_ff2_b, final_g, l, seq_len, is_latent, final):
    t = x.shape[0]
    tm = ROW_TILE
    if is_latent:
        tiles_per_seq = seq_len // tm
        mod_map = lambda i: (l, 1 + i // tiles_per_seq, 0, 0)
    else:
        mod_map = lambda i: (l, 0, 0, 0)
    once = pl.Buffered(1)
    in_specs = [
        pl.BlockSpec((tm, D_MODEL), lambda i: (i, 0)),
        pl.BlockSpec((tm, 2 * GROUP_W), lambda i: (i, 0)),
        pl.BlockSpec((tm, 2 * GROUP_W), lambda i: (i, 0)),
        pl.BlockSpec((None, None, 1, N_MOD * D_MODEL), mod_map),
        pl.BlockSpec((None, D_MODEL, D_MODEL), lambda i: (l, 0, 0), pipeline_mode=once),
        pl.BlockSpec((None, 1, D_MODEL), lambda i: (l, 0, 0)),
        pl.BlockSpec((None, D_MODEL, D_FF), lambda i: (l, 0, 0), pipeline_mode=once),
        pl.BlockSpec((None, D_FF, D_MODEL), lambda i: (l, 0, 0), pipeline_mode=once),
        pl.BlockSpec((1, D_MODEL), lambda i: (0, 0)),
    ]
    return pl.pallas_call(
        functools.partial(_out_mlp_kernel, final=final),
        grid=(t // tm,),
        in_specs=in_specs,
        out_specs=pl.BlockSpec((tm, D_MODEL), lambda i: (i, 0)),
        out_shape=jax.ShapeDtypeStruct((t, D_MODEL), F32),
        compiler_params=pltpu.CompilerParams(
            dimension_semantics=("arbitrary",), vmem_limit_bytes=VMEM_LIMIT),
        name="out_mlp_final" if final else "out_mlp",
    )(x, y_rec, y_att, mod4, w_out_b, norm_g, w_ff1_b, w_ff2_b, final_g)


def _rope_tables(length):
    row = (jnp.arange(length) // GRID_W).astype(F32)
    col = (jnp.arange(length) % GRID_W).astype(F32)
    tabs = []
    for dim in (HEAD_DIM, DIFF_QK_DIM):
        n = dim // 4
        inv = ROPE_BASE ** (-jnp.arange(n, dtype=F32) / n)
        ang = jnp.concatenate([row[:, None] * inv, col[:, None] * inv], -1)
        cos, sin = jnp.cos(ang), jnp.sin(ang)
        reps = LANES // dim
        tabs.append(jnp.tile(jnp.concatenate([cos, cos], -1), (1, reps)))
        tabs.append(jnp.tile(jnp.concatenate([-sin, sin], -1), (1, reps)))
    return tabs


def _block_diag(w):
    n, c, d = w.shape
    eye = jnp.eye(n, dtype=w.dtype)
    return jnp.einsum("ncd,nm->ncmd", w, eye).reshape(n * c, n * d)


def _layer_params(l, ret_decay, ret_gn_g, lru_conv_w, lru_conv_b, lru_w_a, lru_b_a, lru_w_x, lru_b_x, lru_lambda,
                  swa_sink, diff_lambda, diff_norm_g):
    heads = GROUP_W // HEAD_DIM
    log_gamma = jax.nn.log_sigmoid(ret_decay[l].astype(F32))
    lambda_init = 0.8 - 0.6 * math.exp(-0.3 * l)
    lv = diff_lambda[l].astype(F32)
    lam = jnp.exp(jnp.sum(lv[0] * lv[1])) - jnp.exp(jnp.sum(lv[2] * lv[3])) + lambda_init
    scal = jnp.concatenate([log_gamma[0], log_gamma[1], swa_sink[l].astype(F32),
                            jnp.stack([lam, jnp.asarray(1.0 - lambda_init, F32)]),
                            jnp.zeros((2,), F32)])
    wbd = jnp.concatenate([_block_diag(lru_w_a[l, 0]), _block_diag(lru_w_x[l, 0]),
                           _block_diag(lru_w_a[l, 1]), _block_diag(lru_w_x[l, 1])], axis=1).astype(BF16)
    bias = jnp.concatenate([lru_b_a[l, 0], lru_b_x[l, 0], lru_b_a[l, 1], lru_b_x[l, 1]])[None, :]
    return dict(
        scal=scal,
        lg_lane=jnp.repeat(log_gamma, HEAD_DIM, axis=1),
        gn_g=ret_gn_g[l][None, :],
        conv_w=lru_conv_w[l],
        conv_b=lru_conv_b[l][None, :],
        lru_wbd=wbd,
        lru_bias=bias,
        lru_lam=lru_lambda[l],
        diff_ng=jnp.tile(diff_norm_g[l], heads)[None, :],
    )


def kernel(x_prompt, x_sample, c, state_ret, state_lru, cache_swa_k, cache_swa_v, cache_diff_k, cache_diff_v,
           c_ctx, w_ada, b_ada, norm_mix_g, w_in, ret_decay, ret_gn_g, lru_conv_w, lru_conv_b,
           lru_w_a, lru_b_a, lru_w_x, lru_b_x, lru_lambda, swa_sink, diff_lambda, diff_norm_g,
           w_out, norm_mlp_g, w_ff1, w_ff2, final_norm_g):
    batch, seq, _ = x_prompt.shape
    dec_batch, dec_seq, _ = x_sample.shape
    past = cache_swa_k.shape[2]

    cvec = jnp.concatenate([c_ctx[None, :], c, jnp.zeros((MOD_ROWS - 1 - dec_batch, D_MODEL), F32)], axis=0)
    mod4 = _modulation(cvec, w_ada, b_ada).reshape(DEPTH, MOD_ROWS, 1, N_MOD * D_MODEL)

    w_in_b = w_in.astype(BF16)
    w_out_b = w_out.astype(BF16)
    w_ff1_b = w_ff1.astype(BF16)
    w_ff2_b = w_ff2.astype(BF16)
    norm_mix3 = norm_mix_g.reshape(DEPTH, 1, D_MODEL)
    norm_mlp3 = norm_mlp_g.reshape(DEPTH, 1, D_MODEL)
    final_g = final_norm_g[None, :]
    rope_tabs = _rope_tables(dec_seq)
    layers = [_layer_params(l, ret_decay, ret_gn_g, lru_conv_w, lru_conv_b, lru_w_a, lru_b_a, lru_w_x, lru_b_x,
                            lru_lambda, swa_sink, diff_lambda, diff_norm_g) for l in range(DEPTH)]

    heads = GROUP_W // HEAD_DIM

    xp = x_prompt.reshape(batch * seq, D_MODEL)
    new_ret, new_lru, new_swa, new_diff = [], [], [], []
    for l in range(DEPTH):
        lp = layers[l]
        zr, zl, qa, kvs, kvd = _in_proj(xp, mod4, norm_mix3, w_in_b, l, seq, None)
        y_rec, s_ret, s_lru = _recurrent(lp["scal"], zr, zl, lp, l, seq)
        y_att = _attention(lp["scal"], qa, kvs, kvd, lp["diff_ng"], l, seq)
        xp = _out_mlp(xp, y_rec, y_att, mod4, w_out_b, norm_mlp3, w_ff1_b, w_ff2_b, final_g,
                      l, seq, False, l == DEPTH - 1)
        new_ret.append(s_ret)
        new_lru.append(s_lru)
        new_swa.append(kvs)
        new_diff.append(kvd)
    y_prompt = xp.reshape(batch, seq, D_MODEL)
    new_ret = jnp.stack(new_ret, axis=1)
    new_swa = jnp.stack(new_swa, axis=1).reshape(batch, seq, DEPTH, 2, 2, HEAD_DIM)
    new_swa = jnp.transpose(new_swa, (3, 0, 2, 1, 4, 5))
    new_diff = jnp.stack(new_diff, axis=1).reshape(batch, seq, DEPTH, 2, heads, HEAD_DIM)
    new_diff = jnp.transpose(new_diff, (3, 0, 2, 1, 4, 5))

    xs = x_sample.reshape(dec_batch * dec_seq, D_MODEL)
    state = (state_ret, state_lru)
    caches = (cache_swa_k.reshape(dec_batch, DEPTH, past, 2 * HEAD_DIM),
              cache_swa_v.reshape(dec_batch, DEPTH, past, 2 * HEAD_DIM),
              cache_diff_k.reshape(dec_batch, DEPTH, past, GROUP_W),
              cache_diff_v.reshape(dec_batch, DEPTH, past, GROUP_W))
    for l in range(DEPTH):
        lp = layers[l]
        zr, zl, qa, kvs, kvd = _in_proj(xs, mod4, norm_mix3, w_in_b, l, dec_seq, rope_tabs)
        (y_rec,) = _recurrent(lp["scal"], zr, zl, lp, l, dec_seq, state)
        y_att = _attention(lp["scal"], qa, kvs, kvd, lp["diff_ng"], l, dec_seq, caches)
        xs = _out_mlp(xs, y_rec, y_att, mod4, w_out_b, norm_mlp3, w_ff1_b, w_ff2_b, final_g,
                      l, dec_seq, True, l == DEPTH - 1)
    y_sample = xs.reshape(dec_batch, dec_seq, D_MODEL)

    return (y_prompt, y_sample, new_ret, jnp.stack(new_lru, axis=1),
            new_swa[0], new_swa[1], new_diff[0], new_diff[1])
```
